```python
import math
import jax
import jax.numpy as jnp
from jax import lax
import numpy as np

D_MODEL = 1024
BATCH = 4
SEQ = 4096
DEPTH = 2

HEAD_DIM = 64
A_HEADS = 4
MOBA_BLOCK = 256
MOBA_TOPK = 3
MOBA_Q_CHUNK = 128
B_HEADS = 8
B_KV_HEADS = 2
WINDOW = 128
C_HEADS = 4
SB_Q_BLOCK = 128
MEM_LEN = 256
X_HEADS = 4
X_HEAD_DIM = 128
D_FF = 2816
N_EXPERTS = 8
TOP_K = 2
D_FF_EXPERT = 3584

ROPE_THETA = 10000.0
EPS = 1e-6
N_BRANCH = 3

A_W = A_HEADS * HEAD_DIM
B_QW = B_HEADS * HEAD_DIM
B_KVW = B_KV_HEADS * HEAD_DIM
C_W = C_HEADS * HEAD_DIM
X_W = X_HEADS * X_HEAD_DIM
IN_W = 3 * A_W + B_QW + 2 * B_KVW + 3 * C_W + N_BRANCH * D_MODEL
N_DENSE = (DEPTH + 1) // 2
N_MOE = DEPTH // 2

kernel_name = "hybrid_moba_swa_stickbreak_gated_moe"

F32 = jnp.float32


def rms_norm(x, g):
    x32 = x.astype(F32)
    y = x32 * lax.rsqrt(jnp.mean(x32 * x32, axis=-1, keepdims=True) + EPS)
    return y.astype(x.dtype) * g


def split_heads(t, n_heads):
    b, s, _ = t.shape
    return t.reshape(b, s, n_heads, -1).transpose(0, 2, 1, 3)


def merge_heads(t):
    b, h, s, dh = t.shape
    return t.transpose(0, 2, 1, 3).reshape(b, s, h * dh)


def rope(x):
    s, dh = x.shape[2], x.shape[3]
    half = dh // 2
    inv_freq = ROPE_THETA ** (-jnp.arange(half, dtype=F32) / half)
    ang = jnp.arange(s, dtype=F32)[:, None] * inv_freq[None, :]
    cos = jnp.cos(ang).astype(x.dtype)
    sin = jnp.sin(ang).astype(x.dtype)
    x1, x2 = x[..., :half], x[..., half:]
    return jnp.concatenate([x1 * cos - x2 * sin, x2 * cos + x1 * sin], axis=-1)


def moba_attention(q, k, v):
    b, h, s, dh = q.shape
    scale = dh ** -0.5
    nb = -(-s // MOBA_BLOCK)
    pad = nb * MOBA_BLOCK - s
    kp = jnp.pad(k, ((0, 0), (0, 0), (0, pad), (0, 0))).reshape(b, h, nb, MOBA_BLOCK, dh)
    vp = jnp.pad(v, ((0, 0), (0, 0), (0, pad), (0, 0))).reshape(b, h, nb, MOBA_BLOCK, dh)
    k_mean = jnp.mean(kp.astype(F32), axis=3).astype(q.dtype)
    k_sel = min(MOBA_TOPK, nb)
    n_chunks = s // MOBA_Q_CHUNK
    qc = q.reshape(b, h, n_chunks, MOBA_Q_CHUNK, dh).transpose(2, 0, 1, 3, 4)
    offs = jnp.arange(MOBA_Q_CHUNK)
    blk_ids = jnp.arange(nb)
    bi = jnp.arange(b)[:, None, None, None]
    hi = jnp.arange(h)[None, :, None, None]

    def chunk(args):
        qi, ci = args
        t = ci * MOBA_Q_CHUNK + offs
        own = (ci * MOBA_Q_CHUNK) // MOBA_BLOCK
        gate = jnp.einsum('bhcd,bhnd->bhcn', qi, k_mean, preferred_element_type=F32)
        gate = jnp.where(blk_ids < own, gate, -jnp.inf)
        _, top_idx = lax.top_k(gate, k_sel)
        rank_ok = jnp.arange(k_sel) < own
        kg = kp[bi, hi, top_idx]
        vg = vp[bi, hi, top_idx]
        lp = jnp.einsum('bhcd,bhcnjd->bhcnj', qi, kg, preferred_element_type=F32) * scale
        lp = jnp.where(rank_ok[None, None, None, :, None], lp, -jnp.inf)
        k_own = lax.dynamic_index_in_dim(kp, own, axis=2, keepdims=False)
        v_own = lax.dynamic_index_in_dim(vp, own, axis=2, keepdims=False)
        lo = jnp.einsum('bhcd,bhjd->bhcj', qi, k_own, preferred_element_type=F32) * scale
        kpos = own * MOBA_BLOCK + jnp.arange(MOBA_BLOCK)
        lo = jnp.where(kpos[None, :] <= t[:, None], lo, -jnp.inf)
        c = qi.shape[2]
        logits = jnp.concatenate([lp.reshape(b, h, c, k_sel * MOBA_BLOCK), lo], axis=-1)
        probs = jax.nn.softmax(logits, axis=-1).astype(v.dtype)
        pp = probs[..., :k_sel * MOBA_BLOCK].reshape(b, h, c, k_sel, MOBA_BLOCK)
        po = probs[..., k_sel * MOBA_BLOCK:]
        return (jnp.einsum('bhcnj,bhcnjd->bhcd', pp, vg)
                + jnp.einsum('bhcj,bhjd->bhcd', po, v_own))

    out = lax.map(chunk, (qc, jnp.arange(n_chunks)))
    return out.transpose(1, 2, 0, 3, 4).reshape(b, h, s, dh)


def swa_sink_attention(q, k, v, sinks):
    b, hq, s, dh = q.shape
    hkv = k.shape[1]
    g = hq // hkv
    nblk = s // WINDOW
    scale = dh ** -0.5
    qb = q.reshape(b, hkv, g, nblk, WINDOW, dh)

    def band(t):
        tb = t.reshape(b, hkv, nblk, WINDOW, dh)
        prev = jnp.pad(tb, ((0, 0), (0, 0), (1, 0), (0, 0), (0, 0)))[:, :, :nblk]
        return jnp.concatenate([prev, tb], axis=3)

    kband, vband = band(k), band(v)
    logits = jnp.einsum('bkgnqd,bknjd->bkgnqj', qb, kband, preferred_element_type=F32) * scale
    qi = jnp.arange(WINDOW)[:, None] + WINDOW
    kj = jnp.arange(2 * WINDOW)[None, :]
    rel = qi - kj
    in_window = (rel >= 0) & (rel < WINDOW)
    has_prev = (jnp.arange(nblk)[:, None, None] > 0) | (kj[None] >= WINDOW)
    mask = in_window[None] & has_prev
    logits = jnp.where(mask, logits, -jnp.inf)
    sink = jnp.broadcast_to(sinks.astype(F32).reshape(1, hkv, g, 1, 1, 1), logits.shape[:-1] + (1,))
    probs = jax.nn.softmax(jnp.concatenate([logits, sink], axis=-1), axis=-1)[..., :-1]
    o = jnp.einsum('bkgnqj,bknjd->bkgnqd', probs.astype(v.dtype), vband)
    return o.reshape(b, hq, s, dh)


def stick_breaking_attention(q, k, v):
    b, h, s, dh = q.shape
    scale = dh ** -0.5
    nblk = s // SB_Q_BLOCK
    qb = q.reshape(b, h, nblk, SB_Q_BLOCK, dh).transpose(2, 0, 1, 3, 4)
    kpos = jnp.arange(s)
    offs = jnp.arange(SB_Q_BLOCK)

    def block(args):
        qi, i = args
        t = i * SB_Q_BLOCK + offs
        z = jnp.einsum('bhqd,bhkd->bhqk', qi, k, preferred_element_type=F32) * scale
        causal = kpos[None, :] < t[:, None]
        log_beta = jax.nn.log_sigmoid(z)
        log_keep = jnp.where(causal, jax.nn.log_sigmoid(-z), 0.0)
        after = lax.cumsum(log_keep, axis=3, reverse=True) - log_keep
        w = jnp.where(causal, jnp.exp(log_beta + after), 0.0)
        return jnp.einsum('bhqk,bhkd->bhqd', w.astype(v.dtype), v)

    out = lax.map(block, (qb, jnp.arange(nblk)))
    return out.transpose(1, 2, 0, 3, 4).reshape(b, h, s, dh)


def gated_mixer(h, w_in, w_proj_a, w_proj_b, w_proj_c, w_mix_out, sinks):
    proj = h @ w_in
    sizes = (A_W, A_W, A_W, B_QW, B_KVW, B_KVW, C_W, C_W, C_W)
    cuts = [int(c) for c in np.cumsum(sizes)]
    qa, ka, va, qb, kb, vb, qc, kc, vc, gates = jnp.split(proj, cuts, axis=-1)
    ya = merge_heads(moba_attention(rope(split_heads(qa, A_HEADS)),
                                    rope(split_heads(ka, A_HEADS)),
                                    split_heads(va, A_HEADS))) @ w_proj_a
    yb = merge_heads(swa_sink_attention(rope(split_heads(qb, B_HEADS)),
                                        rope(split_heads(kb, B_KV_HEADS)),
                                        split_heads(vb, B_KV_HEADS), sinks)) @ w_proj_b
    yc = merge_heads(stick_breaking_attention(split_heads(qc, C_HEADS),
                                              split_heads(kc, C_HEADS),
                                              split_heads(vc, C_HEADS))) @ w_proj_c
    ga, gb, gc = jnp.split(jax.nn.sigmoid(gates), N_BRANCH, axis=-1)
    return (ga * ya + gb * yb + gc * yc) @ w_mix_out


def memory_cross_attention(h, mem_n, w_xq, w_xkv, w_xo):
    q = split_heads(h @ w_xq, X_HEADS)
    k, v = jnp.split(mem_n @ w_xkv, 2, axis=-1)
    k = split_heads(k, X_HEADS)
    v = split_heads(v, X_HEADS)
    logits = jnp.einsum('bhsd,bhmd->bhsm', q, k, preferred_element_type=F32) * (X_HEAD_DIM ** -0.5)
    probs = jax.nn.softmax(logits, axis=-1).astype(v.dtype)
    return merge_heads(jnp.einsum('bhsm,bhmd->bhsd', probs, v)) @ w_xo


def swiglu(h, w_gate, w_up, w_down):
    return (jax.nn.silu(h @ w_gate) * (h @ w_up)) @ w_down


def moe_swiglu(h, w_router, w_gate, w_up, w_down):
    b, s, d = h.shape
    t = h.reshape(b * s, d)
    logits = (t @ w_router).astype(F32)
    top_val, top_idx = lax.top_k(logits, TOP_K)
    top_w = jax.nn.softmax(top_val, axis=-1)
    combine = jnp.sum(jax.nn.one_hot(top_idx, N_EXPERTS, dtype=F32) * top_w[..., None], axis=1)
    combine = combine.astype(t.dtype)
    out = jnp.zeros_like(t)
    for e in range(N_EXPERTS):
        out = out + combine[:, e:e + 1] * swiglu(t, w_gate[e], w_up[e], w_down[e])
    return out.reshape(b, s, d)


def setup_inputs(seed: int = 0) -> dict:
    key = jax.random.key(seed)
    ks = jax.random.split(key, 24)

    def w(k, shape, fan_in):
        return jax.random.normal(k, shape, F32) * (fan_in ** -0.5)

    def gain(k, shape):
        return 1.0 + 0.02 * jax.random.normal(k, shape, F32)

    return {
        "x": jax.random.normal(ks[0], (BATCH, SEQ, D_MODEL), F32),
        "mem": jax.random.normal(ks[1], (BATCH, MEM_LEN, D_MODEL), F32),
        "norm_mix": gain(ks[2], (DEPTH, D_MODEL)),
        "w_in": w(ks[3], (DEPTH, D_MODEL, IN_W), D_MODEL),
        "w_proj_a": w(ks[4], (DEPTH, A_W, D_MODEL), A_W),
        "w_proj_b": w(ks[5], (DEPTH, B_QW, D_MODEL), B_QW),
        "w_proj_c": w(ks[6], (DEPTH, C_W, D_MODEL), C_W),
        "w_mix_out": w(ks[7], (DEPTH, D_MODEL, D_MODEL), D_MODEL),
        "sinks": 0.5 * jax.random.normal(ks[8], (DEPTH, B_HEADS), F32),
        "norm_cross": gain(ks[9], (DEPTH, D_MODEL)),
        "norm_mem": gain(ks[10], (DEPTH, D_MODEL)),
        "w_xq": w(ks[11], (DEPTH, D_MODEL, X_W), D_MODEL),
        "w_xkv": w(ks[12], (DEPTH, D_MODEL, 2 * X_W), D_MODEL),
        "w_xo": w(ks[13], (DEPTH, X_W, D_MODEL), X_W),
        "norm_ffn": gain(ks[14], (DEPTH, D_MODEL)),
        "ffn_gate": w(ks[15], (N_DENSE, D_MODEL, D_FF), D_MODEL),
        "ffn_up": w(ks[16], (N_DENSE, D_MODEL, D_FF), D_MODEL),
        "ffn_down": w(ks[17], (N_DENSE, D_FF, D_MODEL), D_FF),
        "moe_router": w(ks[18], (N_MOE, D_MODEL, N_EXPERTS), D_MODEL),
        "moe_gate": w(ks[19], (N_MOE, N_EXPERTS, D_MODEL, D_FF_EXPERT), D_MODEL),
        "moe_up": w(ks[20], (N_MOE, N_EXPERTS, D_MODEL, D_FF_EXPERT), D_MODEL),
        "moe_down": w(ks[21], (N_MOE, N_EXPERTS, D_FF_EXPERT, D_MODEL), D_FF_EXPERT),
        "final_norm": gain(ks[22], (D_MODEL,)),
    }


def reference(x, mem, norm_mix, w_in, w_proj_a, w_proj_b, w_proj_c, w_mix_out, sinks,
              norm_cross, norm_mem, w_xq, w_xkv, w_xo, norm_ffn,
              ffn_gate, ffn_up, ffn_down, moe_router, moe_gate, moe_up, moe_down,
              final_norm):
    for l in range(DEPTH):
        h = rms_norm(x, norm_mix[l])
        x = x + gated_mixer(h, w_in[l], w_proj_a[l], w_proj_b[l], w_proj_c[l],
                            w_mix_out[l], sinks[l])
        h = rms_norm(x, norm_cross[l])
        mem_n = rms_norm(mem, norm_mem[l])
        x = x + memory_cross_attention(h, mem_n, w_xq[l], w_xkv[l], w_xo[l])
        h = rms_norm(x, norm_ffn[l])
        if l % 2 == 0:
            i = l // 2
            x = x + swiglu(h, ffn_gate[i], ffn_up[i], ffn_down[i])
        else:
            i = l // 2
            x = x + moe_swiglu(h, moe_router[i], moe_gate[i], moe_up[i], moe_down[i])
    return rms_norm(x, final_norm)
```

```python
import functools

import jax
import jax.numpy as jnp
from jax import lax
from jax.experimental import pallas as pl
from jax.experimental.pallas import tpu as pltpu

F32 = jnp.float32
BF16 = jnp.bfloat16
I32 = jnp.int32

D_MODEL = 1024
HEAD_DIM = 64
A_HEADS = 4
MOBA_BLOCK = 256
MOBA_TOPK = 3
B_HEADS = 8
B_KV_HEADS = 2
WINDOW = 128
C_HEADS = 4
SB_BLOCK = 256
MEM_LEN = 256
X_HEADS = 4
X_HEAD_DIM = 128
N_EXPERTS = 8
TOP_K = 2
ROPE_THETA = 10000.0
EPS = 1e-6

A_W = A_HEADS * HEAD_DIM
B_QW = B_HEADS * HEAD_DIM
B_KVW = B_KV_HEADS * HEAD_DIM
C_W = C_HEADS * HEAD_DIM
X_W = X_HEADS * X_HEAD_DIM
QKV_W = 3 * A_W + B_QW + 2 * B_KVW + 3 * C_W
IN_W = QKV_W + 3 * D_MODEL

LANES = 128
NEG = -1e30

COL_GA, COL_GB, COL_GC = 0, D_MODEL, 2 * D_MODEL
COL_QA = 3 * D_MODEL
COL_KA = COL_QA + A_W
COL_QB = COL_KA + A_W
COL_KB = COL_QB + B_QW
COL_VB = COL_KB + B_KVW
COL_VA = COL_VB + B_KVW
COL_QC = COL_VA + A_W
COL_KC = COL_QC + C_W
COL_VC = COL_KC + C_W
ROPE_LO, ROPE_HI = COL_QA, COL_VB

PROJ_TN = 768
VMEM_LIMIT = 48 * 1024 * 1024


def _params(*sem):
    return pltpu.CompilerParams(dimension_semantics=sem, vmem_limit_bytes=VMEM_LIMIT)


def _rms(x, g):
    ms = jnp.mean(x * x, axis=-1, keepdims=True)
    return x * lax.rsqrt(ms + EPS) * g


def _dot(a, b):
    return jnp.dot(a, b, preferred_element_type=F32)


def _dot_t(a, b):
    return lax.dot_general(a, b, (((1,), (1,)), ((), ())), preferred_element_type=F32)


def _split_bf16(v):
    hi = v.astype(BF16)
    lo = (v - hi.astype(F32)).astype(BF16)
    return hi, lo


def _norm_proj_kernel(x_ref, g_ref, w_ref, cos_ref, sin_ref, o_ref, h_scr, *, rope_groups):
    j = pl.program_id(1)

    @pl.when(j == 0)
    def _():
        h_scr[...] = _rms(x_ref[...], g_ref[...]).astype(BF16)

    acc = _dot(h_scr[...], w_ref[...])
    n_groups = acc.shape[1] // LANES

    def rope(y):
        lane = lax.broadcasted_iota(I32, y.shape, 1)
        first_half = (lane % HEAD_DIM) < (HEAD_DIM // 2)
        sw = jnp.where(first_half, pltpu.roll(y, LANES - HEAD_DIM // 2, 1),
                       pltpu.roll(y, HEAD_DIM // 2, 1))
        return y * cos_ref[...] + sw * sin_ref[...]

    roped = sorted(rope_groups)

    for t in roped:
        @pl.when(j == t)
        def _(t=t):
            for gi in range(n_groups):
                y = acc[:, gi * LANES:(gi + 1) * LANES]
                if gi < rope_groups[t]:
                    y = rope(y)
                o_ref[:, gi * LANES:(gi + 1) * LANES] = y.astype(o_ref.dtype)

    is_plain = j >= 0
    for t in roped:
        is_plain = jnp.logical_and(is_plain, j != t)

    @pl.when(is_plain)
    def _():
        o_ref[...] = acc.astype(o_ref.dtype)


def _norm_proj(x, gain, w, cos, sin, seq, *, tm, tn, rope_groups):
    t, d = x.shape
    n = w.shape[1]
    pos_blocks = seq // tm
    kern = functools.partial(_norm_proj_kernel, rope_groups=rope_groups)
    return pl.pallas_call(
        kern,
        grid=(t // tm, n // tn),
        in_specs=[
            pl.BlockSpec((tm, d), lambda i, j: (i, 0)),
            pl.BlockSpec((1, d), lambda i, j: (0, 0)),
            pl.BlockSpec((d, tn), lambda i, j: (0, j)),
            pl.BlockSpec((tm, LANES), lambda i, j: (i % pos_blocks, 0)),
            pl.BlockSpec((tm, LANES), lambda i, j: (i % pos_blocks, 0)),
        ],
        out_specs=pl.BlockSpec((tm, tn), lambda i, j: (i, j)),
        out_shape=jax.ShapeDtypeStruct((t, n), BF16),
        scratch_shapes=[pltpu.VMEM((tm, d), BF16)],
        compiler_params=_params("parallel", "arbitrary"),
        name="norm_proj",
    )(x, gain, w, cos, sin)


def _moba_kernel(q_ref, k_ref, v_ref, o_ref, kmean_scr, *, nblk):
    i = pl.program_id(2)
    blk = MOBA_BLOCK

    @pl.when(i == 0)
    def _():
        for n in range(nblk):
            kb = k_ref[n * blk:(n + 1) * blk, :].astype(F32)
            kmean_scr[n:n + 1, :] = jnp.mean(kb, axis=0, keepdims=True)

    row = lax.broadcasted_iota(I32, (blk, blk), 0)
    col = lax.broadcasted_iota(I32, (blk, blk), 1)
    causal = col <= row
    lane = lax.broadcasted_iota(I32, (blk, nblk), 1)
    own0 = pl.multiple_of(i * blk, blk)

    outs = []
    for hh in range(2):
        hs = slice(hh * HEAD_DIM, (hh + 1) * HEAD_DIM)
        q = q_ref[:, hs]
        km_hi, km_lo = _split_bf16(kmean_scr[:, hs])
        g = _dot_t(q, km_hi) + _dot_t(q, km_lo)

        cnt = jnp.zeros((blk, nblk), I32)
        for m in range(nblk):
            gm = g[:, m:m + 1]
            beats = (gm > g) | ((gm == g) & (m < lane))
            cnt = cnt + jnp.where(beats, (m < i).astype(I32), 0)
        sel = (cnt < MOBA_TOPK) & (lane < i)
        bias = jnp.where(sel, 0.0, NEG)

        s = _dot_t(q, k_ref[pl.ds(own0, blk), hs])
        s = jnp.where(causal, s, NEG)
        m0 = jnp.max(s, axis=-1, keepdims=True)
        p = jnp.exp(s - m0)
        l0 = jnp.sum(p, axis=-1, keepdims=True)
        acc0 = _dot(p.astype(BF16), v_ref[pl.ds(own0, blk), hs])

        def body(j, carry, q=q, bias=bias, hs=hs):
            m_run, l_run, acc = carry
            r0 = pl.multiple_of(j * blk, blk)
            bcol = jnp.sum(jnp.where(lane == j, bias, 0.0), axis=-1, keepdims=True)
            sj = _dot_t(q, k_ref[pl.ds(r0, blk), hs]) + bcol
            m_new = jnp.maximum(m_run, jnp.max(sj, axis=-1, keepdims=True))
            alpha = jnp.exp(m_run - m_new)
            pj = jnp.exp(sj - m_new)
            l_new = alpha * l_run + jnp.sum(pj, axis=-1, keepdims=True)
            acc_new = alpha * acc + _dot(pj.astype(BF16), v_ref[pl.ds(r0, blk), hs])
            return m_new, l_new, acc_new

        _, l_fin, acc_fin = lax.fori_loop(0, i, body, (m0, l0, acc0))
        outs.append(acc_fin / l_fin)

    o_ref[...] = jnp.concatenate(outs, axis=1).astype(o_ref.dtype)


def _moba(p, batch, seq):
    nblk = seq // MOBA_BLOCK
    cq, ck, cv = COL_QA // LANES, COL_KA // LANES, COL_VA // LANES
    kern = functools.partial(_moba_kernel, nblk=nblk)
    return pl.pallas_call(
        kern,
        grid=(batch, A_HEADS // 2, nblk),
        in_specs=[
            pl.BlockSpec((MOBA_BLOCK, LANES), lambda b, h, i: (b * nblk + i, cq + h)),
            pl.BlockSpec((seq, LANES), lambda b, h, i: (b, ck + h)),
            pl.BlockSpec((seq, LANES), lambda b, h, i: (b, cv + h)),
        ],
        out_specs=pl.BlockSpec((MOBA_BLOCK, LANES), lambda b, h, i: (b * nblk + i, h)),
        out_shape=jax.ShapeDtypeStruct((batch * seq, A_W), BF16),
        scratch_shapes=[pltpu.VMEM((nblk, LANES), F32)],
        compiler_params=_params("parallel", "parallel", "arbitrary"),
        name="moba",
    )(p, p, p)


def _swa_kernel(sinks_ref, q_ref, kv_ref, pkv_ref, o_ref, kv_scr, *, tq):
    i = pl.program_id(1)
    w = WINDOW
    kv_scr[0:w, :] = pkv_ref[...]
    kv_scr[w:, :] = kv_ref[...]

    row = lax.broadcasted_iota(I32, (w, 2 * w), 0)
    col = lax.broadcasted_iota(I32, (w, 2 * w), 1)
    rel = row + w - col
    band = (rel >= 0) & (rel < w)
    group = B_HEADS // B_KV_HEADS

    def body(n, carry):
        base = pl.multiple_of(n * w, w)
        kvt = kv_scr[pl.ds(base, 2 * w), :]
        qt = q_ref[pl.ds(base, w), :]
        col_min = jnp.where(jnp.logical_and(i == 0, n == 0), w, 0)
        mask = band & (col >= col_min)
        outs = []
        for kh in range(B_KV_HEADS):
            kk = kvt[:, kh * HEAD_DIM:(kh + 1) * HEAD_DIM]
            vv = kvt[:, B_KVW + kh * HEAD_DIM:B_KVW + (kh + 1) * HEAD_DIM]
            for gi in range(group):
                hd = kh * group + gi
                qh = qt[:, hd * HEAD_DIM:(hd + 1) * HEAD_DIM]
                s = jnp.where(mask, _dot_t(qh, kk), NEG)
                sink = sinks_ref[hd]
                m = jnp.maximum(jnp.max(s, axis=-1, keepdims=True), sink)
                p = jnp.exp(s - m)
                l = jnp.sum(p, axis=-1, keepdims=True) + jnp.exp(sink - m)
                outs.append(_dot(p.astype(BF16), vv) / l)
        o_ref[pl.ds(base, w), :] = jnp.concatenate(outs, axis=1).astype(o_ref.dtype)
        return carry

    lax.fori_loop(0, tq // w, body, 0)


def _swa(p, sinks, batch, seq, *, tq=512):
    nq = seq // tq
    sub = tq // WINDOW
    kern = functools.partial(_swa_kernel, tq=tq)
    return pl.pallas_call(
        kern,
        grid=(batch, nq),
        in_specs=[
            pl.BlockSpec(memory_space=pltpu.SMEM),
            pl.BlockSpec((tq, B_QW), lambda b, i: (b * nq + i, COL_QB // B_QW)),
            pl.BlockSpec((tq, 2 * B_KVW), lambda b, i: (b * nq + i, COL_KB // (2 * B_KVW))),
            pl.BlockSpec((WINDOW, 2 * B_KVW),
                         lambda b, i: (jnp.maximum((b * nq + i) * sub - 1, 0),
                                       COL_KB // (2 * B_KVW))),
        ],
        out_specs=pl.BlockSpec((tq, B_QW), lambda b, i: (b * nq + i, 0)),
        out_shape=jax.ShapeDtypeStruct((batch * seq, B_QW), BF16),
        scratch_shapes=[pltpu.VMEM((tq + WINDOW, 2 * B_KVW), BF16)],
        compiler_params=_params("parallel", "parallel"),
        name="swa",
    )(sinks, p, p, p)


def _stick_kernel(q_ref, k_ref, v_ref, o_ref):
    i = pl.program_id(2)
    blk = SB_BLOCK
    row = lax.broadcasted_iota(I32, (blk, blk), 0)
    col = lax.broadcasted_iota(I32, (blk, blk), 1)
    causal = col < row
    tri = jnp.where(row > col, 1.0, 0.0).astype(BF16)
    own0 = pl.multiple_of(i * blk, blk)

    def logs(z):
        t = jnp.log1p(jnp.exp(-jnp.abs(z)))
        log_beta = jnp.minimum(z, 0.0) - t
        return log_beta, log_beta - z

    def suffix(lk):
        hi, lo = _split_bf16(lk)
        after = _dot(hi, tri) + _dot(lo, tri)
        total = after[:, 0:1] + lk[:, 0:1]
        return after, total

    outs = []
    for hh in range(2):
        hs = slice(hh * HEAD_DIM, (hh + 1) * HEAD_DIM)
        q = q_ref[:, hs]

        z = _dot_t(q, k_ref[pl.ds(own0, blk), hs])
        log_beta, log_keep = logs(z)
        lk = jnp.where(causal, log_keep, 0.0)
        after, carry0 = suffix(lk)
        wgt = jnp.where(causal, jnp.exp(log_beta + after), 0.0)
        acc0 = _dot(wgt.astype(BF16), v_ref[pl.ds(own0, blk), hs])

        def body(jj, state, q=q, hs=hs):
            carry, acc = state
            r0 = pl.multiple_of((i - 1 - jj) * blk, blk)
            zj = _dot_t(q, k_ref[pl.ds(r0, blk), hs])
            lb, lkj = logs(zj)
            aft, tot = suffix(lkj)
            wj = jnp.exp(lb + aft + carry)
            acc = acc + _dot(wj.astype(BF16), v_ref[pl.ds(r0, blk), hs])
            return carry + tot, acc

        _, acc_fin = lax.fori_loop(0, i, body, (carry0, acc0))
        outs.append(acc_fin)

    o_ref[...] = jnp.concatenate(outs, axis=1).astype(o_ref.dtype)


def _stick(p, batch, seq):
    nblk = seq // SB_BLOCK
    cq, ck, cv = COL_QC // LANES, COL_KC // LANES, COL_VC // LANES
    return pl.pallas_call(
        _stick_kernel,
        grid=(batch, C_HEADS // 2, nblk),
        in_specs=[
            pl.BlockSpec((SB_BLOCK, LANES), lambda b, h, i: (b * nblk + i, cq + h)),
            pl.BlockSpec((seq, LANES), lambda b, h, i: (b, ck + h)),
            pl.BlockSpec((seq, LANES), lambda b, h, i: (b, cv + h)),
        ],
        out_specs=pl.BlockSpec((SB_BLOCK, LANES), lambda b, h, i: (b * nblk + i, h)),
        out_shape=jax.ShapeDtypeStruct((batch * seq, C_W), BF16),
        compiler_params=_params("parallel", "parallel", "parallel"),
        name="stick",
    )(p, p, p)


def _sigmoid(x):
    return 1.0 / (1.0 + jnp.exp(-x))


def _merge_kernel(x_ref, oa_ref, ob_ref, oc_ref, ga_ref, gb_ref, gc_ref,
                  wa_ref, wb_ref, wc_ref, wo_ref, o_ref):
    mixed = _sigmoid(ga_ref[...].astype(F32)) * _dot(oa_ref[...], wa_ref[...])
    mixed = mixed + _sigmoid(gb_ref[...].astype(F32)) * _dot(ob_ref[...], wb_ref[...])
    mixed = mixed + _sigmoid(gc_ref[...].astype(F32)) * _dot(oc_ref[...], wc_ref[...])
    o_ref[...] = x_ref[...] + _dot(mixed.astype(BF16), wo_ref[...])


def _merge(x, oa, ob, oc, p, wa, wb, wc, wo, *, tm=512):
    t, d = x.shape
    full = lambda a: pl.BlockSpec(a.shape, lambda i: (0, 0))
    return pl.pallas_call(
        _merge_kernel,
        grid=(t // tm,),
        in_specs=[
            pl.BlockSpec((tm, d), lambda i: (i, 0)),
            pl.BlockSpec((tm, A_W), lambda i: (i, 0)),
            pl.BlockSpec((tm, B_QW), lambda i: (i, 0)),
            pl.BlockSpec((tm, C_W), lambda i: (i, 0)),
            pl.BlockSpec((tm, d), lambda i: (i, COL_GA // D_MODEL)),
            pl.BlockSpec((tm, d), lambda i: (i, COL_GB // D_MODEL)),
            pl.BlockSpec((tm, d), lambda i: (i, COL_GC // D_MODEL)),
            full(wa), full(wb), full(wc), full(wo),
        ],
        out_specs=pl.BlockSpec((tm, d), lambda i: (i, 0)),
        out_shape=jax.ShapeDtypeStruct((t, d), F32),
        compiler_params=_params("parallel"),
        name="merge",
    )(x, oa, ob, oc, p, p, p, wa, wb, wc, wo)


def _cross_kernel(x_ref, g_ref, wq_ref, kv_ref, wo_ref, o_ref):
    x = x_ref[...]
    h = _rms(x, g_ref[...]).astype(BF16)
    q = (_dot(h, wq_ref[...]) * (X_HEAD_DIM ** -0.5)).astype(BF16)
    outs = []
    for hd in range(X_HEADS):
        hs = slice(hd * X_HEAD_DIM, (hd + 1) * X_HEAD_DIM)
        s = _dot_t(q[:, hs], kv_ref[:, hs])
        m = jnp.max(s, axis=-1, keepdims=True)
        p = jnp.exp(s - m)
        l = jnp.sum(p, axis=-1, keepdims=True)
        vs = slice(X_W + hd * X_HEAD_DIM, X_W + (hd + 1) * X_HEAD_DIM)
        outs.append(_dot(p.astype(BF16), kv_ref[:, vs]) / l)
    att = jnp.concatenate(outs, axis=1).astype(BF16)
    o_ref[...] = x + _dot(att, wo_ref[...])


def _cross(x, gain, wq, kv, wo, seq, *, tm=512):
    t, d = x.shape
    per_batch = seq // tm
    full = lambda a: pl.BlockSpec(a.shape, lambda i: (0, 0))
    return pl.pallas_call(
        _cross_kernel,
        grid=(t // tm,),
        in_specs=[
            pl.BlockSpec((tm, d), lambda i: (i, 0)),
            full(gain), full(wq),
            pl.BlockSpec((MEM_LEN, 2 * X_W), lambda i: (i // per_batch, 0)),
            full(wo),
        ],
        out_specs=pl.BlockSpec((tm, d), lambda i: (i, 0)),
        out_shape=jax.ShapeDtypeStruct((t, d), F32),
        compiler_params=_params("parallel"),
        name="cross",
    )(x, gain, wq, kv, wo)


def _silu(x):
    return x * _sigmoid(x)


def _ffn_kernel(x_ref, g_ref, wg_ref, wu_ref, wd_ref, o_ref, h_scr, acc_scr):
    f = pl.program_id(1)

    @pl.when(f == 0)
    def _():
        h_scr[...] = _rms(x_ref[...], g_ref[...]).astype(BF16)
        acc_scr[...] = x_ref[...]

    h = h_scr[...]
    a = _silu(_dot(h, wg_ref[...])) * _dot(h, wu_ref[...])
    acc_scr[...] += _dot(a.astype(BF16), wd_ref[...])

    @pl.when(f == pl.num_programs(1) - 1)
    def _():
        o_ref[...] = acc_scr[...]


def _ffn(x, gain, wg, wu, wd, *, tm=1024, tf=256):
    t, d = x.shape
    ff = wg.shape[1]
    return pl.pallas_call(
        _ffn_kernel,
        grid=(t // tm, ff // tf),
        in_specs=[
            pl.BlockSpec((tm, d), lambda i, f: (i, 0)),
            pl.BlockSpec((1, d), lambda i, f: (0, 0)),
            pl.BlockSpec((d, tf), lambda i, f: (0, f)),
            pl.BlockSpec((d, tf), lambda i, f: (0, f)),
            pl.BlockSpec((tf, d), lambda i, f: (f, 0)),
        ],
        out_specs=pl.BlockSpec((tm, d), lambda i, f: (i, 0)),
        out_shape=jax.ShapeDtypeStruct((t, d), F32),
        scratch_shapes=[pltpu.VMEM((tm, d), BF16), pltpu.VMEM((tm, d), F32)],
        compiler_params=_params("parallel", "arbitrary"),
        name="ffn",
    )(x, gain, wg, wu, wd)


def _router_kernel(x_ref, g_ref, wr_ref, o_ref):
    h = _rms(x_ref[...], g_ref[...])
    h_hi, h_lo = _split_bf16(h)
    w_hi, w_lo = _split_bf16(wr_ref[...])
    logits = _dot(h_hi, w_hi) + (_dot(h_hi, w_lo) + _dot(h_lo, w_hi))
    lane = lax.broadcasted_iota(I32, logits.shape, 1)
    logits = jnp.where(lane < N_EXPERTS, logits, NEG)
    v1 = jnp.max(logits, axis=-1, keepdims=True)
    i1 = jnp.min(jnp.where(logits == v1, lane, LANES), axis=-1, keepdims=True)
    rest = jnp.where(lane == i1, NEG, logits)
    v2 = jnp.max(rest, axis=-1, keepdims=True)
    i2 = jnp.min(jnp.where(rest == v2, lane, LANES), axis=-1, keepdims=True)
    e = jnp.exp(v2 - v1)
    w1 = 1.0 / (1.0 + e)
    w2 = e / (1.0 + e)
    out = jnp.where(lane == 0, i1.astype(F32), 0.0)
    out = jnp.where(lane == 1, i2.astype(F32), out)
    out = jnp.where(lane == 2, w1, out)
    out = jnp.where(lane == 3, w2, out)
    o_ref[...] = out


def _router(x, gain, wr, *, tm=1024):
    t, d = x.shape
    return pl.pallas_call(
        _router_kernel,
        grid=(t // tm,),
        in_specs=[
            pl.BlockSpec((tm, d), lambda i: (i, 0)),
            pl.BlockSpec((1, d), lambda i: (0, 0)),
            pl.BlockSpec((d, LANES), lambda i: (0, 0)),
        ],
        out_specs=pl.BlockSpec((tm, LANES), lambda i: (i, 0)),
        out_shape=jax.ShapeDtypeStruct((t, LANES), F32),
        compiler_params=_params("parallel"),
        name="router",
    )(x, gain, wr)


def _moe_kernel(tok_ref, texp_ref, tval_ref, x_hbm, g_ref, wg_ref, wu_ref, wd_ref,
                o_ref, xbuf, h_scr, acc_scr, sem, *, tm):
    m = pl.program_id(0)
    f = pl.program_id(1)
    valid = tval_ref[m] > 0

    def row_copy(r):
        tok = tok_ref[m * tm + r]
        return pltpu.make_async_copy(x_hbm.at[pl.ds(tok, 1), :], xbuf.at[pl.ds(r, 1), :], sem)

    @pl.when(jnp.logical_and(valid, f == 0))
    def _():
        def start(r, c):
            row_copy(r).start()
            return c
        lax.fori_loop(0, tm, start, 0)

        def wait(r, c):
            row_copy(r).wait()
            return c
        lax.fori_loop(0, tm, wait, 0)
        h_scr[...] = _rms(xbuf[...], g_ref[...]).astype(BF16)
        acc_scr[...] = jnp.zeros_like(acc_scr)

    @pl.when(valid)
    def _():
        h = h_scr[...]
        a = _silu(_dot(h, wg_ref[0])) * _dot(h, wu_ref[0])
        acc_scr[...] += _dot(a.astype(BF16), wd_ref[0])

    @pl.when(f == pl.num_programs(1) - 1)
    def _():
        o_ref[...] = jnp.where(valid, acc_scr[...], 0.0)


def _moe_experts(x, gain, wg, wu, wd, tok, tile_expert, tile_valid, *, tm, tf=512):
    t, d = x.shape
    ff = wg.shape[2]
    n_tiles = tile_expert.shape[0]
    nf = ff // tf
    kern = functools.partial(_moe_kernel, tm=tm)

    def fsel(m, f, tval):
        return jnp.where(tval[m] > 0, f, nf - 1)

    grid_spec = pltpu.PrefetchScalarGridSpec(
        num_scalar_prefetch=3,
        grid=(n_tiles, nf),
        in_specs=[
            pl.BlockSpec(memory_space=pl.ANY),
            pl.BlockSpec((1, d), lambda m, f, tok, texp, tval: (0, 0)),
            pl.BlockSpec((1, d, tf), lambda m, f, tok, texp, tval: (texp[m], 0, fsel(m, f, tval))),
            pl.BlockSpec((1, d, tf), lambda m, f, tok, texp, tval: (texp[m], 0, fsel(m, f, tval))),
            pl.BlockSpec((1, tf, d), lambda m, f, tok, texp, tval: (texp[m], fsel(m, f, tval), 0)),
        ],
        out_specs=pl.BlockSpec((tm, d), lambda m, f, tok, texp, tval: (m, 0)),
        scratch_shapes=[
            pltpu.VMEM((tm, d), F32),
            pltpu.VMEM((tm, d), BF16),
            pltpu.VMEM((tm, d), F32),
            pltpu.SemaphoreType.DMA(()),
        ],
    )
    return pl.pallas_call(
        kern,
        grid_spec=grid_spec,
        out_shape=jax.ShapeDtypeStruct((n_tiles * tm, d), F32),
        compiler_params=_params("arbitrary", "arbitrary"),
        name="moe_experts",
    )(tok, tile_expert, tile_valid, x, gain, wg, wu, wd)


def _combine_kernel(pos_ref, x_ref, r_ref, y_hbm, o_ref, ybuf, sem, *, tm, n_tok):
    i = pl.program_id(0)

    def row_copy(k, r):
        src = pos_ref[k * n_tok + i * tm + r]
        return pltpu.make_async_copy(y_hbm.at[pl.ds(src, 1), :], ybuf.at[k, pl.ds(r, 1), :], sem)

    def start(r, c):
        for k in range(TOP_K):
            row_copy(k, r).start()
        return c
    lax.fori_loop(0, tm, start, 0)

    def wait(r, c):
        for k in range(TOP_K):
            row_copy(k, r).wait()
        return c
    lax.fori_loop(0, tm, wait, 0)

    route = r_ref[...]
    acc = x_ref[...]
    for k in range(TOP_K):
        acc = acc + route[:, TOP_K + k:TOP_K + k + 1] * ybuf[k]
    o_ref[...] = acc


def _moe_combine(x, route, y_sorted, pos, *, tm=256):
    t, d = x.shape
    kern = functools.partial(_combine_kernel, tm=tm, n_tok=t)
    grid_spec = pltpu.PrefetchScalarGridSpec(
        num_scalar_prefetch=1,
        grid=(t // tm,),
        in_specs=[
            pl.BlockSpec((tm, d), lambda i, pos: (i, 0)),
            pl.BlockSpec((tm, LANES), lambda i, pos: (i, 0)),
            pl.BlockSpec(memory_space=pl.ANY),
        ],
        out_specs=pl.BlockSpec((tm, d), lambda i, pos: (i, 0)),
        scratch_shapes=[pltpu.VMEM((TOP_K, tm, d), F32), pltpu.SemaphoreType.DMA(())],
    )
    return pl.pallas_call(
        kern,
        grid_spec=grid_spec,
        out_shape=jax.ShapeDtypeStruct((t, d), F32),
        compiler_params=_params("arbitrary"),
        name="moe_combine",
    )(pos, x, route, y_sorted)


def _moe(x, gain, w_router, wg, wu, wd, *, tm=512):
    t, d = x.shape
    wr = jnp.zeros((d, LANES), F32).at[:, :N_EXPERTS].set(w_router)
    route = _router(x, gain, wr)
    eflat = route[:, :TOP_K].astype(I32).T.reshape(-1)
    onehot = (eflat[:, None] == jnp.arange(N_EXPERTS, dtype=I32)[None, :]).astype(I32)
    csum = jnp.cumsum(onehot, axis=0)
    rank = jnp.sum((csum - 1) * onehot, axis=1)
    counts = csum[-1]
    tiles_per = (counts + tm - 1) // tm
    tile_end = jnp.cumsum(tiles_per)
    start = (tile_end - tiles_per) * tm
    pos = (jnp.sum(start[None, :] * onehot, axis=1) + rank).astype(I32)
    n_tiles = (TOP_K * t) // tm + N_EXPERTS
    tok_ids = jnp.tile(jnp.arange(t, dtype=I32), TOP_K)
    tok = jnp.zeros((n_tiles * tm,), I32).at[pos].set(tok_ids)
    tile_id = jnp.arange(n_tiles, dtype=I32)
    tile_expert = jnp.sum((tile_id[:, None] >= tile_end[None, :]).astype(I32), axis=1)
    tile_valid = (tile_id < tile_end[-1]).astype(I32)
    last_expert = jnp.max(jnp.where(counts > 0, jnp.arange(N_EXPERTS, dtype=I32), 0))
    tile_expert = jnp.where(tile_valid > 0, tile_expert, last_expert).astype(I32)
    y_sorted = _moe_experts(x, gain, wg, wu, wd, tok, tile_expert, tile_valid, tm=tm)
    return _moe_combine(x, route, y_sorted, pos)


def _final_norm_kernel(x_ref, g_ref, o_ref):
    o_ref[...] = _rms(x_ref[...], g_ref[...])


def _final_norm(x, gain, *, tm=1024):
    t, d = x.shape
    return pl.pallas_call(
        _final_norm_kernel,
        grid=(t // tm,),
        in_specs=[pl.BlockSpec((tm, d), lambda i: (i, 0)), pl.BlockSpec((1, d), lambda i: (0, 0))],
        out_specs=pl.BlockSpec((tm, d), lambda i: (i, 0)),
        out_shape=jax.ShapeDtypeStruct((t, d), F32),
        compiler_params=_params("parallel"),
        name="final_norm",
    )(x, gain)


def _rope_tables(seq):
    half = HEAD_DIM // 2
    inv_freq = ROPE_THETA ** (-jnp.arange(half, dtype=F32) / half)
    ang = jnp.arange(seq, dtype=F32)[:, None] * inv_freq[None, :]
    cos, sin = jnp.cos(ang), jnp.sin(ang)
    reps = LANES // HEAD_DIM
    cos_t = jnp.tile(jnp.concatenate([cos, cos], axis=1), (1, reps))
    sin_t = jnp.tile(jnp.concatenate([-sin, sin], axis=1), (1, reps))
    return cos_t, sin_t


def _in_proj_weight(w_in):
    o = 0
    offs = {}
    for name, width in (("qa", A_W), ("ka", A_W), ("va", A_W), ("qb", B_QW), ("kb", B_KVW),
                        ("vb", B_KVW), ("qc", C_W), ("kc", C_W), ("vc", C_W), ("g", 3 * D_MODEL)):
        offs[name] = (o, o + width)
        o += width
    scale = HEAD_DIM ** -0.5
    parts = []
    for name in ("g", "qa", "ka", "qb", "kb", "vb", "va", "qc", "kc", "vc"):
        lo, hi = offs[name]
        blk = w_in[:, lo:hi]
        if name in ("qa", "qb", "qc"):
            blk = blk * scale
        parts.append(blk)
    return jnp.concatenate(parts, axis=1).astype(BF16)


def _rope_group_map():
    groups = {}
    for t in range(IN_W // PROJ_TN):
        lo, hi = t * PROJ_TN, (t + 1) * PROJ_TN
        n = (min(hi, ROPE_HI) - max(lo, ROPE_LO)) // LANES
        if n > 0:
            assert max(lo, ROPE_LO) == lo
            groups[t] = n
    return groups


def kernel(x, mem, norm_mix, w_in, w_proj_a, w_proj_b, w_proj_c, w_mix_out, sinks, norm_cross,
           norm_mem, w_xq, w_xkv, w_xo, norm_ffn, ffn_gate, ffn_up, ffn_down, moe_router,
           moe_gate, moe_up, moe_down, final_norm):
    batch, seq, d = x.shape
    depth = norm_mix.shape[0]
    assert d == D_MODEL and seq % 1024 == 0 and mem.shape[1] == MEM_LEN
    t = batch * seq
    xf = x.reshape(t, d)
    memf = mem.reshape(batch * MEM_LEN, d)
    cos_t, sin_t = _rope_tables(seq)
    rope_groups = _rope_group_map()
    ones_tab = jnp.ones((MEM_LEN, LANES), F32)

    for l in range(depth):
        gain = lambda g: g[l].reshape(1, d)
        p = _norm_proj(xf, gain(norm_mix), _in_proj_weight(w_in[l]), cos_t, sin_t, seq,
                       tm=1024, tn=PROJ_TN, rope_groups=rope_groups)
        oa = _moba(p, batch, seq)
        ob = _swa(p, sinks[l], batch, seq)
        oc = _stick(p, batch, seq)
        xf = _merge(xf, oa, ob, oc, p, w_proj_a[l].astype(BF16), w_proj_b[l].astype(BF16),
                    w_proj_c[l].astype(BF16), w_mix_out[l].astype(BF16))
        kv = _norm_proj(memf, gain(norm_mem), w_xkv[l].astype(BF16), ones_tab, ones_tab, MEM_LEN,
                        tm=MEM_LEN, tn=2 * X_W, rope_groups={})
        xf = _cross(xf, gain(norm_cross), w_xq[l].astype(BF16), kv, w_xo[l].astype(BF16), seq)
        if l % 2 == 0:
            i = l // 2
            xf = _ffn(xf, gain(norm_ffn), ffn_gate[i].astype(BF16), ffn_up[i].astype(BF16),
                      ffn_down[i].astype(BF16))
        else:
            i = l // 2
            xf = _moe(xf, gain(norm_ffn), moe_router[i], moe_gate[i].astype(BF16),
                      moe_up[i].astype(BF16), moe_down[i].astype(BF16))
    out = _final_norm(xf, final_norm.reshape(1, d))
    return out.reshape(batch, seq, d)
```

```python
import functools

import jax
import jax.numpy as jnp
from jax import lax
from jax.experimental import pallas as pl
from jax.experimental.pallas import tpu as pltpu

F32 = jnp.float32
BF16 = jnp.bfloat16
I32 = jnp.int32

D_MODEL = 1024
HEAD_DIM = 64
A_HEADS = 4
MOBA_BLOCK = 256
MOBA_TOPK = 3
B_HEADS = 8
B_KV_HEADS = 2
WINDOW = 128
C_HEADS = 4
SB_BLOCK = 256
MEM_LEN = 256
X_HEADS = 4
X_HEAD_DIM = 128
N_EXPERTS = 8
TOP_K = 2
ROPE_THETA = 10000.0
EPS = 1e-6

A_W = A_HEADS * HEAD_DIM
B_QW = B_HEADS * HEAD_DIM
B_KVW = B_KV_HEADS * HEAD_DIM
C_W = C_HEADS * HEAD_DIM
X_W = X_HEADS * X_HEAD_DIM
QKV_W = 3 * A_W + B_QW + 2 * B_KVW + 3 * C_W
IN_W = QKV_W + 3 * D_MODEL

LANES = 128
NEG = -1e30

COL_GA, COL_GB, COL_GC = 0, D_MODEL, 2 * D_MODEL
COL_QA = 3 * D_MODEL
COL_KA = COL_QA + A_W
COL_QB = COL_KA + A_W
COL_KB = COL_QB + B_QW
COL_VB = COL_KB + B_KVW
COL_VA = COL_VB + B_KVW
COL_QC = COL_VA + A_W
COL_KC = COL_QC + C_W
COL_VC = COL_KC + C_W
ROPE_LO, ROPE_HI = COL_QA, COL_VB

PROJ_TN = 768
VMEM_LIMIT = 48 * 1024 * 1024


def _params(*sem):
    return pltpu.CompilerParams(dimension_semantics=sem, vmem_limit_bytes=VMEM_LIMIT)


def _rms(x, g):
    ms = jnp.mean(x * x, axis=-1, keepdims=True)
    return x * lax.rsqrt(ms + EPS) * g


def _dot(a, b):
    return jnp.dot(a, b, preferred_element_type=F32)


def _dot_t(a, b):
    return lax.dot_general(a, b, (((1,), (1,)), ((), ())), preferred_element_type=F32)


def _split_bf16(v):
    hi = v.astype(BF16)
    lo = (v - hi.astype(F32)).astype(BF16)
    return hi, lo


def _norm_proj_kernel(x_ref, g_ref, w_ref, cos_ref, sin_ref, o_ref, h_scr, *, rope_groups):
    j = pl.program_id(1)

    @pl.when(j == 0)
    def _():
        h_scr[...] = _rms(x_ref[...], g_ref[...]).astype(BF16)

    acc = _dot(h_scr[...], w_ref[...])
    n_groups = acc.shape[1] // LANES

    def rope(y):
        lane = lax.broadcasted_iota(I32, y.shape, 1)
        first_half = (lane % HEAD_DIM) < (HEAD_DIM // 2)
        sw = jnp.where(first_half, pltpu.roll(y, LANES - HEAD_DIM // 2, 1),
                       pltpu.roll(y, HEAD_DIM // 2, 1))
        return y * cos_ref[...] + sw * sin_ref[...]

    roped = sorted(rope_groups)

    for t in roped:
        @pl.when(j == t)
        def _(t=t):
            for gi in range(n_groups):
                y = acc[:, gi * LANES:(gi + 1) * LANES]
                if gi < rope_groups[t]:
                    y = rope(y)
                o_ref[:, gi * LANES:(gi + 1) * LANES] = y.astype(o_ref.dtype)

    is_plain = j >= 0
    for t in roped:
        is_plain = jnp.logical_and(is_plain, j != t)

    @pl.when(is_plain)
    def _():
        o_ref[...] = acc.astype(o_ref.dtype)


def _norm_proj(x, gain, w, cos, sin, seq, *, tm, tn, rope_groups):
    t, d = x.shape
    n = w.shape[1]
    pos_blocks = seq // tm
    kern = functools.partial(_norm_proj_kernel, rope_groups=rope_groups)
    return pl.pallas_call(
        kern,
        grid=(t // tm, n // tn),
        in_specs=[
            pl.BlockSpec((tm, d), lambda i, j: (i, 0)),
            pl.BlockSpec((1, d), lambda i, j: (0, 0)),
            pl.BlockSpec((d, tn), lambda i, j: (0, j)),
            pl.BlockSpec((tm, LANES), lambda i, j: (i % pos_blocks, 0)),
            pl.BlockSpec((tm, LANES), lambda i, j: (i % pos_blocks, 0)),
        ],
        out_specs=pl.BlockSpec((tm, tn), lambda i, j: (i, j)),
        out_shape=jax.ShapeDtypeStruct((t, n), BF16),
        scratch_shapes=[pltpu.VMEM((tm, d), BF16)],
        compiler_params=_params("parallel", "arbitrary"),
        name="norm_proj",
    )(x, gain, w, cos, sin)


MOBA_VROWS = HEAD_DIM + 16


def _moba_kernel(qt_ref, k_ref, vt_ref, o_ref, kmean_scr, kaug_scr, vaug_scr, qaug_scr, *, nblk):
    i = pl.program_id(1)
    blk = MOBA_BLOCK
    hd = HEAD_DIM
    aug = 2 * hd
    vr = MOBA_VROWS
    heads = range(A_HEADS)
    hsl = [slice(hh * hd, (hh + 1) * hd) for hh in heads]

    @pl.when(i == 0)
    def _():
        ones = jnp.ones((vr - hd, blk), BF16)
        blk_lane = lax.broadcasted_iota(I32, (blk, hd), 1)
        for n in range(nblk):
            rows = slice(n * blk, (n + 1) * blk)
            kmean_scr[n:n + 1, :] = jnp.mean(k_ref[rows, :].astype(F32), axis=0, keepdims=True)
            onehot = jnp.where(blk_lane == n, 1.0, 0.0).astype(BF16)
            for hh in heads:
                kaug_scr[rows, hh * aug:hh * aug + hd] = k_ref[rows, hsl[hh]]
                kaug_scr[rows, hh * aug + hd:(hh + 1) * aug] = onehot
                vaug_scr[n, hh * vr:hh * vr + hd, :] = vt_ref[n, hsl[hh], :]
                vaug_scr[n, hh * vr + hd:(hh + 1) * vr, :] = ones

    key = lax.broadcasted_iota(I32, (blk, blk), 0)
    qry = lax.broadcasted_iota(I32, (blk, blk), 1)
    causal = key <= qry
    blk_id = lax.broadcasted_iota(I32, (nblk, blk), 0)
    own0 = pl.multiple_of(i * blk, blk)

    init = []
    for hh in heads:
        qt = qt_ref[hsl[hh], :]
        km_hi, km_lo = _split_bf16(kmean_scr[:, hsl[hh]])
        g = _dot(km_hi, qt) + _dot(km_lo, qt)

        cnt = jnp.zeros((nblk, blk), I32)
        for m in range(nblk):
            gm = g[m:m + 1, :]
            beats = (gm > g) | ((gm == g) & (m < blk_id))
            cnt = cnt + jnp.where(beats, (m < i).astype(I32), 0)
        sel = (cnt < MOBA_TOPK) & (blk_id < i)
        qaug_scr[hh * aug:hh * aug + hd, :] = qt
        qaug_scr[hh * aug + hd:hh * aug + hd + nblk, :] = jnp.where(sel, 0.0, NEG).astype(BF16)
        qaug_scr[hh * aug + hd + nblk:(hh + 1) * aug, :] = jnp.zeros((hd - nblk, blk), BF16)

        s = _dot(k_ref[pl.ds(own0, blk), hsl[hh]], qt)
        s = jnp.where(causal, s, NEG)
        m0 = jnp.max(s, axis=0, keepdims=True)
        p = jnp.exp2(s - m0)
        init.append((m0, _dot(vaug_scr[i, hh * vr:(hh + 1) * vr, :], p.astype(BF16))))

    def body(j, carry):
        r0 = pl.multiple_of(j * blk, blk)
        ss = [_dot(kaug_scr[pl.ds(r0, blk), hh * aug:(hh + 1) * aug],
                   qaug_scr[hh * aug:(hh + 1) * aug, :]) for hh in heads]
        ms = [jnp.maximum(carry[hh][0], jnp.max(ss[hh], axis=0, keepdims=True)) for hh in heads]
        ps = [jnp.exp2(ss[hh] - ms[hh]).astype(BF16) for hh in heads]
        pvs = [_dot(vaug_scr[j, hh * vr:(hh + 1) * vr, :], ps[hh]) for hh in heads]
        return tuple((ms[hh], jnp.exp2(carry[hh][0] - ms[hh]) * carry[hh][1] + pvs[hh])
                     for hh in heads)

    fin = lax.fori_loop(0, i, body, tuple(init))
    out_t = jnp.concatenate([acc[:hd, :] / acc[hd:hd + 1, :] for _, acc in fin], axis=0)
    o_ref[...] = out_t.T.astype(o_ref.dtype)


def _moba(p, batch, seq):
    nblk = seq // MOBA_BLOCK
    assert nblk <= HEAD_DIM
    qt = _blocks_t(p, COL_QA, A_W, batch, seq, MOBA_BLOCK)
    vt = _blocks_t(p, COL_VA, A_W, batch, seq, MOBA_BLOCK)
    kern = functools.partial(_moba_kernel, nblk=nblk)
    return pl.pallas_call(
        kern,
        grid=(batch, nblk),
        in_specs=[
            pl.BlockSpec((None, None, A_W, MOBA_BLOCK), lambda b, i: (b, i, 0, 0)),
            pl.BlockSpec((seq, A_W), lambda b, i: (b, COL_KA // A_W)),
            pl.BlockSpec((None, nblk, A_W, MOBA_BLOCK), lambda b, i: (b, 0, 0, 0)),
        ],
        out_specs=pl.BlockSpec((MOBA_BLOCK, A_W), lambda b, i: (b * nblk + i, 0)),
        out_shape=jax.ShapeDtypeStruct((batch * seq, A_W), BF16),
        scratch_shapes=[
            pltpu.VMEM((nblk, A_W), F32),
            pltpu.VMEM((seq, 2 * A_W), BF16),
            pltpu.VMEM((nblk, A_HEADS * MOBA_VROWS, MOBA_BLOCK), BF16),
            pltpu.VMEM((2 * A_W, MOBA_BLOCK), BF16),
        ],
        compiler_params=_params("parallel", "arbitrary"),
        name="moba",
    )(qt, p, vt)


def _swa_kernel(sinks_ref, q_ref, kv_ref, pkv_ref, o_ref, kv_scr, *, tq):
    i = pl.program_id(1)
    w = WINDOW
    kv_scr[0:w, :] = pkv_ref[...]
    kv_scr[w:, :] = kv_ref[...]

    row = lax.broadcasted_iota(I32, (w, 2 * w), 0)
    col = lax.broadcasted_iota(I32, (w, 2 * w), 1)
    rel = row + w - col
    band = (rel >= 0) & (rel < w)
    group = B_HEADS // B_KV_HEADS

    def body(n, carry):
        base = pl.multiple_of(n * w, w)
        kvt = kv_scr[pl.ds(base, 2 * w), :]
        qt = q_ref[pl.ds(base, w), :]
        col_min = jnp.where(jnp.logical_and(i == 0, n == 0), w, 0)
        mask = band & (col >= col_min)
        outs = []
        for kh in range(B_KV_HEADS):
            kk = kvt[:, kh * HEAD_DIM:(kh + 1) * HEAD_DIM]
            vv = kvt[:, B_KVW + kh * HEAD_DIM:B_KVW + (kh + 1) * HEAD_DIM]
            for gi in range(group):
                hd = kh * group + gi
                qh = qt[:, hd * HEAD_DIM:(hd + 1) * HEAD_DIM]
                s = jnp.where(mask, _dot_t(qh, kk), NEG)
                sink = sinks_ref[hd]
                m = jnp.maximum(jnp.max(s, axis=-1, keepdims=True), sink)
                p = jnp.exp(s - m)
                l = jnp.sum(p, axis=-1, keepdims=True) + jnp.exp(sink - m)
                outs.append(_dot(p.astype(BF16), vv) / l)
        o_ref[pl.ds(base, w), :] = jnp.concatenate(outs, axis=1).astype(o_ref.dtype)
        return carry

    lax.fori_loop(0, tq // w, body, 0)


def _swa(p, sinks, batch, seq, *, tq=512):
    nq = seq // tq
    sub = tq // WINDOW
    kern = functools.partial(_swa_kernel, tq=tq)
    return pl.pallas_call(
        kern,
        grid=(batch, nq),
        in_specs=[
            pl.BlockSpec(memory_space=pltpu.SMEM),
            pl.BlockSpec((tq, B_QW), lambda b, i: (b * nq + i, COL_QB // B_QW)),
            pl.BlockSpec((tq, 2 * B_KVW), lambda b, i: (b * nq + i, COL_KB // (2 * B_KVW))),
            pl.BlockSpec((WINDOW, 2 * B_KVW),
                         lambda b, i: (jnp.maximum((b * nq + i) * sub - 1, 0),
                                       COL_KB // (2 * B_KVW))),
        ],
        out_specs=pl.BlockSpec((tq, B_QW), lambda b, i: (b * nq + i, 0)),
        out_shape=jax.ShapeDtypeStruct((batch * seq, B_QW), BF16),
        scratch_shapes=[pltpu.VMEM((tq + WINDOW, 2 * B_KVW), BF16)],
        compiler_params=_params("parallel", "parallel"),
        name="swa",
    )(sinks, p, p, p)


def _stick_kernel(qt_ref, k_ref, vt_ref, o_ref):
    i = pl.program_id(1)
    blk = SB_BLOCK
    hd = HEAD_DIM
    heads = range(C_HEADS)
    hsl = [slice(hh * hd, (hh + 1) * hd) for hh in heads]
    key = lax.broadcasted_iota(I32, (blk, blk), 0)
    qry = lax.broadcasted_iota(I32, (blk, blk), 1)
    causal = key < qry
    tri = jnp.where(qry > key, 1.0, 0.0).astype(BF16)
    own0 = pl.multiple_of(i * blk, blk)

    def logs(z):
        neg_abs = lax.bitcast_convert_type(
            lax.bitcast_convert_type(z, jnp.uint32) | jnp.uint32(0x80000000), F32)
        t = jnp.log2(1.0 + jnp.exp2(neg_abs))
        log_beta = jnp.minimum(z, 0.0) - t
        return log_beta, log_beta - z

    def suffix(lk):
        after = _dot(tri, lk.astype(BF16))
        total = after[0:1, :] + lk[0:1, :]
        return after, total

    qts = [qt_ref[hsl[hh], :] for hh in heads]
    init = []
    for hh in heads:
        z = _dot(k_ref[pl.ds(own0, blk), hsl[hh]], qts[hh])
        log_beta, log_keep = logs(z)
        after, carry0 = suffix(jnp.where(causal, log_keep, 0.0))
        wgt = jnp.where(causal, jnp.exp2(log_beta + after), 0.0)
        init.append((carry0, _dot(vt_ref[i, hsl[hh], :], wgt.astype(BF16))))

    def body(jj, state):
        j = i - 1 - jj
        r0 = pl.multiple_of(j * blk, blk)
        zs = [_dot(k_ref[pl.ds(r0, blk), hsl[hh]], qts[hh]) for hh in heads]
        lgs = [logs(zs[hh]) for hh in heads]
        sfx = [suffix(lgs[hh][1]) for hh in heads]
        ws = [jnp.exp2(lgs[hh][0] + sfx[hh][0]).astype(BF16) for hh in heads]
        pvs = [_dot(vt_ref[j, hsl[hh], :], ws[hh]) for hh in heads]
        return tuple((state[hh][0] + sfx[hh][1],
                      state[hh][1] + pvs[hh] * jnp.exp2(state[hh][0])) for hh in heads)

    fin = lax.fori_loop(0, i, body, tuple(init))
    out_t = jnp.concatenate([acc for _, acc in fin], axis=0)
    o_ref[...] = out_t.T.astype(o_ref.dtype)


def _blocks_t(p, col, width, batch, seq, blk):
    t = p[:, col:col + width].reshape(batch, seq // blk, blk, width)
    return jnp.swapaxes(t, 2, 3)


def _stick(p, batch, seq):
    nblk = seq // SB_BLOCK
    qt = _blocks_t(p, COL_QC, C_W, batch, seq, SB_BLOCK)
    vt = _blocks_t(p, COL_VC, C_W, batch, seq, SB_BLOCK)
    return pl.pallas_call(
        _stick_kernel,
        grid=(batch, nblk),
        in_specs=[
            pl.BlockSpec((None, None, C_W, SB_BLOCK), lambda b, i: (b, i, 0, 0)),
            pl.BlockSpec((seq, C_W), lambda b, i: (b, COL_KC // C_W)),
            pl.BlockSpec((None, nblk, C_W, SB_BLOCK), lambda b, i: (b, 0, 0, 0)),
        ],
        out_specs=pl.BlockSpec((SB_BLOCK, C_W), lambda b, i: (b * nblk + i, 0)),
        out_shape=jax.ShapeDtypeStruct((batch * seq, C_W), BF16),
        compiler_params=_params("parallel", "parallel"),
        name="stick",
    )(qt, p, vt)


def _sigmoid(x):
    return 1.0 / (1.0 + jnp.exp(-x))


def _merge_kernel(x_ref, oa_ref, ob_ref, oc_ref, ga_ref, gb_ref, gc_ref,
                  wa_ref, wb_ref, wc_ref, wo_ref, o_ref):
    mixed = _sigmoid(ga_ref[...].astype(F32)) * _dot(oa_ref[...], wa_ref[...])
    mixed = mixed + _sigmoid(gb_ref[...].astype(F32)) * _dot(ob_ref[...], wb_ref[...])
    mixed = mixed + _sigmoid(gc_ref[...].astype(F32)) * _dot(oc_ref[...], wc_ref[...])
    o_ref[...] = x_ref[...] + _dot(mixed.astype(BF16), wo_ref[...])


def _merge(x, oa, ob, oc, p, wa, wb, wc, wo, *, tm=512):
    t, d = x.shape
    full = lambda a: pl.BlockSpec(a.shape, lambda i: (0, 0))
    return pl.pallas_call(
        _merge_kernel,
        grid=(t // tm,),
        in_specs=[
            pl.BlockSpec((tm, d), lambda i: (i, 0)),
            pl.BlockSpec((tm, A_W), lambda i: (i, 0)),
            pl.BlockSpec((tm, B_QW), lambda i: (i, 0)),
            pl.BlockSpec((tm, C_W), lambda i: (i, 0)),
            pl.BlockSpec((tm, d), lambda i: (i, COL_GA // D_MODEL)),
            pl.BlockSpec((tm, d), lambda i: (i, COL_GB // D_MODEL)),
            pl.BlockSpec((tm, d), lambda i: (i, COL_GC // D_MODEL)),
            full(wa), full(wb), full(wc), full(wo),
        ],
        out_specs=pl.BlockSpec((tm, d), lambda i: (i, 0)),
        out_shape=jax.ShapeDtypeStruct((t, d), F32),
        compiler_params=_params("parallel"),
        name="merge",
    )(x, oa, ob, oc, p, p, p, wa, wb, wc, wo)


def _cross_kernel(x_ref, g_ref, wq_ref, kv_ref, wo_ref, o_ref):
    x = x_ref[...]
    h = _rms(x, g_ref[...]).astype(BF16)
    q = (_dot(h, wq_ref[...]) * (X_HEAD_DIM ** -0.5)).astype(BF16)
    outs = []
    for hd in range(X_HEADS):
        hs = slice(hd * X_HEAD_DIM, (hd + 1) * X_HEAD_DIM)
        s = _dot_t(q[:, hs], kv_ref[:, hs])
        m = jnp.max(s, axis=-1, keepdims=True)
        p = jnp.exp(s - m)
        l = jnp.sum(p, axis=-1, keepdims=True)
        vs = slice(X_W + hd * X_HEAD_DIM, X_W + (hd + 1) * X_HEAD_DIM)
        outs.append(_dot(p.astype(BF16), kv_ref[:, vs]) / l)
    att = jnp.concatenate(outs, axis=1).astype(BF16)
    o_ref[...] = x + _dot(att, wo_ref[...])


def _cross(x, gain, wq, kv, wo, seq, *, tm=512):
    t, d = x.shape
    per_batch = seq // tm
    full = lambda a: pl.BlockSpec(a.shape, lambda i: (0, 0))
    return pl.pallas_call(
        _cross_kernel,
        grid=(t // tm,),
        in_specs=[
            pl.BlockSpec((tm, d), lambda i: (i, 0)),
            full(gain), full(wq),
            pl.BlockSpec((MEM_LEN, 2 * X_W), lambda i: (i // per_batch, 0)),
            full(wo),
        ],
        out_specs=pl.BlockSpec((tm, d), lambda i: (i, 0)),
        out_shape=jax.ShapeDtypeStruct((t, d), F32),
        compiler_params=_params("parallel"),
        name="cross",
    )(x, gain, wq, kv, wo)


def _silu(x):
    return x * _sigmoid(x)


def _ffn_kernel(x_ref, g_ref, wg_ref, wu_ref, wd_ref, o_ref, h_scr, acc_scr):
    f = pl.program_id(1)

    @pl.when(f == 0)
    def _():
        h_scr[...] = _rms(x_ref[...], g_ref[...]).astype(BF16)
        acc_scr[...] = x_ref[...]

    h = h_scr[...]
    a = _silu(_dot(h, wg_ref[...])) * _dot(h, wu_ref[...])
    acc_scr[...] += _dot(a.astype(BF16), wd_ref[...])

    @pl.when(f == pl.num_programs(1) - 1)
    def _():
        o_ref[...] = acc_scr[...]


def _ffn(x, gain, wg, wu, wd, *, tm=1024, tf=256):
    t, d = x.shape
    ff = wg.shape[1]
    return pl.pallas_call(
        _ffn_kernel,
        grid=(t // tm, ff // tf),
        in_specs=[
            pl.BlockSpec((tm, d), lambda i, f: (i, 0)),
            pl.BlockSpec((1, d), lambda i, f: (0, 0)),
            pl.BlockSpec((d, tf), lambda i, f: (0, f)),
            pl.BlockSpec((d, tf), lambda i, f: (0, f)),
            pl.BlockSpec((tf, d), lambda i, f: (f, 0)),
        ],
        out_specs=pl.BlockSpec((tm, d), lambda i, f: (i, 0)),
        out_shape=jax.ShapeDtypeStruct((t, d), F32),
        scratch_shapes=[pltpu.VMEM((tm, d), BF16), pltpu.VMEM((tm, d), F32)],
        compiler_params=_params("parallel", "arbitrary"),
        name="ffn",
    )(x, gain, wg, wu, wd)


def _router_kernel(x_ref, g_ref, wr_ref, o_ref):
    h = _rms(x_ref[...], g_ref[...])
    h_hi, h_lo = _split_bf16(h)
    w_hi, w_lo = _split_bf16(wr_ref[...])
    logits = _dot(h_hi, w_hi) + (_dot(h_hi, w_lo) + _dot(h_lo, w_hi))
    lane = lax.broadcasted_iota(I32, logits.shape, 1)
    logits = jnp.where(lane < N_EXPERTS, logits, NEG)
    v1 = jnp.max(logits, axis=-1, keepdims=True)
    i1 = jnp.min(jnp.where(logits == v1, lane, LANES), axis=-1, keepdims=True)
    rest = jnp.where(lane == i1, NEG, logits)
    v2 = jnp.max(rest, axis=-1, keepdims=True)
    i2 = jnp.min(jnp.where(rest == v2, lane, LANES), axis=-1, keepdims=True)
    e = jnp.exp(v2 - v1)
    w1 = 1.0 / (1.0 + e)
    w2 = e / (1.0 + e)
    out = jnp.where(lane == 0, i1.astype(F32), 0.0)
    out = jnp.where(lane == 1, i2.astype(F32), out)
    out = jnp.where(lane == 2, w1, out)
    out = jnp.where(lane == 3, w2, out)
    o_ref[...] = out


def _router(x, gain, wr, *, tm=1024):
    t, d = x.shape
    return pl.pallas_call(
        _router_kernel,
        grid=(t // tm,),
        in_specs=[
            pl.BlockSpec((tm, d), lambda i: (i, 0)),
            pl.BlockSpec((1, d), lambda i: (0, 0)),
            pl.BlockSpec((d, LANES), lambda i: (0, 0)),
        ],
        out_specs=pl.BlockSpec((tm, LANES), lambda i: (i, 0)),
        out_shape=jax.ShapeDtypeStruct((t, LANES), F32),
        compiler_params=_params("parallel"),
        name="router",
    )(x, gain, wr)


def _moe_kernel(tok_ref, texp_ref, tval_ref, x_hbm, g_ref, wg_ref, wu_ref, wd_ref,
                o_ref, xbuf, h_scr, acc_scr, sem, *, tm):
    m = pl.program_id(0)
    f = pl.program_id(1)
    valid = tval_ref[m] > 0

    def row_copy(r):
        tok = tok_ref[m * tm + r]
        return pltpu.make_async_copy(x_hbm.at[pl.ds(tok, 1), :], xbuf.at[pl.ds(r, 1), :], sem)

    @pl.when(jnp.logical_and(valid, f == 0))
    def _():
        def start(r, c):
            row_copy(r).start()
            return c
        lax.fori_loop(0, tm, start, 0)

        def wait(r, c):
            row_copy(r).wait()
            return c
        lax.fori_loop(0, tm, wait, 0)
        h_scr[...] = _rms(xbuf[...], g_ref[...]).astype(BF16)
        acc_scr[...] = jnp.zeros_like(acc_scr)

    @pl.when(valid)
    def _():
        h = h_scr[...]
        a = _silu(_dot(h, wg_ref[0])) * _dot(h, wu_ref[0])
        acc_scr[...] += _dot(a.astype(BF16), wd_ref[0])

    @pl.when(f == pl.num_programs(1) - 1)
    def _():
        o_ref[...] = jnp.where(valid, acc_scr[...], 0.0)


def _moe_experts(x, gain, wg, wu, wd, tok, tile_expert, tile_valid, *, tm, tf=512):
    t, d = x.shape
    ff = wg.shape[2]
    n_tiles = tile_expert.shape[0]
    nf = ff // tf
    kern = functools.partial(_moe_kernel, tm=tm)

    def fsel(m, f, tval):
        return jnp.where(tval[m] > 0, f, nf - 1)

    grid_spec = pltpu.PrefetchScalarGridSpec(
        num_scalar_prefetch=3,
        grid=(n_tiles, nf),
        in_specs=[
            pl.BlockSpec(memory_space=pl.ANY),
            pl.BlockSpec((1, d), lambda m, f, tok, texp, tval: (0, 0)),
            pl.BlockSpec((1, d, tf), lambda m, f, tok, texp, tval: (texp[m], 0, fsel(m, f, tval))),
            pl.BlockSpec((1, d, tf), lambda m, f, tok, texp, tval: (texp[m], 0, fsel(m, f, tval))),
            pl.BlockSpec((1, tf, d), lambda m, f, tok, texp, tval: (texp[m], fsel(m, f, tval), 0)),
        ],
        out_specs=pl.BlockSpec((tm, d), lambda m, f, tok, texp, tval: (m, 0)),
        scratch_shapes=[
            pltpu.VMEM((tm, d), F32),
            pltpu.VMEM((tm, d), BF16),
            pltpu.VMEM((tm, d), F32),
            pltpu.SemaphoreType.DMA(()),
        ],
    )
    return pl.pallas_call(
        kern,
        grid_spec=grid_spec,
        out_shape=jax.ShapeDtypeStruct((n_tiles * tm, d), F32),
        compiler_params=_params("arbitrary", "arbitrary"),
        name="moe_experts",
    )(tok, tile_expert, tile_valid, x, gain, wg, wu, wd)


def _combine_kernel(pos_ref, x_ref, r_ref, y_hbm, o_ref, ybuf, sem, *, tm, n_tok):
    i = pl.program_id(0)

    def row_copy(k, r):
        src = pos_ref[k * n_tok + i * tm + r]
        return pltpu.make_async_copy(y_hbm.at[pl.ds(src, 1), :], ybuf.at[k, pl.ds(r, 1), :], sem)

    def start(r, c):
        for k in range(TOP_K):
            row_copy(k, r).start()
        return c
    lax.fori_loop(0, tm, start, 0)

    def wait(r, c):
        for k in range(TOP_K):
            row_copy(k, r).wait()
        return c
    lax.fori_loop(0, tm, wait, 0)

    route = r_ref[...]
    acc = x_ref[...]
    for k in range(TOP_K):
        acc = acc + route[:, TOP_K + k:TOP_K + k + 1] * ybuf[k]
    o_ref[...] = acc


def _moe_combine(x, route, y_sorted, pos, *, tm=256):
    t, d = x.shape
    kern = functools.partial(_combine_kernel, tm=tm, n_tok=t)
    grid_spec = pltpu.PrefetchScalarGridSpec(
        num_scalar_prefetch=1,
        grid=(t // tm,),
        in_specs=[
            pl.BlockSpec((tm, d), lambda i, pos: (i, 0)),
            pl.BlockSpec((tm, LANES), lambda i, pos: (i, 0)),
            pl.BlockSpec(memory_space=pl.ANY),
        ],
        out_specs=pl.BlockSpec((tm, d), lambda i, pos: (i, 0)),
        scratch_shapes=[pltpu.VMEM((TOP_K, tm, d), F32), pltpu.SemaphoreType.DMA(())],
    )
    return pl.pallas_call(
        kern,
        grid_spec=grid_spec,
        out_shape=jax.ShapeDtypeStruct((t, d), F32),
        compiler_params=_params("arbitrary"),
        name="moe_combine",
    )(pos, x, route, y_sorted)


def _moe(x, gain, w_router, wg, wu, wd, *, tm=512):
    t, d = x.shape
    wr = jnp.zeros((d, LANES), F32).at[:, :N_EXPERTS].set(w_router)
    route = _router(x, gain, wr)
    eflat = route[:, :TOP_K].astype(I32).T.reshape(-1)
    onehot = (eflat[:, None] == jnp.arange(N_EXPERTS, dtype=I32)[None, :]).astype(I32)
    csum = jnp.cumsum(onehot, axis=0)
    rank = jnp.sum((csum - 1) * onehot, axis=1)
    counts = csum[-1]
    tiles_per = (counts + tm - 1) // tm
    tile_end = jnp.cumsum(tiles_per)
    start = (tile_end - tiles_per) * tm
    pos = (jnp.sum(start[None, :] * onehot, axis=1) + rank).astype(I32)
    n_tiles = (TOP_K * t) // tm + N_EXPERTS
    tok_ids = jnp.tile(jnp.arange(t, dtype=I32), TOP_K)
    tok = jnp.zeros((n_tiles * tm,), I32).at[pos].set(tok_ids)
    tile_id = jnp.arange(n_tiles, dtype=I32)
    tile_expert = jnp.sum((tile_id[:, None] >= tile_end[None, :]).astype(I32), axis=1)
    tile_valid = (tile_id < tile_end[-1]).astype(I32)
    last_expert = jnp.max(jnp.where(counts > 0, jnp.arange(N_EXPERTS, dtype=I32), 0))
    tile_expert = jnp.where(tile_valid > 0, tile_expert, last_expert).astype(I32)
    y_sorted = _moe_experts(x, gain, wg, wu, wd, tok, tile_expert, tile_valid, tm=tm)
    return _moe_combine(x, route, y_sorted, pos)


def _final_norm_kernel(x_ref, g_ref, o_ref):
    o_ref[...] = _rms(x_ref[...], g_ref[...])


def _final_norm(x, gain, *, tm=1024):
    t, d = x.shape
    return pl.pallas_call(
        _final_norm_kernel,
        grid=(t // tm,),
        in_specs=[pl.BlockSpec((tm, d), lambda i: (i, 0)), pl.BlockSpec((1, d), lambda i: (0, 0))],
        out_specs=pl.BlockSpec((tm, d), lambda i: (i, 0)),
        out_shape=jax.ShapeDtypeStruct((t, d), F32),
        compiler_params=_params("parallel"),
        name="final_norm",
    )(x, gain)


def _rope_tables(seq):
    half = HEAD_DIM // 2
    inv_freq = ROPE_THETA ** (-jnp.arange(half, dtype=F32) / half)
    ang = jnp.arange(seq, dtype=F32)[:, None] * inv_freq[None, :]
    cos, sin = jnp.cos(ang), jnp.sin(ang)
    reps = LANES // HEAD_DIM
    cos_t = jnp.tile(jnp.concatenate([cos, cos], axis=1), (1, reps))
    sin_t = jnp.tile(jnp.concatenate([-sin, sin], axis=1), (1, reps))
    return cos_t, sin_t


def _in_proj_weight(w_in):
    o = 0
    offs = {}
    for name, width in (("qa", A_W), ("ka", A_W), ("va", A_W), ("qb", B_QW), ("kb", B_KVW),
                        ("vb", B_KVW), ("qc", C_W), ("kc", C_W), ("vc", C_W), ("g", 3 * D_MODEL)):
        offs[name] = (o, o + width)
        o += width
    scale = HEAD_DIM ** -0.5
    log2e = 1.4426950408889634
    q_scale = {"qa": scale * log2e, "qb": scale, "qc": scale * log2e}
    parts = []
    for name in ("g", "qa", "ka", "qb", "kb", "vb", "va", "qc", "kc", "vc"):
        lo, hi = offs[name]
        blk = w_in[:, lo:hi]
        if name in q_scale:
            blk = blk * q_scale[name]
        parts.append(blk)
    return jnp.concatenate(parts, axis=1).astype(BF16)


def _rope_group_map():
    groups = {}
    for t in range(IN_W // PROJ_TN):
        lo, hi = t * PROJ_TN, (t + 1) * PROJ_TN
        n = (min(hi, ROPE_HI) - max(lo, ROPE_LO)) // LANES
        if n > 0:
            assert max(lo, ROPE_LO) == lo
            groups[t] = n
    return groups


def kernel(x, mem, norm_mix, w_in, w_proj_a, w_proj_b, w_proj_c, w_mix_out, sinks, norm_cross,
           norm_mem, w_xq, w_xkv, w_xo, norm_ffn, ffn_gate, ffn_up, ffn_down, moe_router,
           moe_gate, moe_up, moe_down, final_norm):
    batch, seq, d = x.shape
    depth = norm_mix.shape[0]
    assert d == D_MODEL and seq % 1024 == 0 and mem.shape[1] == MEM_LEN
    t = batch * seq
    xf = x.reshape(t, d)
    memf = mem.reshape(batch * MEM_LEN, d)
    cos_t, sin_t = _rope_tables(seq)
    rope_groups = _rope_group_map()
    ones_tab = jnp.ones((MEM_LEN, LANES), F32)

    for l in range(depth):
        gain = lambda g: g[l].reshape(1, d)
        p = _norm_proj(xf, gain(norm_mix), _in_proj_weight(w_in[l]), cos_t, sin_t, seq,
                       tm=1024, tn=PROJ_TN, rope_groups=rope_groups)
        oa = _moba(p, batch, seq)
        ob = _swa(p, sinks[l], batch, seq)
        oc = _stick(p, batch, seq)
        xf = _merge(xf, oa, ob, oc, p, w_proj_a[l].astype(BF16), w_proj_b[l].astype(BF16),
                    w_proj_c[l].astype(BF16), w_mix_out[l].astype(BF16))
        kv = _norm_proj(memf, gain(norm_mem), w_xkv[l].astype(BF16), ones_tab, ones_tab, MEM_LEN,
                        tm=MEM_LEN, tn=2 * X_W, rope_groups={})
        xf = _cross(xf, gain(norm_cross), w_xq[l].astype(BF16), kv, w_xo[l].astype(BF16), seq)
        if l % 2 == 0:
            i = l // 2
            xf = _ffn(xf, gain(norm_ffn), ffn_gate[i].astype(BF16), ffn_up[i].astype(BF16),
                      ffn_down[i].astype(BF16))
        else:
            i = l // 2
            xf = _moe(xf, gain(norm_ffn), moe_router[i], moe_gate[i].astype(BF16),
                      moe_up[i].astype(BF16), moe_down[i].astype(BF16))
    out = _final_norm(xf, final_norm.reshape(1, d))
    return out.reshape(batch, seq, d)
```

```python
import functools

import jax
import jax.numpy as jnp
from jax import lax
from jax.experimental import pallas as pl
from jax.experimental.pallas import tpu as pltpu

F32 = jnp.float32
BF16 = jnp.bfloat16
I32 = jnp.int32

D_MODEL = 1024
HEAD_DIM = 64
A_HEADS = 4
MOBA_BLOCK = 256
MOBA_TOPK = 3
B_HEADS = 8
B_KV_HEADS = 2
WINDOW = 128
C_HEADS = 4
SB_BLOCK = 256
MEM_LEN = 256
X_HEADS = 4
X_HEAD_DIM = 128
N_EXPERTS = 8
TOP_K = 2
ROPE_THETA = 10000.0
EPS = 1e-6

A_W = A_HEADS * HEAD_DIM
B_QW = B_HEADS * HEAD_DIM
B_KVW = B_KV_HEADS * HEAD_DIM
C_W = C_HEADS * HEAD_DIM
X_W = X_HEADS * X_HEAD_DIM
QKV_W = 3 * A_W + B_QW + 2 * B_KVW + 3 * C_W
IN_W = QKV_W + 3 * D_MODEL

LANES = 128
NEG = -1e30

COL_GA, COL_GB, COL_GC = 0, D_MODEL, 2 * D_MODEL
COL_QA = 3 * D_MODEL
COL_KA = COL_QA + A_W
COL_QB = COL_KA + A_W
COL_KB = COL_QB + B_QW
COL_VB = COL_KB + B_KVW
COL_VA = COL_VB + B_KVW
COL_QC = COL_VA + A_W
COL_KC = COL_QC + C_W
COL_VC = COL_KC + C_W
ROPE_LO, ROPE_HI = COL_QA, COL_VB

PROJ_TN = 768
VMEM_LIMIT = 48 * 1024 * 1024


def _params(*sem):
    return pltpu.CompilerParams(dimension_semantics=sem, vmem_limit_bytes=VMEM_LIMIT)


def _rms(x, g):
    ms = jnp.mean(x * x, axis=-1, keepdims=True)
    return x * lax.rsqrt(ms + EPS) * g


def _dot(a, b):
    return jnp.dot(a, b, preferred_element_type=F32)


def _dot_t(a, b):
    return lax.dot_general(a, b, (((1,), (1,)), ((), ())), preferred_element_type=F32)


def _split_bf16(v):
    hi = v.astype(BF16)
    lo = (v - hi.astype(F32)).astype(BF16)
    return hi, lo


def _norm_proj_kernel(x_ref, g_ref, w_ref, cos_ref, sin_ref, o_ref, h_scr, *, rope_groups):
    j = pl.program_id(1)

    @pl.when(j == 0)
    def _():
        h_scr[...] = _rms(x_ref[...], g_ref[...]).astype(BF16)

    acc = _dot(h_scr[...], w_ref[...])
    n_groups = acc.shape[1] // LANES

    def rope(y):
        lane = lax.broadcasted_iota(I32, y.shape, 1)
        first_half = (lane % HEAD_DIM) < (HEAD_DIM // 2)
        sw = jnp.where(first_half, pltpu.roll(y, LANES - HEAD_DIM // 2, 1),
                       pltpu.roll(y, HEAD_DIM // 2, 1))
        return y * cos_ref[...] + sw * sin_ref[...]

    roped = sorted(rope_groups)

    for t in roped:
        @pl.when(j == t)
        def _(t=t):
            for gi in range(n_groups):
                y = acc[:, gi * LANES:(gi + 1) * LANES]
                if gi < rope_groups[t]:
                    y = rope(y)
                o_ref[:, gi * LANES:(gi + 1) * LANES] = y.astype(o_ref.dtype)

    is_plain = j >= 0
    for t in roped:
        is_plain = jnp.logical_and(is_plain, j != t)

    @pl.when(is_plain)
    def _():
        o_ref[...] = acc.astype(o_ref.dtype)


def _norm_proj(x, gain, w, cos, sin, seq, *, tm, tn, rope_groups):
    t, d = x.shape
    n = w.shape[1]
    pos_blocks = seq // tm
    kern = functools.partial(_norm_proj_kernel, rope_groups=rope_groups)
    return pl.pallas_call(
        kern,
        grid=(t // tm, n // tn),
        in_specs=[
            pl.BlockSpec((tm, d), lambda i, j: (i, 0)),
            pl.BlockSpec((1, d), lambda i, j: (0, 0)),
            pl.BlockSpec((d, tn), lambda i, j: (0, j)),
            pl.BlockSpec((tm, LANES), lambda i, j: (i % pos_blocks, 0)),
            pl.BlockSpec((tm, LANES), lambda i, j: (i % pos_blocks, 0)),
        ],
        out_specs=pl.BlockSpec((tm, tn), lambda i, j: (i, j)),
        out_shape=jax.ShapeDtypeStruct((t, n), BF16),
        scratch_shapes=[pltpu.VMEM((tm, d), BF16)],
        compiler_params=_params("parallel", "arbitrary"),
        name="norm_proj",
    )(x, gain, w, cos, sin)


MOBA_VROWS = HEAD_DIM + 16


def _moba_kernel(qt_ref, k_ref, vt_ref, o_ref, kmean_scr, kaug_scr, vaug_scr, qaug_scr, *, nblk):
    i = pl.program_id(1)
    blk = MOBA_BLOCK
    hd = HEAD_DIM
    aug = 2 * hd
    vr = MOBA_VROWS
    heads = range(A_HEADS)
    hsl = [slice(hh * hd, (hh + 1) * hd) for hh in heads]

    @pl.when(i == 0)
    def _():
        ones = jnp.ones((vr - hd, blk), BF16)
        blk_lane = lax.broadcasted_iota(I32, (blk, hd), 1)
        for n in range(nblk):
            rows = slice(n * blk, (n + 1) * blk)
            kmean_scr[n:n + 1, :] = jnp.mean(k_ref[rows, :].astype(F32), axis=0, keepdims=True)
            onehot = jnp.where(blk_lane == n, 1.0, 0.0).astype(BF16)
            for hh in heads:
                kaug_scr[rows, hh * aug:hh * aug + hd] = k_ref[rows, hsl[hh]]
                kaug_scr[rows, hh * aug + hd:(hh + 1) * aug] = onehot
                vaug_scr[n, hh * vr:hh * vr + hd, :] = vt_ref[n, hsl[hh], :]
                vaug_scr[n, hh * vr + hd:(hh + 1) * vr, :] = ones

    key = lax.broadcasted_iota(I32, (blk, blk), 0)
    qry = lax.broadcasted_iota(I32, (blk, blk), 1)
    causal = key <= qry
    blk_id = lax.broadcasted_iota(I32, (nblk, blk), 0)
    own0 = pl.multiple_of(i * blk, blk)

    init = []
    for hh in heads:
        qt = qt_ref[hsl[hh], :]
        km_hi, km_lo = _split_bf16(kmean_scr[:, hsl[hh]])
        g = _dot(km_hi, qt) + _dot(km_lo, qt)

        cnt = jnp.zeros((nblk, blk), I32)
        for m in range(nblk):
            gm = g[m:m + 1, :]
            beats = (gm > g) | ((gm == g) & (m < blk_id))
            cnt = cnt + jnp.where(beats, (m < i).astype(I32), 0)
        sel = (cnt < MOBA_TOPK) & (blk_id < i)
        qaug_scr[hh * aug:hh * aug + hd, :] = qt
        qaug_scr[hh * aug + hd:hh * aug + hd + nblk, :] = jnp.where(sel, 0.0, NEG).astype(BF16)
        qaug_scr[hh * aug + hd + nblk:(hh + 1) * aug, :] = jnp.zeros((hd - nblk, blk), BF16)

        s = _dot(k_ref[pl.ds(own0, blk), hsl[hh]], qt)
        s = jnp.where(causal, s, NEG)
        m0 = jnp.max(s, axis=0, keepdims=True)
        p = jnp.exp2(s - m0)
        init.append((m0, _dot(vaug_scr[i, hh * vr:(hh + 1) * vr, :], p.astype(BF16))))

    def body(j, carry):
        r0 = pl.multiple_of(j * blk, blk)
        ss = [_dot(kaug_scr[pl.ds(r0, blk), hh * aug:(hh + 1) * aug],
                   qaug_scr[hh * aug:(hh + 1) * aug, :]) for hh in heads]
        ms = [jnp.maximum(carry[hh][0], jnp.max(ss[hh], axis=0, keepdims=True)) for hh in heads]
        ps = [jnp.exp2(ss[hh] - ms[hh]).astype(BF16) for hh in heads]
        pvs = [_dot(vaug_scr[j, hh * vr:(hh + 1) * vr, :], ps[hh]) for hh in heads]
        return tuple((ms[hh], jnp.exp2(carry[hh][0] - ms[hh]) * carry[hh][1] + pvs[hh])
                     for hh in heads)

    fin = lax.fori_loop(0, i, body, tuple(init))
    out_t = jnp.concatenate([acc[:hd, :] / acc[hd:hd + 1, :] for _, acc in fin], axis=0)
    o_ref[...] = out_t.T.astype(o_ref.dtype)


def _moba(p, batch, seq):
    nblk = seq // MOBA_BLOCK
    assert nblk <= HEAD_DIM
    qt = _blocks_t(p, COL_QA, A_W, batch, seq, MOBA_BLOCK)
    vt = _blocks_t(p, COL_VA, A_W, batch, seq, MOBA_BLOCK)
    kern = functools.partial(_moba_kernel, nblk=nblk)
    return pl.pallas_call(
        kern,
        grid=(batch, nblk),
        in_specs=[
            pl.BlockSpec((None, None, A_W, MOBA_BLOCK), lambda b, i: (b, i, 0, 0)),
            pl.BlockSpec((seq, A_W), lambda b, i: (b, COL_KA // A_W)),
            pl.BlockSpec((None, nblk, A_W, MOBA_BLOCK), lambda b, i: (b, 0, 0, 0)),
        ],
        out_specs=pl.BlockSpec((MOBA_BLOCK, A_W), lambda b, i: (b * nblk + i, 0)),
        out_shape=jax.ShapeDtypeStruct((batch * seq, A_W), BF16),
        scratch_shapes=[
            pltpu.VMEM((nblk, A_W), F32),
            pltpu.VMEM((seq, 2 * A_W), BF16),
            pltpu.VMEM((nblk, A_HEADS * MOBA_VROWS, MOBA_BLOCK), BF16),
            pltpu.VMEM((2 * A_W, MOBA_BLOCK), BF16),
        ],
        compiler_params=_params("parallel", "arbitrary"),
        name="moba",
    )(qt, p, vt)


def _swa_kernel(sinks_ref, q_ref, kv_ref, pkv_ref, o_ref, kv_scr, *, tq):
    i = pl.program_id(1)
    w = WINDOW
    kv_scr[0:w, :] = pkv_ref[...]
    kv_scr[w:, :] = kv_ref[...]

    row = lax.broadcasted_iota(I32, (w, 2 * w), 0)
    col = lax.broadcasted_iota(I32, (w, 2 * w), 1)
    rel = row + w - col
    band = (rel >= 0) & (rel < w)
    group = B_HEADS // B_KV_HEADS

    def body(n, carry):
        base = pl.multiple_of(n * w, w)
        kvt = kv_scr[pl.ds(base, 2 * w), :]
        qt = q_ref[pl.ds(base, w), :]
        col_min = jnp.where(jnp.logical_and(i == 0, n == 0), w, 0)
        mask = band & (col >= col_min)
        outs = []
        for kh in range(B_KV_HEADS):
            kk = kvt[:, kh * HEAD_DIM:(kh + 1) * HEAD_DIM]
            vv = kvt[:, B_KVW + kh * HEAD_DIM:B_KVW + (kh + 1) * HEAD_DIM]
            for gi in range(group):
                hd = kh * group + gi
                qh = qt[:, hd * HEAD_DIM:(hd + 1) * HEAD_DIM]
                s = jnp.where(mask, _dot_t(qh, kk), NEG)
                sink = sinks_ref[hd]
                m = jnp.maximum(jnp.max(s, axis=-1, keepdims=True), sink)
                p = jnp.exp(s - m)
                l = jnp.sum(p, axis=-1, keepdims=True) + jnp.exp(sink - m)
                outs.append(_dot(p.astype(BF16), vv) / l)
        o_ref[pl.ds(base, w), :] = jnp.concatenate(outs, axis=1).astype(o_ref.dtype)
        return carry

    lax.fori_loop(0, tq // w, body, 0)


def _swa(p, sinks, batch, seq, *, tq=512):
    nq = seq // tq
    sub = tq // WINDOW
    kern = functools.partial(_swa_kernel, tq=tq)
    return pl.pallas_call(
        kern,
        grid=(batch, nq),
        in_specs=[
            pl.BlockSpec(memory_space=pltpu.SMEM),
            pl.BlockSpec((tq, B_QW), lambda b, i: (b * nq + i, COL_QB // B_QW)),
            pl.BlockSpec((tq, 2 * B_KVW), lambda b, i: (b * nq + i, COL_KB // (2 * B_KVW))),
            pl.BlockSpec((WINDOW, 2 * B_KVW),
                         lambda b, i: (jnp.maximum((b * nq + i) * sub - 1, 0),
                                       COL_KB // (2 * B_KVW))),
        ],
        out_specs=pl.BlockSpec((tq, B_QW), lambda b, i: (b * nq + i, 0)),
        out_shape=jax.ShapeDtypeStruct((batch * seq, B_QW), BF16),
        scratch_shapes=[pltpu.VMEM((tq + WINDOW, 2 * B_KVW), BF16)],
        compiler_params=_params("parallel", "parallel"),
        name="swa",
    )(sinks, p, p, p)


def _stick_kernel(qt_ref, k_ref, vt_ref, o_ref):
    i = pl.program_id(1)
    blk = SB_BLOCK
    hd = HEAD_DIM
    heads = range(C_HEADS)
    hsl = [slice(hh * hd, (hh + 1) * hd) for hh in heads]
    key = lax.broadcasted_iota(I32, (blk, blk), 0)
    qry = lax.broadcasted_iota(I32, (blk, blk), 1)
    causal = key < qry
    tri = jnp.where(qry > key, 1.0, 0.0).astype(BF16)
    own0 = pl.multiple_of(i * blk, blk)

    def logs(z):
        neg_abs = lax.bitcast_convert_type(
            lax.bitcast_convert_type(z, jnp.uint32) | jnp.uint32(0x80000000), F32)
        t = jnp.log2(1.0 + jnp.exp2(neg_abs))
        log_beta = jnp.minimum(z, 0.0) - t
        return log_beta, log_beta - z

    def suffix(lk):
        after = _dot(tri, lk.astype(BF16))
        total = after[0:1, :] + lk[0:1, :]
        return after, total

    qts = [qt_ref[hsl[hh], :] for hh in heads]
    init = []
    for hh in heads:
        z = _dot(k_ref[pl.ds(own0, blk), hsl[hh]], qts[hh])
        log_beta, log_keep = logs(z)
        after, carry0 = suffix(jnp.where(causal, log_keep, 0.0))
        wgt = jnp.where(causal, jnp.exp2(log_beta + after), 0.0)
        init.append((carry0, _dot(vt_ref[i, hsl[hh], :], wgt.astype(BF16))))

    def body(jj, state):
        j = i - 1 - jj
        r0 = pl.multiple_of(j * blk, blk)
        zs = [_dot(k_ref[pl.ds(r0, blk), hsl[hh]], qts[hh]) for hh in heads]
        lgs = [logs(zs[hh]) for hh in heads]
        sfx = [suffix(lgs[hh][1]) for hh in heads]
        ws = [jnp.exp2(lgs[hh][0] + sfx[hh][0]).astype(BF16) for hh in heads]
        pvs = [_dot(vt_ref[j, hsl[hh], :], ws[hh]) for hh in heads]
        return tuple((state[hh][0] + sfx[hh][1],
                      state[hh][1] + pvs[hh] * jnp.exp2(state[hh][0])) for hh in heads)

    fin = lax.fori_loop(0, i, body, tuple(init))
    out_t = jnp.concatenate([acc for _, acc in fin], axis=0)
    o_ref[...] = out_t.T.astype(o_ref.dtype)


def _blocks_t(p, col, width, batch, seq, blk):
    t = p[:, col:col + width].reshape(batch, seq // blk, blk, width)
    return jnp.swapaxes(t, 2, 3)


def _stick(p, batch, seq):
    nblk = seq // SB_BLOCK
    qt = _blocks_t(p, COL_QC, C_W, batch, seq, SB_BLOCK)
    vt = _blocks_t(p, COL_VC, C_W, batch, seq, SB_BLOCK)
    return pl.pallas_call(
        _stick_kernel,
        grid=(batch, nblk),
        in_specs=[
            pl.BlockSpec((None, None, C_W, SB_BLOCK), lambda b, i: (b, i, 0, 0)),
            pl.BlockSpec((seq, C_W), lambda b, i: (b, COL_KC // C_W)),
            pl.BlockSpec((None, nblk, C_W, SB_BLOCK), lambda b, i: (b, 0, 0, 0)),
        ],
        out_specs=pl.BlockSpec((SB_BLOCK, C_W), lambda b, i: (b * nblk + i, 0)),
        out_shape=jax.ShapeDtypeStruct((batch * seq, C_W), BF16),
        compiler_params=_params("parallel", "parallel"),
        name="stick",
    )(qt, p, vt)


def _sigmoid(x):
    return 1.0 / (1.0 + jnp.exp(-x))


def _merge_kernel(x_ref, oa_ref, ob_ref, oc_ref, ga_ref, gb_ref, gc_ref,
                  wa_ref, wb_ref, wc_ref, wo_ref, o_ref):
    mixed = _sigmoid(ga_ref[...].astype(F32)) * _dot(oa_ref[...], wa_ref[...])
    mixed = mixed + _sigmoid(gb_ref[...].astype(F32)) * _dot(ob_ref[...], wb_ref[...])
    mixed = mixed + _sigmoid(gc_ref[...].astype(F32)) * _dot(oc_ref[...], wc_ref[...])
    o_ref[...] = x_ref[...] + _dot(mixed.astype(BF16), wo_ref[...])


def _merge(x, oa, ob, oc, p, wa, wb, wc, wo, *, tm=512):
    t, d = x.shape
    full = lambda a: pl.BlockSpec(a.shape, lambda i: (0, 0))
    return pl.pallas_call(
        _merge_kernel,
        grid=(t // tm,),
        in_specs=[
            pl.BlockSpec((tm, d), lambda i: (i, 0)),
            pl.BlockSpec((tm, A_W), lambda i: (i, 0)),
            pl.BlockSpec((tm, B_QW), lambda i: (i, 0)),
            pl.BlockSpec((tm, C_W), lambda i: (i, 0)),
            pl.BlockSpec((tm, d), lambda i: (i, COL_GA // D_MODEL)),
            pl.BlockSpec((tm, d), lambda i: (i, COL_GB // D_MODEL)),
            pl.BlockSpec((tm, d), lambda i: (i, COL_GC // D_MODEL)),
            full(wa), full(wb), full(wc), full(wo),
        ],
        out_specs=pl.BlockSpec((tm, d), lambda i: (i, 0)),
        out_shape=jax.ShapeDtypeStruct((t, d), F32),
        compiler_params=_params("parallel"),
        name="merge",
    )(x, oa, ob, oc, p, p, p, wa, wb, wc, wo)


def _cross_kernel(x_ref, g_ref, wq_ref, kv_ref, wo_ref, o_ref):
    x = x_ref[...]
    h = _rms(x, g_ref[...]).astype(BF16)
    q = (_dot(h, wq_ref[...]) * (X_HEAD_DIM ** -0.5)).astype(BF16)
    outs = []
    for hd in range(X_HEADS):
        hs = slice(hd * X_HEAD_DIM, (hd + 1) * X_HEAD_DIM)
        s = _dot_t(q[:, hs], kv_ref[:, hs])
        m = jnp.max(s, axis=-1, keepdims=True)
        p = jnp.exp(s - m)
        l = jnp.sum(p, axis=-1, keepdims=True)
        vs = slice(X_W + hd * X_HEAD_DIM, X_W + (hd + 1) * X_HEAD_DIM)
        outs.append(_dot(p.astype(BF16), kv_ref[:, vs]) / l)
    att = jnp.concatenate(outs, axis=1).astype(BF16)
    o_ref[...] = x + _dot(att, wo_ref[...])


def _cross(x, gain, wq, kv, wo, seq, *, tm=512):
    t, d = x.shape
    per_batch = seq // tm
    full = lambda a: pl.BlockSpec(a.shape, lambda i: (0, 0))
    return pl.pallas_call(
        _cross_kernel,
        grid=(t // tm,),
        in_specs=[
            pl.BlockSpec((tm, d), lambda i: (i, 0)),
            full(gain), full(wq),
            pl.BlockSpec((MEM_LEN, 2 * X_W), lambda i: (i // per_batch, 0)),
            full(wo),
        ],
        out_specs=pl.BlockSpec((tm, d), lambda i: (i, 0)),
        out_shape=jax.ShapeDtypeStruct((t, d), F32),
        compiler_params=_params("parallel"),
        name="cross",
    )(x, gain, wq, kv, wo)


def _silu(x):
    return x * _sigmoid(x)


def _ffn_kernel(x_ref, g_ref, wg_ref, wu_ref, wd_ref, o_ref, h_scr, acc_scr):
    f = pl.program_id(1)

    @pl.when(f == 0)
    def _():
        h_scr[...] = _rms(x_ref[...], g_ref[...]).astype(BF16)
        acc_scr[...] = x_ref[...]

    h = h_scr[...]
    a = _silu(_dot(h, wg_ref[...])) * _dot(h, wu_ref[...])
    acc_scr[...] += _dot(a.astype(BF16), wd_ref[...])

    @pl.when(f == pl.num_programs(1) - 1)
    def _():
        o_ref[...] = acc_scr[...]


def _ffn(x, gain, wg, wu, wd, *, tm=1024, tf=256):
    t, d = x.shape
    ff = wg.shape[1]
    return pl.pallas_call(
        _ffn_kernel,
        grid=(t // tm, ff // tf),
        in_specs=[
            pl.BlockSpec((tm, d), lambda i, f: (i, 0)),
            pl.BlockSpec((1, d), lambda i, f: (0, 0)),
            pl.BlockSpec((d, tf), lambda i, f: (0, f)),
            pl.BlockSpec((d, tf), lambda i, f: (0, f)),
            pl.BlockSpec((tf, d), lambda i, f: (f, 0)),
        ],
        out_specs=pl.BlockSpec((tm, d), lambda i, f: (i, 0)),
        out_shape=jax.ShapeDtypeStruct((t, d), F32),
        scratch_shapes=[pltpu.VMEM((tm, d), BF16), pltpu.VMEM((tm, d), F32)],
        compiler_params=_params("parallel", "arbitrary"),
        name="ffn",
    )(x, gain, wg, wu, wd)


def _router_kernel(x_ref, g_ref, wr_ref, o_ref):
    h = _rms(x_ref[...], g_ref[...])
    h_hi, h_lo = _split_bf16(h)
    w_hi, w_lo = _split_bf16(wr_ref[...])
    logits = _dot(h_hi, w_hi) + (_dot(h_hi, w_lo) + _dot(h_lo, w_hi))
    lane = lax.broadcasted_iota(I32, logits.shape, 1)
    logits = jnp.where(lane < N_EXPERTS, logits, NEG)
    v1 = jnp.max(logits, axis=-1, keepdims=True)
    i1 = jnp.min(jnp.where(logits == v1, lane, LANES), axis=-1, keepdims=True)
    rest = jnp.where(lane == i1, NEG, logits)
    v2 = jnp.max(rest, axis=-1, keepdims=True)
    i2 = jnp.min(jnp.where(rest == v2, lane, LANES), axis=-1, keepdims=True)
    e = jnp.exp(v2 - v1)
    w1 = 1.0 / (1.0 + e)
    w2 = e / (1.0 + e)
    out = jnp.where(lane == 0, i1.astype(F32), 0.0)
    out = jnp.where(lane == 1, i2.astype(F32), out)
    out = jnp.where(lane == 2, w1, out)
    out = jnp.where(lane == 3, w2, out)
    o_ref[...] = out


def _router(x, gain, wr, *, tm=1024):
    t, d = x.shape
    return pl.pallas_call(
        _router_kernel,
        grid=(t // tm,),
        in_specs=[
            pl.BlockSpec((tm, d), lambda i: (i, 0)),
            pl.BlockSpec((1, d), lambda i: (0, 0)),
            pl.BlockSpec((d, LANES), lambda i: (0, 0)),
        ],
        out_specs=pl.BlockSpec((tm, LANES), lambda i: (i, 0)),
        out_shape=jax.ShapeDtypeStruct((t, LANES), F32),
        compiler_params=_params("parallel"),
        name="router",
    )(x, gain, wr)


def _moe_kernel(tok_ref, dst_ref, texp_ref, tval_ref, x_hbm, g_ref, wg_ref, wu_ref, wd_ref,
                y_hbm, xbuf, obuf, h_scr, acc_scr, gsem, ssem, *, tm, nf):
    m = pl.program_id(0)
    f = pl.program_id(1)
    last_phase = pl.num_programs(0) - 1
    slot = m % 2
    other = 1 - slot
    valid = tval_ref[m] > 0
    per_step = (tm // nf) // 8 * 8
    in_steps = per_step * nf

    def gather_row(tile, r, s):
        tok = tok_ref[tile * tm + r]
        return pltpu.make_async_copy(x_hbm.at[pl.ds(tok, 1), :], xbuf.at[s, pl.ds(r, 1), :],
                                     gsem.at[s])

    def scatter_row(phase, r, s):
        dst = dst_ref[phase * tm + r]
        return pltpu.make_async_copy(obuf.at[s, pl.ds(r, 1), :], y_hbm.at[pl.ds(dst, 1), :],
                                     ssem.at[s])

    wait_group = 16

    def gather_wait(tile, s):
        def body(it, c):
            for u in range(wait_group):
                gather_row(tile, it * wait_group + u, s).wait()
            return c
        lax.fori_loop(0, tm // wait_group, body, 0)

    def scatter_wait(phase, s):
        def body(it, c):
            for u in range(wait_group):
                scatter_row(phase, it * wait_group + u, s).wait()
            return c
        lax.fori_loop(0, tm // wait_group, body, 0)

    @pl.when(f == 0)
    def _():
        @pl.when(m == 0)
        def _():
            obuf[...] = jnp.zeros_like(obuf)

            def first(r, c):
                gather_row(0, r, 0).start()
                return c
            lax.fori_loop(0, tm, first, 0, unroll=8)

        @pl.when(m >= 1)
        def _():
            scatter_wait(m - 1, slot)
        gather_wait(m, slot)
        for r in range(in_steps, tm):
            gather_row(m + 1, r, other).start()
            scatter_row(m, r, other).start()
        h_scr[...] = _rms(xbuf[slot], g_ref[...]).astype(BF16)
        acc_scr[...] = jnp.zeros_like(acc_scr)

    def move_rows():
        r0 = pl.multiple_of(f * per_step, 8)
        for rr in range(per_step):
            r = r0 + rr
            gather_row(m + 1, r, other).start()
            scatter_row(m, r, other).start()

    @pl.when(valid)
    def _():
        move_rows()
        h = h_scr[...]
        a = _silu(_dot(h, wg_ref[0])) * _dot(h, wu_ref[0])
        acc_scr[...] += _dot(a.astype(BF16), wd_ref[0])

    @pl.when(jnp.logical_not(valid))
    def _():
        move_rows()

    @pl.when(f == nf - 1)
    def _():
        obuf[slot] = jnp.where(valid, acc_scr[...], 0.0)

        @pl.when(m == last_phase)
        def _():
            scatter_wait(m, other)
            gather_wait(m + 1, other)


def _moe_experts(x, gain, wg, wu, wd, tok, dst, tile_expert, tile_valid, n_rows, *, tm, tf=512):
    t, d = x.shape
    ff = wg.shape[2]
    n_phases = tile_expert.shape[0]
    nf = ff // tf
    kern = functools.partial(_moe_kernel, tm=tm, nf=nf)

    def fsel(m, f, tval):
        return jnp.where(tval[m] > 0, f, nf - 1)

    grid_spec = pltpu.PrefetchScalarGridSpec(
        num_scalar_prefetch=4,
        grid=(n_phases, nf),
        in_specs=[
            pl.BlockSpec(memory_space=pl.ANY),
            pl.BlockSpec((1, d), lambda m, f, tok, dst, texp, tval: (0, 0)),
            pl.BlockSpec((1, d, tf), lambda m, f, tok, dst, texp, tval: (texp[m], 0, fsel(m, f, tval))),
            pl.BlockSpec((1, d, tf), lambda m, f, tok, dst, texp, tval: (texp[m], 0, fsel(m, f, tval))),
            pl.BlockSpec((1, tf, d), lambda m, f, tok, dst, texp, tval: (texp[m], fsel(m, f, tval), 0)),
        ],
        out_specs=pl.BlockSpec(memory_space=pl.ANY),
        scratch_shapes=[
            pltpu.VMEM((2, tm, d), F32),
            pltpu.VMEM((2, tm, d), F32),
            pltpu.VMEM((tm, d), BF16),
            pltpu.VMEM((tm, d), F32),
            pltpu.SemaphoreType.DMA((2,)),
            pltpu.SemaphoreType.DMA((2,)),
        ],
    )
    return pl.pallas_call(
        kern,
        grid_spec=grid_spec,
        out_shape=jax.ShapeDtypeStruct((n_rows, d), F32),
        compiler_params=_params("arbitrary", "arbitrary"),
        name="moe_experts",
    )(tok, dst, tile_expert, tile_valid, x, gain, wg, wu, wd)


def _combine_kernel(x_ref, r_ref, y0_ref, y1_ref, g_ref, o_ref, *, out_norm):
    route = r_ref[...]
    out = (x_ref[...] + route[:, TOP_K:TOP_K + 1] * y0_ref[...]
           + route[:, TOP_K + 1:TOP_K + 2] * y1_ref[...])
    o_ref[...] = _rms(out, g_ref[...]) if out_norm else out


def _moe_combine(x, route, y, out_gain, *, tm=512):
    t, d = x.shape
    assert TOP_K == 2
    out_norm = out_gain is not None
    gain = out_gain if out_norm else jnp.ones((1, d), F32)
    return pl.pallas_call(
        functools.partial(_combine_kernel, out_norm=out_norm),
        grid=(t // tm,),
        in_specs=[
            pl.BlockSpec((tm, d), lambda i: (i, 0)),
            pl.BlockSpec((tm, LANES), lambda i: (i, 0)),
            pl.BlockSpec((tm, d), lambda i: (i, 0)),
            pl.BlockSpec((tm, d), lambda i: (t // tm + i, 0)),
            pl.BlockSpec((1, d), lambda i: (0, 0)),
        ],
        out_specs=pl.BlockSpec((tm, d), lambda i: (i, 0)),
        out_shape=jax.ShapeDtypeStruct((t, d), F32),
        compiler_params=_params("parallel"),
        name="moe_combine",
    )(x, route, y, y, gain)


def _moe(x, gain, w_router, wg, wu, wd, out_gain=None, *, tm=512):
    t, d = x.shape
    wr = jnp.zeros((d, LANES), F32).at[:, :N_EXPERTS].set(w_router)
    route = _router(x, gain, wr)
    tok, dst, tile_expert, tile_valid, n_rows = _route_metadata(route[:, :TOP_K].astype(I32), tm)
    y = _moe_experts(x, gain, wg, wu, wd, tok, dst, tile_expert, tile_valid, n_rows, tm=tm)
    return _moe_combine(x, route, y, out_gain)


def _route_metadata(experts, tm):
    t = experts.shape[0]
    n_slots = TOP_K * t
    eflat = experts.T.reshape(-1)
    onehot = (eflat[:, None] == jnp.arange(N_EXPERTS, dtype=I32)[None, :]).astype(I32)
    csum = jnp.cumsum(onehot, axis=0)
    rank = jnp.sum((csum - 1) * onehot, axis=1)
    counts = csum[-1]
    tiles_per = (counts + tm - 1) // tm
    tile_end = jnp.cumsum(tiles_per)
    start = (tile_end - tiles_per) * tm
    pos = (jnp.sum(start[None, :] * onehot, axis=1) + rank).astype(I32)
    n_tiles = n_slots // tm + N_EXPERTS
    n_rows_pad = n_tiles * tm
    slot_of_row = jnp.full((n_rows_pad,), -1, I32).at[pos].set(jnp.arange(n_slots, dtype=I32))
    real = slot_of_row >= 0
    tok = jnp.concatenate([jnp.where(real, slot_of_row % t, 0), jnp.zeros((2 * tm,), I32)])
    real_ext = jnp.concatenate([jnp.zeros((tm,), bool), real])
    dump = n_slots + jnp.cumsum(jnp.logical_not(real_ext).astype(I32)) - 1
    dst = jnp.where(real_ext, jnp.concatenate([jnp.zeros((tm,), I32), slot_of_row]), dump)
    tile_id = jnp.arange(n_tiles + 1, dtype=I32)
    tile_expert = jnp.sum((tile_id[:, None] >= tile_end[None, :]).astype(I32), axis=1)
    tile_valid = (tile_id < tile_end[-1]).astype(I32)
    last_expert = jnp.max(jnp.where(counts > 0, jnp.arange(N_EXPERTS, dtype=I32), 0))
    tile_expert = jnp.where(tile_valid > 0, tile_expert, last_expert).astype(I32)
    return tok, dst.astype(I32), tile_expert, tile_valid, n_rows_pad + tm


def _final_norm_kernel(x_ref, g_ref, o_ref):
    o_ref[...] = _rms(x_ref[...], g_ref[...])


def _final_norm(x, gain, *, tm=1024):
    t, d = x.shape
    return pl.pallas_call(
        _final_norm_kernel,
        grid=(t // tm,),
        in_specs=[pl.BlockSpec((tm, d), lambda i: (i, 0)), pl.BlockSpec((1, d), lambda i: (0, 0))],
        out_specs=pl.BlockSpec((tm, d), lambda i: (i, 0)),
        out_shape=jax.ShapeDtypeStruct((t, d), F32),
        compiler_params=_params("parallel"),
        name="final_norm",
    )(x, gain)


def _rope_tables(seq):
    half = HEAD_DIM // 2
    inv_freq = ROPE_THETA ** (-jnp.arange(half, dtype=F32) / half)
    ang = jnp.arange(seq, dtype=F32)[:, None] * inv_freq[None, :]
    cos, sin = jnp.cos(ang), jnp.sin(ang)
    reps = LANES // HEAD_DIM
    cos_t = jnp.tile(jnp.concatenate([cos, cos], axis=1), (1, reps))
    sin_t = jnp.tile(jnp.concatenate([-sin, sin], axis=1), (1, reps))
    return cos_t, sin_t


def _in_proj_weight(w_in):
    o = 0
    offs = {}
    for name, width in (("qa", A_W), ("ka", A_W), ("va", A_W), ("qb", B_QW), ("kb", B_KVW),
                        ("vb", B_KVW), ("qc", C_W), ("kc", C_W), ("vc", C_W), ("g", 3 * D_MODEL)):
        offs[name] = (o, o + width)
        o += width
    scale = HEAD_DIM ** -0.5
    log2e = 1.4426950408889634
    q_scale = {"qa": scale * log2e, "qb": scale, "qc": scale * log2e}
    parts = []
    for name in ("g", "qa", "ka", "qb", "kb", "vb", "va", "qc", "kc", "vc"):
        lo, hi = offs[name]
        blk = w_in[:, lo:hi]
        if name in q_scale:
            blk = blk * q_scale[name]
        parts.append(blk)
    return jnp.concatenate(parts, axis=1).astype(BF16)


def _rope_group_map():
    groups = {}
    for t in range(IN_W // PROJ_TN):
        lo, hi = t * PROJ_TN, (t + 1) * PROJ_TN
        n = (min(hi, ROPE_HI) - max(lo, ROPE_LO)) // LANES
        if n > 0:
            assert max(lo, ROPE_LO) == lo
            groups[t] = n
    return groups


def kernel(x, mem, norm_mix, w_in, w_proj_a, w_proj_b, w_proj_c, w_mix_out, sinks, norm_cross,
           norm_mem, w_xq, w_xkv, w_xo, norm_ffn, ffn_gate, ffn_up, ffn_down, moe_router,
           moe_gate, moe_up, moe_down, final_norm):
    batch, seq, d = x.shape
    depth = norm_mix.shape[0]
    assert d == D_MODEL and seq % 1024 == 0 and mem.shape[1] == MEM_LEN
    t = batch * seq
    xf = x.reshape(t, d)
    memf = mem.reshape(batch * MEM_LEN, d)
    cos_t, sin_t = _rope_tables(seq)
    rope_groups = _rope_group_map()
    ones_tab = jnp.ones((MEM_LEN, LANES), F32)

    for l in range(depth):
        gain = lambda g: g[l].reshape(1, d)
        p = _norm_proj(xf, gain(norm_mix), _in_proj_weight(w_in[l]), cos_t, sin_t, seq,
                       tm=1024, tn=PROJ_TN, rope_groups=rope_groups)
        oa = _moba(p, batch, seq)
        ob = _swa(p, sinks[l], batch, seq)
        oc = _stick(p, batch, seq)
        xf = _merge(xf, oa, ob, oc, p, w_proj_a[l].astype(BF16), w_proj_b[l].astype(BF16),
                    w_proj_c[l].astype(BF16), w_mix_out[l].astype(BF16))
        kv = _norm_proj(memf, gain(norm_mem), w_xkv[l].astype(BF16), ones_tab, ones_tab, MEM_LEN,
                        tm=MEM_LEN, tn=2 * X_W, rope_groups={})
        xf = _cross(xf, gain(norm_cross), w_xq[l].astype(BF16), kv, w_xo[l].astype(BF16), seq)
        if l % 2 == 0:
            i = l // 2
            xf = _ffn(xf, gain(norm_ffn), ffn_gate[i].astype(BF16), ffn_up[i].astype(BF16),
                      ffn_down[i].astype(BF16))
        else:
            i = l // 2
            out_gain = final_norm.reshape(1, d) if l == depth - 1 else None
            xf = _moe(xf, gain(norm_ffn), moe_router[i], moe_gate[i].astype(BF16),
                      moe_up[i].astype(BF16), moe_down[i].astype(BF16), out_gain)
    if depth % 2 == 1:
        xf = _final_norm(xf, final_norm.reshape(1, d))
    return xf.reshape(batch, seq, d)
```

```python
import functools

import jax
import jax.numpy as jnp
from jax import lax
from jax.experimental import pallas as pl
from jax.experimental.pallas import tpu as pltpu

F32 = jnp.float32
BF16 = jnp.bfloat16
I32 = jnp.int32

D_MODEL = 1024
HEAD_DIM = 64
A_HEADS = 4
MOBA_BLOCK = 256
MOBA_TOPK = 3
B_HEADS = 8
B_KV_HEADS = 2
WINDOW = 128
C_HEADS = 4
SB_BLOCK = 256
MEM_LEN = 256
X_HEADS = 4
X_HEAD_DIM = 128
N_EXPERTS = 8
TOP_K = 2
ROPE_THETA = 10000.0
EPS = 1e-6

A_W = A_HEADS * HEAD_DIM
B_QW = B_HEADS * HEAD_DIM
B_KVW = B_KV_HEADS * HEAD_DIM
C_W = C_HEADS * HEAD_DIM
X_W = X_HEADS * X_HEAD_DIM
QKV_W = 3 * A_W + B_QW + 2 * B_KVW + 3 * C_W
IN_W = QKV_W + 3 * D_MODEL

LANES = 128
NEG = -1e30

COL_GA, COL_GB, COL_GC = 0, D_MODEL, 2 * D_MODEL
COL_QA = 3 * D_MODEL
COL_KA = COL_QA + A_W
COL_QB = COL_KA + A_W
COL_KB = COL_QB + B_QW
COL_VB = COL_KB + B_KVW
COL_VA = COL_VB + B_KVW
COL_QC = COL_VA + A_W
COL_KC = COL_QC + C_W
COL_VC = COL_KC + C_W
ROPE_LO, ROPE_HI = COL_QA, COL_VB

PROJ_TN = 768
FFN_TF = 256
MOE_TF = 512
VMEM_LIMIT = 48 * 1024 * 1024


def _params(*sem):
    return pltpu.CompilerParams(dimension_semantics=sem, vmem_limit_bytes=VMEM_LIMIT)


def _rms(x, g):
    ms = jnp.mean(x * x, axis=-1, keepdims=True)
    return x * lax.rsqrt(ms + EPS) * g


def _dot(a, b):
    return jnp.dot(a, b, preferred_element_type=F32)


def _dot_t(a, b):
    return lax.dot_general(a, b, (((1,), (1,)), ((), ())), preferred_element_type=F32)


def _split_bf16(v):
    hi = v.astype(BF16)
    lo = (v - hi.astype(F32)).astype(BF16)
    return hi, lo


def _norm_proj_kernel(x_ref, g_ref, w_ref, cos_ref, sin_ref, o_ref, h_scr, *, rope_groups):
    j = pl.program_id(1)

    @pl.when(j == 0)
    def _():
        h_scr[...] = _rms(x_ref[...], g_ref[...]).astype(BF16)

    acc = _dot(h_scr[...], w_ref[...])
    n_groups = acc.shape[1] // LANES

    def rope(y):
        lane = lax.broadcasted_iota(I32, y.shape, 1)
        first_half = (lane % HEAD_DIM) < (HEAD_DIM // 2)
        sw = jnp.where(first_half, pltpu.roll(y, LANES - HEAD_DIM // 2, 1),
                       pltpu.roll(y, HEAD_DIM // 2, 1))
        return y * cos_ref[...] + sw * sin_ref[...]

    roped = sorted(rope_groups)

    for t in roped:
        @pl.when(j == t)
        def _(t=t):
            for gi in range(n_groups):
                y = acc[:, gi * LANES:(gi + 1) * LANES]
                if gi < rope_groups[t]:
                    y = rope(y)
                o_ref[:, gi * LANES:(gi + 1) * LANES] = y.astype(o_ref.dtype)

    is_plain = j >= 0
    for t in roped:
        is_plain = jnp.logical_and(is_plain, j != t)

    @pl.when(is_plain)
    def _():
        o_ref[...] = acc.astype(o_ref.dtype)


def _norm_proj(x, gain, w, cos, sin, seq, *, tm, tn, rope_groups):
    t, d = x.shape
    n = w.shape[1]
    pos_blocks = seq // tm
    kern = functools.partial(_norm_proj_kernel, rope_groups=rope_groups)
    return pl.pallas_call(
        kern,
        grid=(t // tm, n // tn),
        in_specs=[
            pl.BlockSpec((tm, d), lambda i, j: (i, 0)),
            pl.BlockSpec((1, d), lambda i, j: (0, 0)),
            pl.BlockSpec((d, tn), lambda i, j: (0, j)),
            pl.BlockSpec((tm, LANES), lambda i, j: (i % pos_blocks, 0)),
            pl.BlockSpec((tm, LANES), lambda i, j: (i % pos_blocks, 0)),
        ],
        out_specs=pl.BlockSpec((tm, tn), lambda i, j: (i, j)),
        out_shape=jax.ShapeDtypeStruct((t, n), BF16),
        scratch_shapes=[pltpu.VMEM((tm, d), BF16)],
        compiler_params=_params("parallel", "arbitrary"),
        name="norm_proj",
    )(x, gain, w, cos, sin)


MOBA_VROWS = HEAD_DIM + 16


def _moba_kernel(qt_ref, k_ref, vt_ref, o_ref, kmean_scr, kaug_scr, vaug_scr, qaug_scr, *, nblk):
    i = pl.program_id(1)
    blk = MOBA_BLOCK
    hd = HEAD_DIM
    aug = 2 * hd
    vr = MOBA_VROWS
    heads = range(A_HEADS)
    hsl = [slice(hh * hd, (hh + 1) * hd) for hh in heads]

    @pl.when(i == 0)
    def _():
        ones = jnp.ones((vr - hd, blk), BF16)
        blk_lane = lax.broadcasted_iota(I32, (blk, hd), 1)
        for n in range(nblk):
            rows = slice(n * blk, (n + 1) * blk)
            kmean_scr[n:n + 1, :] = jnp.mean(k_ref[rows, :].astype(F32), axis=0, keepdims=True)
            onehot = jnp.where(blk_lane == n, 1.0, 0.0).astype(BF16)
            for hh in heads:
                kaug_scr[rows, hh * aug:hh * aug + hd] = k_ref[rows, hsl[hh]]
                kaug_scr[rows, hh * aug + hd:(hh + 1) * aug] = onehot
                vaug_scr[n, hh * vr:hh * vr + hd, :] = vt_ref[n, hsl[hh], :]
                vaug_scr[n, hh * vr + hd:(hh + 1) * vr, :] = ones

    key = lax.broadcasted_iota(I32, (blk, blk), 0)
    qry = lax.broadcasted_iota(I32, (blk, blk), 1)
    causal = key <= qry
    blk_id = lax.broadcasted_iota(I32, (nblk, blk), 0)
    own0 = pl.multiple_of(i * blk, blk)

    init = []
    for hh in heads:
        qt = qt_ref[hsl[hh], :]
        km_hi, km_lo = _split_bf16(kmean_scr[:, hsl[hh]])
        g = _dot(km_hi, qt) + _dot(km_lo, qt)

        cnt = jnp.zeros((nblk, blk), I32)
        for m in range(nblk):
            gm = g[m:m + 1, :]
            beats = (gm > g) | ((gm == g) & (m < blk_id))
            cnt = cnt + jnp.where(beats, (m < i).astype(I32), 0)
        sel = (cnt < MOBA_TOPK) & (blk_id < i)
        qaug_scr[hh * aug:hh * aug + hd, :] = qt
        qaug_scr[hh * aug + hd:hh * aug + hd + nblk, :] = jnp.where(sel, 0.0, NEG).astype(BF16)
        qaug_scr[hh * aug + hd + nblk:(hh + 1) * aug, :] = jnp.zeros((hd - nblk, blk), BF16)

        s = _dot(k_ref[pl.ds(own0, blk), hsl[hh]], qt)
        s = jnp.where(causal, s, NEG)
        m0 = jnp.max(s, axis=0, keepdims=True)
        p = jnp.exp2(s - m0)
        init.append((m0, _dot(vaug_scr[i, hh * vr:(hh + 1) * vr, :], p.astype(BF16))))

    def step(blocks, carry):
        rows = [pl.ds(pl.multiple_of(j * blk, blk), blk) for j in blocks]
        ss = [[_dot(kaug_scr[r, hh * aug:(hh + 1) * aug], qaug_scr[hh * aug:(hh + 1) * aug, :])
               for r in rows] for hh in heads]
        ms = []
        for hh in heads:
            m_new = carry[hh][0]
            for s in ss[hh]:
                m_new = jnp.maximum(m_new, jnp.max(s, axis=0, keepdims=True))
            ms.append(m_new)
        ps = [[jnp.exp2(s - ms[hh]).astype(BF16) for s in ss[hh]] for hh in heads]
        pvs = []
        for hh in heads:
            pv = None
            for j, p in zip(blocks, ps[hh]):
                d = _dot(vaug_scr[j, hh * vr:(hh + 1) * vr, :], p)
                pv = d if pv is None else pv + d
            pvs.append(pv)
        return tuple((ms[hh], jnp.exp2(carry[hh][0] - ms[hh]) * carry[hh][1] + pvs[hh])
                     for hh in heads)

    fin = lax.fori_loop(0, i // 2, lambda jp, c: step([2 * jp, 2 * jp + 1], c), tuple(init))
    fin = lax.fori_loop(0, i % 2, lambda _, c: step([i - 1], c), fin)
    out_t = jnp.concatenate([acc[:hd, :] / acc[hd:hd + 1, :] for _, acc in fin], axis=0)
    o_ref[...] = out_t.T.astype(o_ref.dtype)


def _moba(p, batch, seq):
    nblk = seq // MOBA_BLOCK
    assert nblk <= HEAD_DIM
    qt = _blocks_t(p, COL_QA, A_W, batch, seq, MOBA_BLOCK)
    vt = _blocks_t(p, COL_VA, A_W, batch, seq, MOBA_BLOCK)
    kern = functools.partial(_moba_kernel, nblk=nblk)
    return pl.pallas_call(
        kern,
        grid=(batch, nblk),
        in_specs=[
            pl.BlockSpec((None, None, A_W, MOBA_BLOCK), lambda b, i: (b, i, 0, 0)),
            pl.BlockSpec((seq, A_W), lambda b, i: (b, COL_KA // A_W)),
            pl.BlockSpec((None, nblk, A_W, MOBA_BLOCK), lambda b, i: (b, 0, 0, 0)),
        ],
        out_specs=pl.BlockSpec((MOBA_BLOCK, A_W), lambda b, i: (b * nblk + i, 0)),
        out_shape=jax.ShapeDtypeStruct((batch * seq, A_W), BF16),
        scratch_shapes=[
            pltpu.VMEM((nblk, A_W), F32),
            pltpu.VMEM((seq, 2 * A_W), BF16),
            pltpu.VMEM((nblk, A_HEADS * MOBA_VROWS, MOBA_BLOCK), BF16),
            pltpu.VMEM((2 * A_W, MOBA_BLOCK), BF16),
        ],
        compiler_params=_params("parallel", "arbitrary"),
        name="moba",
    )(qt, p, vt)


def _swa_kernel(sinks_ref, q_ref, kv_ref, pkv_ref, o_ref, kv_scr, *, tq):
    i = pl.program_id(1)
    w = WINDOW
    kv_scr[0:w, :] = pkv_ref[...]
    kv_scr[w:, :] = kv_ref[...]

    row = lax.broadcasted_iota(I32, (w, 2 * w), 0)
    col = lax.broadcasted_iota(I32, (w, 2 * w), 1)
    rel = row + w - col
    band = (rel >= 0) & (rel < w)
    group = B_HEADS // B_KV_HEADS

    def body(n, carry):
        base = pl.multiple_of(n * w, w)
        kvt = kv_scr[pl.ds(base, 2 * w), :]
        qt = q_ref[pl.ds(base, w), :]
        col_min = jnp.where(jnp.logical_and(i == 0, n == 0), w, 0)
        mask = band & (col >= col_min)
        outs = []
        for kh in range(B_KV_HEADS):
            kk = kvt[:, kh * HEAD_DIM:(kh + 1) * HEAD_DIM]
            vv = kvt[:, B_KVW + kh * HEAD_DIM:B_KVW + (kh + 1) * HEAD_DIM]
            for gi in range(group):
                hd = kh * group + gi
                qh = qt[:, hd * HEAD_DIM:(hd + 1) * HEAD_DIM]
                s = jnp.where(mask, _dot_t(qh, kk), NEG)
                sink = sinks_ref[hd]
                m = jnp.maximum(jnp.max(s, axis=-1, keepdims=True), sink)
                p = jnp.exp(s - m)
                l = jnp.sum(p, axis=-1, keepdims=True) + jnp.exp(sink - m)
                outs.append(_dot(p.astype(BF16), vv) / l)
        o_ref[pl.ds(base, w), :] = jnp.concatenate(outs, axis=1).astype(o_ref.dtype)
        return carry

    lax.fori_loop(0, tq // w, body, 0)


def _swa(p, sinks, batch, seq, *, tq=512):
    nq = seq // tq
    sub = tq // WINDOW
    kern = functools.partial(_swa_kernel, tq=tq)
    return pl.pallas_call(
        kern,
        grid=(batch, nq),
        in_specs=[
            pl.BlockSpec(memory_space=pltpu.SMEM),
            pl.BlockSpec((tq, B_QW), lambda b, i: (b * nq + i, COL_QB // B_QW)),
            pl.BlockSpec((tq, 2 * B_KVW), lambda b, i: (b * nq + i, COL_KB // (2 * B_KVW))),
            pl.BlockSpec((WINDOW, 2 * B_KVW),
                         lambda b, i: (jnp.maximum((b * nq + i) * sub - 1, 0),
                                       COL_KB // (2 * B_KVW))),
        ],
        out_specs=pl.BlockSpec((tq, B_QW), lambda b, i: (b * nq + i, 0)),
        out_shape=jax.ShapeDtypeStruct((batch * seq, B_QW), BF16),
        scratch_shapes=[pltpu.VMEM((tq + WINDOW, 2 * B_KVW), BF16)],
        compiler_params=_params("parallel", "parallel"),
        name="swa",
    )(sinks, p, p, p)


def _stick_kernel(qt_ref, k_ref, vt_ref, o_ref):
    i = pl.program_id(1)
    blk = SB_BLOCK
    hd = HEAD_DIM
    heads = range(C_HEADS)
    hsl = [slice(hh * hd, (hh + 1) * hd) for hh in heads]
    key = lax.broadcasted_iota(I32, (blk, blk), 0)
    qry = lax.broadcasted_iota(I32, (blk, blk), 1)
    causal = key < qry
    tri = jnp.where(qry > key, 1.0, 0.0).astype(BF16)
    own0 = pl.multiple_of(i * blk, blk)

    def logs(z):
        t = jnp.log2(1.0 + jnp.exp2(-jnp.abs(z)))
        log_beta = jnp.minimum(z, 0.0) - t
        return log_beta, log_beta - z

    def suffix(lk):
        after = _dot(tri, lk.astype(BF16))
        total = after[0:1, :] + lk[0:1, :]
        return after, total

    qts = [qt_ref[hsl[hh], :] for hh in heads]
    init = []
    for hh in heads:
        z = _dot(k_ref[pl.ds(own0, blk), hsl[hh]], qts[hh])
        log_beta, log_keep = logs(z)
        after, carry0 = suffix(jnp.where(causal, log_keep, 0.0))
        wgt = jnp.where(causal, jnp.exp2(log_beta + after), 0.0)
        init.append((carry0, _dot(vt_ref[i, hsl[hh], :], wgt.astype(BF16))))

    def step(blocks, state):
        rows = [pl.ds(j * blk if isinstance(j, int) else pl.multiple_of(j * blk, blk), blk)
                for j in blocks]
        pairs = [(hh, b) for hh in heads for b in range(len(blocks))]
        zs = {(hh, b): _dot(k_ref[rows[b], hsl[hh]], qts[hh]) for hh, b in pairs}
        lgs = {k: logs(zs[k]) for k in pairs}
        sfx = {k: suffix(lgs[k][1]) for k in pairs}
        ws = {k: jnp.exp2(lgs[k][0] + sfx[k][0]).astype(BF16) for k in pairs}
        pvs = {(hh, b): _dot(vt_ref[blocks[b], hsl[hh], :], ws[(hh, b)]) for hh, b in pairs}
        out = []
        for hh in heads:
            carry, acc = state[hh]
            for b in range(len(blocks)):
                acc = acc + pvs[(hh, b)] * jnp.exp2(carry)
                carry = carry + sfx[(hh, b)][1]
            out.append((carry, acc))
        return tuple(out)

    fin = lax.fori_loop(0, i // 2, lambda jp, s: step([i - 1 - 2 * jp, i - 2 - 2 * jp], s),
                        tuple(init))
    fin = lax.fori_loop(0, i % 2, lambda _, s: step([0], s), fin)
    out_t = jnp.concatenate([acc for _, acc in fin], axis=0)
    o_ref[...] = out_t.T.astype(o_ref.dtype)


def _blocks_t(p, col, width, batch, seq, blk):
    t = p[:, col:col + width].reshape(batch, seq // blk, blk, width)
    return jnp.swapaxes(t, 2, 3)


def _stick(p, batch, seq):
    nblk = seq // SB_BLOCK
    qt = _blocks_t(p, COL_QC, C_W, batch, seq, SB_BLOCK)
    vt = _blocks_t(p, COL_VC, C_W, batch, seq, SB_BLOCK)
    return pl.pallas_call(
        _stick_kernel,
        grid=(batch, nblk),
        in_specs=[
            pl.BlockSpec((None, None, C_W, SB_BLOCK), lambda b, i: (b, i, 0, 0)),
            pl.BlockSpec((seq, C_W), lambda b, i: (b, COL_KC // C_W)),
            pl.BlockSpec((None, nblk, C_W, SB_BLOCK), lambda b, i: (b, 0, 0, 0)),
        ],
        out_specs=pl.BlockSpec((SB_BLOCK, C_W), lambda b, i: (b * nblk + i, 0)),
        out_shape=jax.ShapeDtypeStruct((batch * seq, C_W), BF16),
        compiler_params=_params("parallel", "parallel"),
        name="stick",
    )(qt, p, vt)


def _sigmoid(x):
    return 1.0 / (1.0 + jnp.exp(-x))


def _merge_kernel(x_ref, oa_ref, ob_ref, oc_ref, ga_ref, gb_ref, gc_ref,
                  wa_ref, wb_ref, wc_ref, wo_ref, o_ref):
    mixed = _sigmoid(ga_ref[...].astype(F32)) * _dot(oa_ref[...], wa_ref[...])
    mixed = mixed + _sigmoid(gb_ref[...].astype(F32)) * _dot(ob_ref[...], wb_ref[...])
    mixed = mixed + _sigmoid(gc_ref[...].astype(F32)) * _dot(oc_ref[...], wc_ref[...])
    o_ref[...] = x_ref[...] + _dot(mixed.astype(BF16), wo_ref[...])


def _merge(x, oa, ob, oc, p, wa, wb, wc, wo, *, tm=512):
    t, d = x.shape
    full = lambda a: pl.BlockSpec(a.shape, lambda i: (0, 0))
    return pl.pallas_call(
        _merge_kernel,
        grid=(t // tm,),
        in_specs=[
            pl.BlockSpec((tm, d), lambda i: (i, 0)),
            pl.BlockSpec((tm, A_W), lambda i: (i, 0)),
            pl.BlockSpec((tm, B_QW), lambda i: (i, 0)),
            pl.BlockSpec((tm, C_W), lambda i: (i, 0)),
            pl.BlockSpec((tm, d), lambda i: (i, COL_GA // D_MODEL)),
            pl.BlockSpec((tm, d), lambda i: (i, COL_GB // D_MODEL)),
            pl.BlockSpec((tm, d), lambda i: (i, COL_GC // D_MODEL)),
            full(wa), full(wb), full(wc), full(wo),
        ],
        out_specs=pl.BlockSpec((tm, d), lambda i: (i, 0)),
        out_shape=jax.ShapeDtypeStruct((t, d), F32),
        compiler_params=_params("parallel"),
        name="merge",
    )(x, oa, ob, oc, p, p, p, wa, wb, wc, wo)


def _cross_kernel(x_ref, g_ref, wq_ref, kv_ref, wo_ref, o_ref):
    x = x_ref[...]
    h = _rms(x, g_ref[...]).astype(BF16)
    q = (_dot(h, wq_ref[...]) * (X_HEAD_DIM ** -0.5)).astype(BF16)
    outs = []
    for hd in range(X_HEADS):
        hs = slice(hd * X_HEAD_DIM, (hd + 1) * X_HEAD_DIM)
        s = _dot_t(q[:, hs], kv_ref[:, hs])
        m = jnp.max(s, axis=-1, keepdims=True)
        p = jnp.exp(s - m)
        l = jnp.sum(p, axis=-1, keepdims=True)
        vs = slice(X_W + hd * X_HEAD_DIM, X_W + (hd + 1) * X_HEAD_DIM)
        outs.append(_dot(p.astype(BF16), kv_ref[:, vs]) / l)
    att = jnp.concatenate(outs, axis=1).astype(BF16)
    o_ref[...] = x + _dot(att, wo_ref[...])


def _cross(x, gain, wq, kv, wo, seq, *, tm=512):
    t, d = x.shape
    per_batch = seq // tm
    full = lambda a: pl.BlockSpec(a.shape, lambda i: (0, 0))
    return pl.pallas_call(
        _cross_kernel,
        grid=(t // tm,),
        in_specs=[
            pl.BlockSpec((tm, d), lambda i: (i, 0)),
            full(gain), full(wq),
            pl.BlockSpec((MEM_LEN, 2 * X_W), lambda i: (i // per_batch, 0)),
            full(wo),
        ],
        out_specs=pl.BlockSpec((tm, d), lambda i: (i, 0)),
        out_shape=jax.ShapeDtypeStruct((t, d), F32),
        compiler_params=_params("parallel"),
        name="cross",
    )(x, gain, wq, kv, wo)


def _silu(x):
    return x * _sigmoid(x)


def _ffn_kernel(x_ref, g_ref, wg_ref, wu_ref, wd_ref, o_ref, h_scr, acc_scr):
    f = pl.program_id(1)

    @pl.when(f == 0)
    def _():
        h_scr[...] = _rms(x_ref[...], g_ref[...]).astype(BF16)
        acc_scr[...] = x_ref[...]

    h = h_scr[...]
    a = _silu(_dot(h, wg_ref[...])) * _dot(h, wu_ref[...])
    acc_scr[...] += _dot(a.astype(BF16), wd_ref[...])

    @pl.when(f == pl.num_programs(1) - 1)
    def _():
        o_ref[...] = acc_scr[...]


def _col_tiles(w, tf):
    *lead, d, ff = w.shape
    wt = w.reshape(*lead, d, ff // tf, tf)
    return jnp.swapaxes(wt, -3, -2).astype(BF16)


def _ffn(x, gain, wg, wu, wd, *, tm=1024, tf=256):
    t, d = x.shape
    ff = wd.shape[0]
    return pl.pallas_call(
        _ffn_kernel,
        grid=(t // tm, ff // tf),
        in_specs=[
            pl.BlockSpec((tm, d), lambda i, f: (i, 0)),
            pl.BlockSpec((1, d), lambda i, f: (0, 0)),
            pl.BlockSpec((None, d, tf), lambda i, f: (f, 0, 0)),
            pl.BlockSpec((None, d, tf), lambda i, f: (f, 0, 0)),
            pl.BlockSpec((tf, d), lambda i, f: (f, 0)),
        ],
        out_specs=pl.BlockSpec((tm, d), lambda i, f: (i, 0)),
        out_shape=jax.ShapeDtypeStruct((t, d), F32),
        scratch_shapes=[pltpu.VMEM((tm, d), BF16), pltpu.VMEM((tm, d), F32)],
        compiler_params=_params("parallel", "arbitrary"),
        name="ffn",
    )(x, gain, wg, wu, wd)


def _router_kernel(x_ref, g_ref, wr_ref, o_ref):
    h = _rms(x_ref[...], g_ref[...])
    h_hi, h_lo = _split_bf16(h)
    w_hi, w_lo = _split_bf16(wr_ref[...])
    logits = _dot(h_hi, w_hi) + (_dot(h_hi, w_lo) + _dot(h_lo, w_hi))
    lane = lax.broadcasted_iota(I32, logits.shape, 1)
    logits = jnp.where(lane < N_EXPERTS, logits, NEG)
    v1 = jnp.max(logits, axis=-1, keepdims=True)
    i1 = jnp.min(jnp.where(logits == v1, lane, LANES), axis=-1, keepdims=True)
    rest = jnp.where(lane == i1, NEG, logits)
    v2 = jnp.max(rest, axis=-1, keepdims=True)
    i2 = jnp.min(jnp.where(rest == v2, lane, LANES), axis=-1, keepdims=True)
    e = jnp.exp(v2 - v1)
    w1 = 1.0 / (1.0 + e)
    w2 = e / (1.0 + e)
    out = jnp.where(lane == 0, i1.astype(F32), 0.0)
    out = jnp.where(lane == 1, i2.astype(F32), out)
    out = jnp.where(lane == 2, w1, out)
    out = jnp.where(lane == 3, w2, out)
    o_ref[...] = out


def _router(x, gain, wr, *, tm=1024):
    t, d = x.shape
    return pl.pallas_call(
        _router_kernel,
        grid=(t // tm,),
        in_specs=[
            pl.BlockSpec((tm, d), lambda i: (i, 0)),
            pl.BlockSpec((1, d), lambda i: (0, 0)),
            pl.BlockSpec((d, LANES), lambda i: (0, 0)),
        ],
        out_specs=pl.BlockSpec((tm, LANES), lambda i: (i, 0)),
        out_shape=jax.ShapeDtypeStruct((t, LANES), F32),
        compiler_params=_params("parallel"),
        name="router",
    )(x, gain, wr)


def _moe_kernel(tok_ref, dst_ref, texp_ref, tval_ref, x_hbm, g_ref, wg_ref, wu_ref, wd_ref,
                y_hbm, xbuf, obuf, h_scr, acc_scr, gsem, ssem, *, tm, nf):
    m = pl.program_id(0)
    f = pl.program_id(1)
    last_phase = pl.num_programs(0) - 1
    slot = m % 2
    other = 1 - slot
    valid = tval_ref[m] > 0
    per_step = (tm // nf) // 8 * 8
    in_steps = per_step * nf

    def gather_row(tile, r, s):
        tok = tok_ref[tile * tm + r]
        return pltpu.make_async_copy(x_hbm.at[pl.ds(tok, 1), :], xbuf.at[s, pl.ds(r, 1), :],
                                     gsem.at[s])

    def scatter_row(phase, r, s):
        dst = dst_ref[phase * tm + r]
        return pltpu.make_async_copy(obuf.at[s, pl.ds(r, 1), :], y_hbm.at[pl.ds(dst, 1), :],
                                     ssem.at[s])

    wait_group = 16

    def gather_wait(tile, s):
        def body(it, c):
            for u in range(wait_group):
                gather_row(tile, it * wait_group + u, s).wait()
            return c
        lax.fori_loop(0, tm // wait_group, body, 0)

    def scatter_wait(phase, s):
        def body(it, c):
            for u in range(wait_group):
                scatter_row(phase, it * wait_group + u, s).wait()
            return c
        lax.fori_loop(0, tm // wait_group, body, 0)

    @pl.when(f == 0)
    def _():
        @pl.when(m == 0)
        def _():
            obuf[...] = jnp.zeros_like(obuf)

            def first(r, c):
                gather_row(0, r, 0).start()
                return c
            lax.fori_loop(0, tm, first, 0, unroll=8)

        @pl.when(m >= 1)
        def _():
            scatter_wait(m - 1, slot)
        gather_wait(m, slot)
        for r in range(in_steps, tm):
            gather_row(m + 1, r, other).start()
            scatter_row(m, r, other).start()
        h_scr[...] = _rms(xbuf[slot], g_ref[...]).astype(BF16)
        acc_scr[...] = jnp.zeros_like(acc_scr)

    def move_rows():
        r0 = pl.multiple_of(f * per_step, 8)
        for rr in range(per_step):
            r = r0 + rr
            gather_row(m + 1, r, other).start()
            scatter_row(m, r, other).start()

    @pl.when(valid)
    def _():
        move_rows()
        h = h_scr[...]
        a = _silu(_dot(h, wg_ref[...])) * _dot(h, wu_ref[...])
        acc_scr[...] += _dot(a.astype(BF16), wd_ref[...])

    @pl.when(jnp.logical_not(valid))
    def _():
        move_rows()

    @pl.when(f == nf - 1)
    def _():
        obuf[slot] = jnp.where(valid, acc_scr[...], 0.0)

        @pl.when(m == last_phase)
        def _():
            scatter_wait(m, other)
            gather_wait(m + 1, other)


def _moe_experts(x, gain, wg, wu, wd, tok, dst, tile_expert, tile_valid, n_rows, *, tm):
    t, d = x.shape
    nf, tf = wg.shape[1], wg.shape[3]
    n_phases = tile_expert.shape[0]
    kern = functools.partial(_moe_kernel, tm=tm, nf=nf)

    def fsel(m, f, tval):
        return jnp.where(tval[m] > 0, f, nf - 1)

    grid_spec = pltpu.PrefetchScalarGridSpec(
        num_scalar_prefetch=4,
        grid=(n_phases, nf),
        in_specs=[
            pl.BlockSpec(memory_space=pl.ANY),
            pl.BlockSpec((1, d), lambda m, f, tok, dst, texp, tval: (0, 0)),
            pl.BlockSpec((None, None, d, tf),
                         lambda m, f, tok, dst, texp, tval: (texp[m], fsel(m, f, tval), 0, 0)),
            pl.BlockSpec((None, None, d, tf),
                         lambda m, f, tok, dst, texp, tval: (texp[m], fsel(m, f, tval), 0, 0)),
            pl.BlockSpec((None, tf, d),
                         lambda m, f, tok, dst, texp, tval: (texp[m], fsel(m, f, tval), 0)),
        ],
        out_specs=pl.BlockSpec(memory_space=pl.ANY),
        scratch_shapes=[
            pltpu.VMEM((2, tm, d), F32),
            pltpu.VMEM((2, tm, d), F32),
            pltpu.VMEM((tm, d), BF16),
            pltpu.VMEM((tm, d), F32),
            pltpu.SemaphoreType.DMA((2,)),
            pltpu.SemaphoreType.DMA((2,)),
        ],
    )
    return pl.pallas_call(
        kern,
        grid_spec=grid_spec,
        out_shape=jax.ShapeDtypeStruct((n_rows, d), F32),
        compiler_params=_params("arbitrary", "arbitrary"),
        name="moe_experts",
    )(tok, dst, tile_expert, tile_valid, x, gain, wg, wu, wd)


def _combine_kernel(x_ref, r_ref, y0_ref, y1_ref, g_ref, o_ref, *, out_norm):
    route = r_ref[...]
    out = (x_ref[...] + route[:, TOP_K:TOP_K + 1] * y0_ref[...]
           + route[:, TOP_K + 1:TOP_K + 2] * y1_ref[...])
    o_ref[...] = _rms(out, g_ref[...]) if out_norm else out


def _moe_combine(x, route, y, out_gain, *, tm=512):
    t, d = x.shape
    assert TOP_K == 2
    out_norm = out_gain is not None
    gain = out_gain if out_norm else jnp.ones((1, d), F32)
    return pl.pallas_call(
        functools.partial(_combine_kernel, out_norm=out_norm),
        grid=(t // tm,),
        in_specs=[
            pl.BlockSpec((tm, d), lambda i: (i, 0)),
            pl.BlockSpec((tm, LANES), lambda i: (i, 0)),
            pl.BlockSpec((tm, d), lambda i: (i, 0)),
            pl.BlockSpec((tm, d), lambda i: (t // tm + i, 0)),
            pl.BlockSpec((1, d), lambda i: (0, 0)),
        ],
        out_specs=pl.BlockSpec((tm, d), lambda i: (i, 0)),
        out_shape=jax.ShapeDtypeStruct((t, d), F32),
        compiler_params=_params("parallel"),
        name="moe_combine",
    )(x, route, y, y, gain)


def _moe(x, gain, w_router, wg, wu, wd, out_gain=None, *, tm=1024):
    t, d = x.shape
    wr = jnp.zeros((d, LANES), F32).at[:, :N_EXPERTS].set(w_router)
    route = _router(x, gain, wr)
    tok, dst, tile_expert, tile_valid, n_rows = _route_metadata(route[:, :TOP_K].astype(I32), tm)
    y = _moe_experts(x, gain, wg, wu, wd, tok, dst, tile_expert, tile_valid, n_rows, tm=tm)
    return _moe_combine(x, route, y, out_gain)


def _route_metadata(experts, tm):
    t = experts.shape[0]
    n_slots = TOP_K * t
    eflat = experts.T.reshape(-1)
    onehot = (eflat[:, None] == jnp.arange(N_EXPERTS, dtype=I32)[None, :]).astype(I32)
    csum = jnp.cumsum(onehot, axis=0)
    rank = jnp.sum((csum - 1) * onehot, axis=1)
    counts = csum[-1]
    tiles_per = (counts + tm - 1) // tm
    tile_end = jnp.cumsum(tiles_per)
    start = (tile_end - tiles_per) * tm
    pos = (jnp.sum(start[None, :] * onehot, axis=1) + rank).astype(I32)
    n_tiles = n_slots // tm + N_EXPERTS
    n_rows_pad = n_tiles * tm
    slot_of_row = jnp.full((n_rows_pad,), -1, I32).at[pos].set(jnp.arange(n_slots, dtype=I32))
    real = slot_of_row >= 0
    tok = jnp.concatenate([jnp.where(real, slot_of_row % t, 0), jnp.zeros((2 * tm,), I32)])
    real_ext = jnp.concatenate([jnp.zeros((tm,), bool), real])
    dump = n_slots + jnp.cumsum(jnp.logical_not(real_ext).astype(I32)) - 1
    dst = jnp.where(real_ext, jnp.concatenate([jnp.zeros((tm,), I32), slot_of_row]), dump)
    tile_id = jnp.arange(n_tiles + 1, dtype=I32)
    tile_expert = jnp.sum((tile_id[:, None] >= tile_end[None, :]).astype(I32), axis=1)
    tile_valid = (tile_id < tile_end[-1]).astype(I32)
    last_expert = jnp.max(jnp.where(counts > 0, jnp.arange(N_EXPERTS, dtype=I32), 0))
    tile_expert = jnp.where(tile_valid > 0, tile_expert, last_expert).astype(I32)
    return tok, dst.astype(I32), tile_expert, tile_valid, n_rows_pad + tm


def _final_norm_kernel(x_ref, g_ref, o_ref):
    o_ref[...] = _rms(x_ref[...], g_ref[...])


def _final_norm(x, gain, *, tm=1024):
    t, d = x.shape
    return pl.pallas_call(
        _final_norm_kernel,
        grid=(t // tm,),
        in_specs=[pl.BlockSpec((tm, d), lambda i: (i, 0)), pl.BlockSpec((1, d), lambda i: (0, 0))],
        out_specs=pl.BlockSpec((tm, d), lambda i: (i, 0)),
        out_shape=jax.ShapeDtypeStruct((t, d), F32),
        compiler_params=_params("parallel"),
        name="final_norm",
    )(x, gain)


def _rope_tables(seq):
    half = HEAD_DIM // 2
    inv_freq = ROPE_THETA ** (-jnp.arange(half, dtype=F32) / half)
    ang = jnp.arange(seq, dtype=F32)[:, None] * inv_freq[None, :]
    cos, sin = jnp.cos(ang), jnp.sin(ang)
    reps = LANES // HEAD_DIM
    cos_t = jnp.tile(jnp.concatenate([cos, cos], axis=1), (1, reps))
    sin_t = jnp.tile(jnp.concatenate([-sin, sin], axis=1), (1, reps))
    return cos_t, sin_t


def _in_proj_weight(w_in):
    o = 0
    offs = {}
    for name, width in (("qa", A_W), ("ka", A_W), ("va", A_W), ("qb", B_QW), ("kb", B_KVW),
                        ("vb", B_KVW), ("qc", C_W), ("kc", C_W), ("vc", C_W), ("g", 3 * D_MODEL)):
        offs[name] = (o, o + width)
        o += width
    scale = HEAD_DIM ** -0.5
    log2e = 1.4426950408889634
    q_scale = {"qa": scale * log2e, "qb": scale, "qc": scale * log2e}
    parts = []
    for name in ("g", "qa", "ka", "qb", "kb", "vb", "va", "qc", "kc", "vc"):
        lo, hi = offs[name]
        blk = w_in[:, lo:hi]
        if name in q_scale:
            blk = blk * q_scale[name]
        parts.append(blk)
    return jnp.concatenate(parts, axis=1).astype(BF16)


def _rope_group_map():
    groups = {}
    for t in range(IN_W // PROJ_TN):
        lo, hi = t * PROJ_TN, (t + 1) * PROJ_TN
        n = (min(hi, ROPE_HI) - max(lo, ROPE_LO)) // LANES
        if n > 0:
            assert max(lo, ROPE_LO) == lo
            groups[t] = n
    return groups


def kernel(x, mem, norm_mix, w_in, w_proj_a, w_proj_b, w_proj_c, w_mix_out, sinks, norm_cross,
           norm_mem, w_xq, w_xkv, w_xo, norm_ffn, ffn_gate, ffn_up, ffn_down, moe_router,
           moe_gate, moe_up, moe_down, final_norm):
    batch, seq, d = x.shape
    depth = norm_mix.shape[0]
    assert d == D_MODEL and seq % 1024 == 0 and mem.shape[1] == MEM_LEN
    t = batch * seq
    xf = x.reshape(t, d)
    memf = mem.reshape(batch * MEM_LEN, d)
    cos_t, sin_t = _rope_tables(seq)
    rope_groups = _rope_group_map()
    ones_tab = jnp.ones((MEM_LEN, LANES), F32)

    for l in range(depth):
        gain = lambda g: g[l].reshape(1, d)
        p = _norm_proj(xf, gain(norm_mix), _in_proj_weight(w_in[l]), cos_t, sin_t, seq,
                       tm=1024, tn=PROJ_TN, rope_groups=rope_groups)
        oa = _moba(p, batch, seq)
        ob = _swa(p, sinks[l], batch, seq)
        oc = _stick(p, batch, seq)
        xf = _merge(xf, oa, ob, oc, p, w_proj_a[l].astype(BF16), w_proj_b[l].astype(BF16),
                    w_proj_c[l].astype(BF16), w_mix_out[l].astype(BF16))
        kv = _norm_proj(memf, gain(norm_mem), w_xkv[l].astype(BF16), ones_tab, ones_tab, MEM_LEN,
                        tm=MEM_LEN, tn=2 * X_W, rope_groups={})
        xf = _cross(xf, gain(norm_cross), w_xq[l].astype(BF16), kv, w_xo[l].astype(BF16), seq)
        if l % 2 == 0:
            i = l // 2
            xf = _ffn(xf, gain(norm_ffn), _col_tiles(ffn_gate[i], FFN_TF),
                      _col_tiles(ffn_up[i], FFN_TF), ffn_down[i].astype(BF16), tf=FFN_TF)
        else:
            i = l // 2
            out_gain = final_norm.reshape(1, d) if l == depth - 1 else None
            xf = _moe(xf, gain(norm_ffn), moe_router[i], _col_tiles(moe_gate[i], MOE_TF),
                      _col_tiles(moe_up[i], MOE_TF), moe_down[i].astype(BF16), out_gain)
    if depth % 2 == 1:
        xf = _final_norm(xf, final_norm.reshape(1, d))
    return xf.reshape(batch, seq, d)
```

```python
import functools

import jax
import jax.numpy as jnp
from jax import lax
from jax.experimental import pallas as pl
from jax.experimental.pallas import tpu as pltpu

F32 = jnp.float32
BF16 = jnp.bfloat16
I32 = jnp.int32

D_MODEL = 1024
HEAD_DIM = 64
A_HEADS = 4
MOBA_BLOCK = 256
MOBA_TOPK = 3
B_HEADS = 8
B_KV_HEADS = 2
WINDOW = 128
C_HEADS = 4
SB_BLOCK = 256
MEM_LEN = 256
X_HEADS = 4
X_HEAD_DIM = 128
N_EXPERTS = 8
TOP_K = 2
ROPE_THETA = 10000.0
EPS = 1e-6

A_W = A_HEADS * HEAD_DIM
B_QW = B_HEADS * HEAD_DIM
B_KVW = B_KV_HEADS * HEAD_DIM
C_W = C_HEADS * HEAD_DIM
X_W = X_HEADS * X_HEAD_DIM
QKV_W = 3 * A_W + B_QW + 2 * B_KVW + 3 * C_W
IN_W = QKV_W + 3 * D_MODEL

LANES = 128
SUBLANES = 8
NEG = -1e30

COL_GA, COL_GB, COL_GC = 0, D_MODEL, 2 * D_MODEL
COL_QA = 3 * D_MODEL
COL_KA = COL_QA + A_W
COL_QB = COL_KA + A_W
COL_KB = COL_QB + B_QW
COL_VB = COL_KB + B_KVW
COL_VA = COL_VB + B_KVW
COL_QC = COL_VA + A_W
COL_KC = COL_QC + C_W
COL_VC = COL_KC + C_W
ROPE_LO, ROPE_HI = COL_QA, COL_VB

PROJ_TN = 768
FFN_TF = 256
MOE_TF = 512
VMEM_LIMIT = 48 * 1024 * 1024


def _params(*sem):
    return pltpu.CompilerParams(dimension_semantics=sem, vmem_limit_bytes=VMEM_LIMIT)


def _rms(x, g):
    ms = jnp.mean(x * x, axis=-1, keepdims=True)
    return x * lax.rsqrt(ms + EPS) * g


def _dot(a, b):
    return jnp.dot(a, b, preferred_element_type=F32)


def _dot_t(a, b):
    return lax.dot_general(a, b, (((1,), (1,)), ((), ())), preferred_element_type=F32)


def _split_bf16(v):
    hi = v.astype(BF16)
    lo = (v - hi.astype(F32)).astype(BF16)
    return hi, lo


def _norm_proj_kernel(x_ref, g_ref, w_ref, cos_ref, sin_ref, o_ref, h_scr, *, rope_groups):
    j = pl.program_id(1)

    @pl.when(j == 0)
    def _():
        h_scr[...] = _rms(x_ref[...], g_ref[...]).astype(BF16)

    acc = _dot(h_scr[...], w_ref[...])
    n_groups = acc.shape[1] // LANES

    def rope(y):
        lane = lax.broadcasted_iota(I32, y.shape, 1)
        first_half = (lane % HEAD_DIM) < (HEAD_DIM // 2)
        sw = jnp.where(first_half, pltpu.roll(y, LANES - HEAD_DIM // 2, 1),
                       pltpu.roll(y, HEAD_DIM // 2, 1))
        return y * cos_ref[...] + sw * sin_ref[...]

    roped = sorted(rope_groups)

    for t in roped:
        @pl.when(j == t)
        def _(t=t):
            for gi in range(n_groups):
                y = acc[:, gi * LANES:(gi + 1) * LANES]
                if gi < rope_groups[t]:
                    y = rope(y)
                o_ref[:, gi * LANES:(gi + 1) * LANES] = y.astype(o_ref.dtype)

    is_plain = j >= 0
    for t in roped:
        is_plain = jnp.logical_and(is_plain, j != t)

    @pl.when(is_plain)
    def _():
        o_ref[...] = acc.astype(o_ref.dtype)


def _norm_proj(x, gain, w, cos, sin, seq, *, tm, tn, rope_groups):
    t, d = x.shape
    n = w.shape[1]
    pos_blocks = seq // tm
    kern = functools.partial(_norm_proj_kernel, rope_groups=rope_groups)
    return pl.pallas_call(
        kern,
        grid=(t // tm, n // tn),
        in_specs=[
            pl.BlockSpec((tm, d), lambda i, j: (i, 0)),
            pl.BlockSpec((1, d), lambda i, j: (0, 0)),
            pl.BlockSpec((d, tn), lambda i, j: (0, j)),
            pl.BlockSpec((tm, LANES), lambda i, j: (i % pos_blocks, 0)),
            pl.BlockSpec((tm, LANES), lambda i, j: (i % pos_blocks, 0)),
        ],
        out_specs=pl.BlockSpec((tm, tn), lambda i, j: (i, j)),
        out_shape=jax.ShapeDtypeStruct((t, n), BF16),
        scratch_shapes=[pltpu.VMEM((tm, d), BF16)],
        compiler_params=_params("parallel", "arbitrary"),
        name="norm_proj",
    )(x, gain, w, cos, sin)


MOBA_VROWS = HEAD_DIM + 16


def _moba_kernel(qt_ref, k_ref, vt_ref, o_ref, kmean_scr, kaug_scr, vaug_scr, qaug_scr, *, nblk):
    i = pl.program_id(1)
    blk = MOBA_BLOCK
    hd = HEAD_DIM
    aug = 2 * hd
    vr = MOBA_VROWS
    heads = range(A_HEADS)
    hsl = [slice(hh * hd, (hh + 1) * hd) for hh in heads]

    @pl.when(i == 0)
    def _():
        ones = jnp.ones((vr - hd, blk), BF16)
        blk_lane = lax.broadcasted_iota(I32, (blk, hd), 1)
        for n in range(nblk):
            rows = slice(n * blk, (n + 1) * blk)
            kmean_scr[n:n + 1, :] = jnp.mean(k_ref[rows, :].astype(F32), axis=0, keepdims=True)
            onehot = jnp.where(blk_lane == n, 1.0, 0.0).astype(BF16)
            for hh in heads:
                kaug_scr[rows, hh * aug:hh * aug + hd] = k_ref[rows, hsl[hh]]
                kaug_scr[rows, hh * aug + hd:(hh + 1) * aug] = onehot
                vaug_scr[n, hh * vr:hh * vr + hd, :] = vt_ref[n, hsl[hh], :]
                vaug_scr[n, hh * vr + hd:(hh + 1) * vr, :] = ones

    key = lax.broadcasted_iota(I32, (blk, blk), 0)
    qry = lax.broadcasted_iota(I32, (blk, blk), 1)
    causal = key <= qry
    blk_id = lax.broadcasted_iota(I32, (nblk, blk), 0)
    own0 = pl.multiple_of(i * blk, blk)

    init = []
    for hh in heads:
        qt = qt_ref[hsl[hh], :]
        km_hi, km_lo = _split_bf16(kmean_scr[:, hsl[hh]])
        g = _dot(km_hi, qt) + _dot(km_lo, qt)

        cnt = jnp.zeros((nblk, blk), I32)
        for m in range(nblk):
            gm = g[m:m + 1, :]
            beats = (gm > g) | ((gm == g) & (m < blk_id))
            cnt = cnt + jnp.where(beats, (m < i).astype(I32), 0)
        sel = (cnt < MOBA_TOPK) & (blk_id < i)
        qaug_scr[hh * aug:hh * aug + hd, :] = qt
        qaug_scr[hh * aug + hd:hh * aug + hd + nblk, :] = jnp.where(sel, 0.0, NEG).astype(BF16)
        qaug_scr[hh * aug + hd + nblk:(hh + 1) * aug, :] = jnp.zeros((hd - nblk, blk), BF16)

        s = _dot(k_ref[pl.ds(own0, blk), hsl[hh]], qt)
        s = jnp.where(causal, s, NEG)
        m0 = jnp.max(s, axis=0, keepdims=True)
        p = jnp.exp2(s - m0)
        init.append((m0, _dot(vaug_scr[i, hh * vr:(hh + 1) * vr, :], p.astype(BF16))))

    def step(blocks, carry):
        rows = [pl.ds(pl.multiple_of(j * blk, blk), blk) for j in blocks]
        ss = [[_dot(kaug_scr[r, hh * aug:(hh + 1) * aug], qaug_scr[hh * aug:(hh + 1) * aug, :])
               for r in rows] for hh in heads]
        ms = []
        for hh in heads:
            m_new = carry[hh][0]
            for s in ss[hh]:
                m_new = jnp.maximum(m_new, jnp.max(s, axis=0, keepdims=True))
            ms.append(m_new)
        ps = [[jnp.exp2(s - ms[hh]).astype(BF16) for s in ss[hh]] for hh in heads]
        pvs = []
        for hh in heads:
            pv = None
            for j, p in zip(blocks, ps[hh]):
                d = _dot(vaug_scr[j, hh * vr:(hh + 1) * vr, :], p)
                pv = d if pv is None else pv + d
            pvs.append(pv)
        return tuple((ms[hh], jnp.exp2(carry[hh][0] - ms[hh]) * carry[hh][1] + pvs[hh])
                     for hh in heads)

    fin = lax.fori_loop(0, i // 2, lambda jp, c: step([2 * jp, 2 * jp + 1], c), tuple(init))
    fin = lax.fori_loop(0, i % 2, lambda _, c: step([i - 1], c), fin)
    out_t = jnp.concatenate([acc[:hd, :] / acc[hd:hd + 1, :] for _, acc in fin], axis=0)
    o_ref[...] = out_t.T.astype(o_ref.dtype)


def _moba(p, batch, seq):
    nblk = seq // MOBA_BLOCK
    assert nblk <= HEAD_DIM
    qt = _blocks_t(p, COL_QA, A_W, batch, seq, MOBA_BLOCK)
    vt = _blocks_t(p, COL_VA, A_W, batch, seq, MOBA_BLOCK)
    kern = functools.partial(_moba_kernel, nblk=nblk)
    return pl.pallas_call(
        kern,
        grid=(batch, nblk),
        in_specs=[
            pl.BlockSpec((None, None, A_W, MOBA_BLOCK), lambda b, i: (b, i, 0, 0)),
            pl.BlockSpec((seq, A_W), lambda b, i: (b, COL_KA // A_W)),
            pl.BlockSpec((None, nblk, A_W, MOBA_BLOCK), lambda b, i: (b, 0, 0, 0)),
        ],
        out_specs=pl.BlockSpec((MOBA_BLOCK, A_W), lambda b, i: (b * nblk + i, 0)),
        out_shape=jax.ShapeDtypeStruct((batch * seq, A_W), BF16),
        scratch_shapes=[
            pltpu.VMEM((nblk, A_W), F32),
            pltpu.VMEM((seq, 2 * A_W), BF16),
            pltpu.VMEM((nblk, A_HEADS * MOBA_VROWS, MOBA_BLOCK), BF16),
            pltpu.VMEM((2 * A_W, MOBA_BLOCK), BF16),
        ],
        compiler_params=_params("parallel", "arbitrary"),
        name="moba",
    )(qt, p, vt)


def _swa_kernel(sinks_ref, q_ref, kv_ref, pkv_ref, o_ref, kv_scr, *, tq):
    i = pl.program_id(1)
    w = WINDOW
    kv_scr[0:w, :] = pkv_ref[...]
    kv_scr[w:, :] = kv_ref[...]

    row = lax.broadcasted_iota(I32, (w, 2 * w), 0)
    col = lax.broadcasted_iota(I32, (w, 2 * w), 1)
    rel = row + w - col
    band = (rel >= 0) & (rel < w)
    group = B_HEADS // B_KV_HEADS

    def body(n, carry):
        base = pl.multiple_of(n * w, w)
        kvt = kv_scr[pl.ds(base, 2 * w), :]
        qt = q_ref[pl.ds(base, w), :]
        col_min = jnp.where(jnp.logical_and(i == 0, n == 0), w, 0)
        mask = band & (col >= col_min)
        outs = []
        for kh in range(B_KV_HEADS):
            kk = kvt[:, kh * HEAD_DIM:(kh + 1) * HEAD_DIM]
            vv = kvt[:, B_KVW + kh * HEAD_DIM:B_KVW + (kh + 1) * HEAD_DIM]
            for gi in range(group):
                hd = kh * group + gi
                qh = qt[:, hd * HEAD_DIM:(hd + 1) * HEAD_DIM]
                s = jnp.where(mask, _dot_t(qh, kk), NEG)
                sink = sinks_ref[hd]
                m = jnp.maximum(jnp.max(s, axis=-1, keepdims=True), sink)
                p = jnp.exp(s - m)
                l = jnp.sum(p, axis=-1, keepdims=True) + jnp.exp(sink - m)
                outs.append(_dot(p.astype(BF16), vv) / l)
        o_ref[pl.ds(base, w), :] = jnp.concatenate(outs, axis=1).astype(o_ref.dtype)
        return carry

    lax.fori_loop(0, tq // w, body, 0)


def _swa(p, sinks, batch, seq, *, tq=512):
    nq = seq // tq
    sub = tq // WINDOW
    kern = functools.partial(_swa_kernel, tq=tq)
    return pl.pallas_call(
        kern,
        grid=(batch, nq),
        in_specs=[
            pl.BlockSpec(memory_space=pltpu.SMEM),
            pl.BlockSpec((tq, B_QW), lambda b, i: (b * nq + i, COL_QB // B_QW)),
            pl.BlockSpec((tq, 2 * B_KVW), lambda b, i: (b * nq + i, COL_KB // (2 * B_KVW))),
            pl.BlockSpec((WINDOW, 2 * B_KVW),
                         lambda b, i: (jnp.maximum((b * nq + i) * sub - 1, 0),
                                       COL_KB // (2 * B_KVW))),
        ],
        out_specs=pl.BlockSpec((tq, B_QW), lambda b, i: (b * nq + i, 0)),
        out_shape=jax.ShapeDtypeStruct((batch * seq, B_QW), BF16),
        scratch_shapes=[pltpu.VMEM((tq + WINDOW, 2 * B_KVW), BF16)],
        compiler_params=_params("parallel", "parallel"),
        name="swa",
    )(sinks, p, p, p)


def _stick_kernel(qt_ref, k_ref, vt_ref, o_ref):
    i = pl.program_id(1)
    blk = SB_BLOCK
    hd = HEAD_DIM
    heads = range(C_HEADS)
    hsl = [slice(hh * hd, (hh + 1) * hd) for hh in heads]
    key = lax.broadcasted_iota(I32, (blk, blk), 0)
    qry = lax.broadcasted_iota(I32, (blk, blk), 1)
    causal = key < qry
    tri = jnp.where(qry > key, 1.0, 0.0).astype(BF16)
    own0 = pl.multiple_of(i * blk, blk)

    def logs(z):
        t = jnp.log2(1.0 + jnp.exp2(-jnp.abs(z)))
        log_beta = jnp.minimum(z, 0.0) - t
        return log_beta, log_beta - z

    def suffix(lk):
        after = _dot(tri, lk.astype(BF16))
        total = after[0:1, :] + lk[0:1, :]
        return after, total

    qts = [qt_ref[hsl[hh], :] for hh in heads]
    init = []
    for hh in heads:
        z = _dot(k_ref[pl.ds(own0, blk), hsl[hh]], qts[hh])
        log_beta, log_keep = logs(z)
        after, carry0 = suffix(jnp.where(causal, log_keep, 0.0))
        wgt = jnp.where(causal, jnp.exp2(log_beta + after), 0.0)
        init.append((carry0, _dot(vt_ref[i, hsl[hh], :], wgt.astype(BF16))))

    def step(blocks, state):
        rows = [pl.ds(j * blk if isinstance(j, int) else pl.multiple_of(j * blk, blk), blk)
                for j in blocks]
        pairs = [(hh, b) for hh in heads for b in range(len(blocks))]
        zs = {(hh, b): _dot(k_ref[rows[b], hsl[hh]], qts[hh]) for hh, b in pairs}
        lgs = {k: logs(zs[k]) for k in pairs}
        sfx = {k: suffix(lgs[k][1]) for k in pairs}
        ws = {k: jnp.exp2(lgs[k][0] + sfx[k][0]).astype(BF16) for k in pairs}
        pvs = {(hh, b): _dot(vt_ref[blocks[b], hsl[hh], :], ws[(hh, b)]) for hh, b in pairs}
        out = []
        for hh in heads:
            carry, acc = state[hh]
            for b in range(len(blocks)):
                acc = acc + pvs[(hh, b)] * jnp.exp2(carry)
                carry = carry + sfx[(hh, b)][1]
            out.append((carry, acc))
        return tuple(out)

    fin = lax.fori_loop(0, i // 2, lambda jp, s: step([i - 1 - 2 * jp, i - 2 - 2 * jp], s),
                        tuple(init))
    fin = lax.fori_loop(0, i % 2, lambda _, s: step([0], s), fin)
    out_t = jnp.concatenate([acc for _, acc in fin], axis=0)
    o_ref[...] = out_t.T.astype(o_ref.dtype)


def _blocks_t(p, col, width, batch, seq, blk):
    t = p[:, col:col + width].reshape(batch, seq // blk, blk, width)
    return jnp.swapaxes(t, 2, 3)


def _stick(p, batch, seq):
    nblk = seq // SB_BLOCK
    qt = _blocks_t(p, COL_QC, C_W, batch, seq, SB_BLOCK)
    vt = _blocks_t(p, COL_VC, C_W, batch, seq, SB_BLOCK)
    return pl.pallas_call(
        _stick_kernel,
        grid=(batch, nblk),
        in_specs=[
            pl.BlockSpec((None, None, C_W, SB_BLOCK), lambda b, i: (b, i, 0, 0)),
            pl.BlockSpec((seq, C_W), lambda b, i: (b, COL_KC // C_W)),
            pl.BlockSpec((None, nblk, C_W, SB_BLOCK), lambda b, i: (b, 0, 0, 0)),
        ],
        out_specs=pl.BlockSpec((SB_BLOCK, C_W), lambda b, i: (b * nblk + i, 0)),
        out_shape=jax.ShapeDtypeStruct((batch * seq, C_W), BF16),
        compiler_params=_params("parallel", "parallel"),
        name="stick",
    )(qt, p, vt)


def _sigmoid(x):
    return 1.0 / (1.0 + jnp.exp(-x))


def _merge_kernel(x_ref, oa_ref, ob_ref, oc_ref, ga_ref, gb_ref, gc_ref,
                  wa_ref, wb_ref, wc_ref, wo_ref, o_ref):
    mixed = _sigmoid(ga_ref[...].astype(F32)) * _dot(oa_ref[...], wa_ref[...])
    mixed = mixed + _sigmoid(gb_ref[...].astype(F32)) * _dot(ob_ref[...], wb_ref[...])
    mixed = mixed + _sigmoid(gc_ref[...].astype(F32)) * _dot(oc_ref[...], wc_ref[...])
    o_ref[...] = x_ref[...] + _dot(mixed.astype(BF16), wo_ref[...])


def _merge(x, oa, ob, oc, p, wa, wb, wc, wo, *, tm=512):
    t, d = x.shape
    full = lambda a: pl.BlockSpec(a.shape, lambda i: (0, 0))
    return pl.pallas_call(
        _merge_kernel,
        grid=(t // tm,),
        in_specs=[
            pl.BlockSpec((tm, d), lambda i: (i, 0)),
            pl.BlockSpec((tm, A_W), lambda i: (i, 0)),
            pl.BlockSpec((tm, B_QW), lambda i: (i, 0)),
            pl.BlockSpec((tm, C_W), lambda i: (i, 0)),
            pl.BlockSpec((tm, d), lambda i: (i, COL_GA // D_MODEL)),
            pl.BlockSpec((tm, d), lambda i: (i, COL_GB // D_MODEL)),
            pl.BlockSpec((tm, d), lambda i: (i, COL_GC // D_MODEL)),
            full(wa), full(wb), full(wc), full(wo),
        ],
        out_specs=pl.BlockSpec((tm, d), lambda i: (i, 0)),
        out_shape=jax.ShapeDtypeStruct((t, d), F32),
        compiler_params=_params("parallel"),
        name="merge",
    )(x, oa, ob, oc, p, p, p, wa, wb, wc, wo)


def _cross_kernel(x_ref, g_ref, wq_ref, kv_ref, wo_ref, o_ref):
    x = x_ref[...]
    h = _rms(x, g_ref[...]).astype(BF16)
    q = (_dot(h, wq_ref[...]) * (X_HEAD_DIM ** -0.5)).astype(BF16)
    outs = []
    for hd in range(X_HEADS):
        hs = slice(hd * X_HEAD_DIM, (hd + 1) * X_HEAD_DIM)
        s = _dot_t(q[:, hs], kv_ref[:, hs])
        m = jnp.max(s, axis=-1, keepdims=True)
        p = jnp.exp(s - m)
        l = jnp.sum(p, axis=-1, keepdims=True)
        vs = slice(X_W + hd * X_HEAD_DIM, X_W + (hd + 1) * X_HEAD_DIM)
        outs.append(_dot(p.astype(BF16), kv_ref[:, vs]) / l)
    att = jnp.concatenate(outs, axis=1).astype(BF16)
    o_ref[...] = x + _dot(att, wo_ref[...])


def _cross(x, gain, wq, kv, wo, seq, *, tm=512):
    t, d = x.shape
    per_batch = seq // tm
    full = lambda a: pl.BlockSpec(a.shape, lambda i: (0, 0))
    return pl.pallas_call(
        _cross_kernel,
        grid=(t // tm,),
        in_specs=[
            pl.BlockSpec((tm, d), lambda i: (i, 0)),
            full(gain), full(wq),
            pl.BlockSpec((MEM_LEN, 2 * X_W), lambda i: (i // per_batch, 0)),
            full(wo),
        ],
        out_specs=pl.BlockSpec((tm, d), lambda i: (i, 0)),
        out_shape=jax.ShapeDtypeStruct((t, d), F32),
        compiler_params=_params("parallel"),
        name="cross",
    )(x, gain, wq, kv, wo)


def _silu(x):
    return x * _sigmoid(x)


def _ffn_kernel(x_ref, g_ref, wg_ref, wu_ref, wd_ref, o_ref, h_scr, acc_scr):
    f = pl.program_id(1)

    @pl.when(f == 0)
    def _():
        h_scr[...] = _rms(x_ref[...], g_ref[...]).astype(BF16)
        acc_scr[...] = x_ref[...]

    h = h_scr[...]
    a = _silu(_dot(h, wg_ref[...])) * _dot(h, wu_ref[...])
    acc_scr[...] += _dot(a.astype(BF16), wd_ref[...])

    @pl.when(f == pl.num_programs(1) - 1)
    def _():
        o_ref[...] = acc_scr[...]


def _ffn(x, gain, wg, wu, wd, *, tm=1024, tf=256):
    t, d = x.shape
    ff = wd.shape[0]
    return pl.pallas_call(
        _ffn_kernel,
        grid=(t // tm, ff // tf),
        in_specs=[
            pl.BlockSpec((tm, d), lambda i, f: (i, 0)),
            pl.BlockSpec((1, d), lambda i, f: (0, 0)),
            pl.BlockSpec((d, tf), lambda i, f: (0, f)),
            pl.BlockSpec((d, tf), lambda i, f: (0, f)),
            pl.BlockSpec((tf, d), lambda i, f: (f, 0)),
        ],
        out_specs=pl.BlockSpec((tm, d), lambda i, f: (i, 0)),
        out_shape=jax.ShapeDtypeStruct((t, d), F32),
        scratch_shapes=[pltpu.VMEM((tm, d), BF16), pltpu.VMEM((tm, d), F32)],
        compiler_params=_params("parallel", "arbitrary"),
        name="ffn",
    )(x, gain, wg, wu, wd)


def _router_kernel(x_ref, g_ref, wr_ref, o_ref):
    h = _rms(x_ref[...], g_ref[...])
    h_hi, h_lo = _split_bf16(h)
    w_hi, w_lo = _split_bf16(wr_ref[...])
    logits = _dot(h_hi, w_hi) + (_dot(h_hi, w_lo) + _dot(h_lo, w_hi))
    lane = lax.broadcasted_iota(I32, logits.shape, 1)
    logits = jnp.where(lane < N_EXPERTS, logits, NEG)
    v1 = jnp.max(logits, axis=-1, keepdims=True)
    i1 = jnp.min(jnp.where(logits == v1, lane, LANES), axis=-1, keepdims=True)
    rest = jnp.where(lane == i1, NEG, logits)
    v2 = jnp.max(rest, axis=-1, keepdims=True)
    i2 = jnp.min(jnp.where(rest == v2, lane, LANES), axis=-1, keepdims=True)
    e = jnp.exp(v2 - v1)
    w1 = 1.0 / (1.0 + e)
    w2 = e / (1.0 + e)
    out = jnp.where(lane == 0, i1.astype(F32), 0.0)
    out = jnp.where(lane == 1, i2.astype(F32), out)
    out = jnp.where(lane == 2, w1, out)
    out = jnp.where(lane == 3, w2, out)
    o_ref[...] = out


def _router(x, gain, wr, *, tm=1024):
    t, d = x.shape
    return pl.pallas_call(
        _router_kernel,
        grid=(t // tm,),
        in_specs=[
            pl.BlockSpec((tm, d), lambda i: (i, 0)),
            pl.BlockSpec((1, d), lambda i: (0, 0)),
            pl.BlockSpec((d, LANES), lambda i: (0, 0)),
        ],
        out_specs=pl.BlockSpec((tm, LANES), lambda i: (i, 0)),
        out_shape=jax.ShapeDtypeStruct((t, LANES), F32),
        compiler_params=_params("parallel"),
        name="router",
    )(x, gain, wr)


def _moe_kernel(tok_ref, dst_ref, texp_ref, tval_ref, x_hbm, g_ref, wg_ref, wu_ref, wd_ref,
                y_hbm, xbuf, obuf, h_scr, acc_scr, gsem, ssem, *, tm, nf):
    m = pl.program_id(0)
    f = pl.program_id(1)
    last_phase = pl.num_programs(0) - 1
    slot = m % 2
    other = 1 - slot
    valid = tval_ref[m] > 0
    per_step = (tm // nf) // 8 * 8
    in_steps = per_step * nf

    def row_tile(r):
        return pl.ds(pl.multiple_of(r * SUBLANES, SUBLANES), SUBLANES)

    def gather_row(tile, r, s):
        tok = pl.multiple_of(tok_ref[tile * tm + r], SUBLANES)
        return pltpu.make_async_copy(x_hbm.at[pl.ds(tok, SUBLANES), :], xbuf.at[s, row_tile(r), :],
                                     gsem.at[s])

    def scatter_row(phase, r, s):
        dst = pl.multiple_of(dst_ref[phase * tm + r], SUBLANES)
        return pltpu.make_async_copy(obuf.at[s, row_tile(r), :], y_hbm.at[pl.ds(dst, SUBLANES), :],
                                     ssem.at[s])

    def chunk(c):
        return pl.ds(c, tm, stride=SUBLANES)

    wait_group = 16

    def gather_wait(tile, s):
        def body(it, c):
            for u in range(wait_group):
                gather_row(tile, it * wait_group + u, s).wait()
            return c
        lax.fori_loop(0, tm // wait_group, body, 0)

    def scatter_wait(phase, s):
        def body(it, c):
            for u in range(wait_group):
                scatter_row(phase, it * wait_group + u, s).wait()
            return c
        lax.fori_loop(0, tm // wait_group, body, 0)

    @pl.when(f == 0)
    def _():
        @pl.when(m == 0)
        def _():
            obuf[...] = jnp.zeros_like(obuf)

            def first(r, c):
                gather_row(0, r, 0).start()
                return c
            lax.fori_loop(0, tm, first, 0, unroll=8)

        @pl.when(m >= 1)
        def _():
            scatter_wait(m - 1, slot)
        gather_wait(m, slot)
        for r in range(in_steps, tm):
            gather_row(m + 1, r, other).start()
            scatter_row(m, r, other).start()
        n_chunks = h_scr.shape[1] // LANES
        xs = [xbuf[slot, chunk(c), :] for c in range(n_chunks)]
        ssq = xs[0] * xs[0]
        for xc in xs[1:]:
            ssq = ssq + xc * xc
        inv = lax.rsqrt(jnp.sum(ssq, axis=-1, keepdims=True) / h_scr.shape[1] + EPS)
        for c in range(n_chunks):
            lanes = slice(c * LANES, (c + 1) * LANES)
            h_scr[:, lanes] = (xs[c] * inv * g_ref[:, lanes]).astype(BF16)
        acc_scr[...] = jnp.zeros_like(acc_scr)

    def move_rows():
        r0 = pl.multiple_of(f * per_step, 8)
        for rr in range(per_step):
            r = r0 + rr
            gather_row(m + 1, r, other).start()
            scatter_row(m, r, other).start()

    @pl.when(valid)
    def _():
        move_rows()
        h = h_scr[...]
        a = _silu(_dot(h, wg_ref[...])) * _dot(h, wu_ref[...])
        acc_scr[...] += _dot(a.astype(BF16), wd_ref[...])

    @pl.when(jnp.logical_not(valid))
    def _():
        move_rows()

    @pl.when(f == nf - 1)
    def _():
        for c in range(acc_scr.shape[1] // LANES):
            obuf[slot, chunk(c), :] = jnp.where(valid, acc_scr[:, c * LANES:(c + 1) * LANES], 0.0)

        @pl.when(m == last_phase)
        def _():
            scatter_wait(m, other)
            gather_wait(m + 1, other)


def _moe_experts(x_rows, gain, wg, wu, wd, tok, dst, tile_expert, tile_valid, n_rows, *, tm, tf):
    d = gain.shape[1]
    assert d == SUBLANES * LANES
    nf = wg.shape[2] // tf
    n_phases = tile_expert.shape[0]
    kern = functools.partial(_moe_kernel, tm=tm, nf=nf)

    def fsel(m, f, tval):
        return jnp.where(tval[m] > 0, f, nf - 1)

    grid_spec = pltpu.PrefetchScalarGridSpec(
        num_scalar_prefetch=4,
        grid=(n_phases, nf),
        in_specs=[
            pl.BlockSpec(memory_space=pl.ANY),
            pl.BlockSpec((1, d), lambda m, f, tok, dst, texp, tval: (0, 0)),
            pl.BlockSpec((None, d, tf),
                         lambda m, f, tok, dst, texp, tval: (texp[m], 0, fsel(m, f, tval))),
            pl.BlockSpec((None, d, tf),
                         lambda m, f, tok, dst, texp, tval: (texp[m], 0, fsel(m, f, tval))),
            pl.BlockSpec((None, tf, d),
                         lambda m, f, tok, dst, texp, tval: (texp[m], fsel(m, f, tval), 0)),
        ],
        out_specs=pl.BlockSpec(memory_space=pl.ANY),
        scratch_shapes=[
            pltpu.VMEM((2, tm * SUBLANES, LANES), F32),
            pltpu.VMEM((2, tm * SUBLANES, LANES), F32),
            pltpu.VMEM((tm, d), BF16),
            pltpu.VMEM((tm, d), F32),
            pltpu.SemaphoreType.DMA((2,)),
            pltpu.SemaphoreType.DMA((2,)),
        ],
    )
    return pl.pallas_call(
        kern,
        grid_spec=grid_spec,
        out_shape=jax.ShapeDtypeStruct((n_rows * SUBLANES, LANES), F32),
        compiler_params=_params("arbitrary", "arbitrary"),
        name="moe_experts",
    )(tok, dst, tile_expert, tile_valid, x_rows, gain, wg, wu, wd)


def _combine_kernel(x_ref, r_ref, y0_ref, y1_ref, g_ref, o_ref, *, out_norm):
    tm, d = x_ref.shape
    route = r_ref[...]
    w0 = route[:, TOP_K:TOP_K + 1]
    w1 = route[:, TOP_K + 1:TOP_K + 2]
    for c in range(d // LANES):
        lanes = slice(c * LANES, (c + 1) * LANES)
        rows = pl.ds(c, tm, stride=SUBLANES)
        o_ref[:, lanes] = x_ref[:, lanes] + w0 * y0_ref[rows, :] + w1 * y1_ref[rows, :]
    if out_norm:
        o_ref[...] = _rms(o_ref[...], g_ref[...])


def _moe_combine(x, route, y_rows, out_gain, *, tm=512):
    t, d = x.shape
    assert TOP_K == 2
    out_norm = out_gain is not None
    gain = out_gain if out_norm else jnp.ones((1, d), F32)
    return pl.pallas_call(
        functools.partial(_combine_kernel, out_norm=out_norm),
        grid=(t // tm,),
        in_specs=[
            pl.BlockSpec((tm, d), lambda i: (i, 0)),
            pl.BlockSpec((tm, LANES), lambda i: (i, 0)),
            pl.BlockSpec((tm * SUBLANES, LANES), lambda i: (i, 0)),
            pl.BlockSpec((tm * SUBLANES, LANES), lambda i: (t // tm + i, 0)),
            pl.BlockSpec((1, d), lambda i: (0, 0)),
        ],
        out_specs=pl.BlockSpec((tm, d), lambda i: (i, 0)),
        out_shape=jax.ShapeDtypeStruct((t, d), F32),
        compiler_params=_params("parallel"),
        name="moe_combine",
    )(x, route, y_rows, y_rows, gain)


def _moe(x, gain, w_router, wg, wu, wd, out_gain=None, *, tm=1024, tf=MOE_TF):
    t, d = x.shape
    wr = jnp.zeros((d, LANES), F32).at[:, :N_EXPERTS].set(w_router)
    route = _router(x, gain, wr)
    tok, dst, tile_expert, tile_valid, n_rows = _route_metadata(route[:, :TOP_K].astype(I32), tm)
    x_rows = x.reshape(t * SUBLANES, LANES)
    y_rows = _moe_experts(x_rows, gain, wg, wu, wd, tok * SUBLANES, dst * SUBLANES,
                          tile_expert, tile_valid, n_rows, tm=tm, tf=tf)
    return _moe_combine(x, route, y_rows, out_gain)


def _route_metadata(experts, tm):
    t = experts.shape[0]
    n_slots = TOP_K * t
    eflat = experts.T.reshape(-1)
    onehot = (eflat[:, None] == jnp.arange(N_EXPERTS, dtype=I32)[None, :]).astype(I32)
    csum = jnp.cumsum(onehot, axis=0)
    rank = jnp.sum((csum - 1) * onehot, axis=1)
    counts = csum[-1]
    tiles_per = (counts + tm - 1) // tm
    tile_end = jnp.cumsum(tiles_per)
    start = (tile_end - tiles_per) * tm
    pos = (jnp.sum(start[None, :] * onehot, axis=1) + rank).astype(I32)
    n_tiles = n_slots // tm + N_EXPERTS
    n_rows_pad = n_tiles * tm
    slot_of_row = jnp.full((n_rows_pad,), -1, I32).at[pos].set(jnp.arange(n_slots, dtype=I32))
    real = slot_of_row >= 0
    tok = jnp.concatenate([jnp.where(real, slot_of_row % t, 0), jnp.zeros((2 * tm,), I32)])
    real_ext = jnp.concatenate([jnp.zeros((tm,), bool), real])
    dump = n_slots + jnp.cumsum(jnp.logical_not(real_ext).astype(I32)) - 1
    dst = jnp.where(real_ext, jnp.concatenate([jnp.zeros((tm,), I32), slot_of_row]), dump)
    tile_id = jnp.arange(n_tiles + 1, dtype=I32)
    tile_expert = jnp.sum((tile_id[:, None] >= tile_end[None, :]).astype(I32), axis=1)
    tile_valid = (tile_id < tile_end[-1]).astype(I32)
    last_expert = jnp.max(jnp.where(counts > 0, jnp.arange(N_EXPERTS, dtype=I32), 0))
    tile_expert = jnp.where(tile_valid > 0, tile_expert, last_expert).astype(I32)
    return tok, dst.astype(I32), tile_expert, tile_valid, n_rows_pad + tm


def _final_norm_kernel(x_ref, g_ref, o_ref):
    o_ref[...] = _rms(x_ref[...], g_ref[...])


def _final_norm(x, gain, *, tm=1024):
    t, d = x.shape
    return pl.pallas_call(
        _final_norm_kernel,
        grid=(t // tm,),
        in_specs=[pl.BlockSpec((tm, d), lambda i: (i, 0)), pl.BlockSpec((1, d), lambda i: (0, 0))],
        out_specs=pl.BlockSpec((tm, d), lambda i: (i, 0)),
        out_shape=jax.ShapeDtypeStruct((t, d), F32),
        compiler_params=_params("parallel"),
        name="final_norm",
    )(x, gain)


def _rope_tables(seq):
    half = HEAD_DIM // 2
    inv_freq = ROPE_THETA ** (-jnp.arange(half, dtype=F32) / half)
    ang = jnp.arange(seq, dtype=F32)[:, None] * inv_freq[None, :]
    cos, sin = jnp.cos(ang), jnp.sin(ang)
    reps = LANES // HEAD_DIM
    cos_t = jnp.tile(jnp.concatenate([cos, cos], axis=1), (1, reps))
    sin_t = jnp.tile(jnp.concatenate([-sin, sin], axis=1), (1, reps))
    return cos_t, sin_t


def _in_proj_weight(w_in):
    o = 0
    offs = {}
    for name, width in (("qa", A_W), ("ka", A_W), ("va", A_W), ("qb", B_QW), ("kb", B_KVW),
                        ("vb", B_KVW), ("qc", C_W), ("kc", C_W), ("vc", C_W), ("g", 3 * D_MODEL)):
        offs[name] = (o, o + width)
        o += width
    scale = HEAD_DIM ** -0.5
    log2e = 1.4426950408889634
    q_scale = {"qa": scale * log2e, "qb": scale, "qc": scale * log2e}
    parts = []
    for name in ("g", "qa", "ka", "qb", "kb", "vb", "va", "qc", "kc", "vc"):
        lo, hi = offs[name]
        blk = w_in[:, lo:hi]
        if name in q_scale:
            blk = blk * q_scale[name]
        parts.append(blk)
    return jnp.concatenate(parts, axis=1).astype(BF16)


def _rope_group_map():
    groups = {}
    for t in range(IN_W // PROJ_TN):
        lo, hi = t * PROJ_TN, (t + 1) * PROJ_TN
        n = (min(hi, ROPE_HI) - max(lo, ROPE_LO)) // LANES
        if n > 0:
            assert max(lo, ROPE_LO) == lo
            groups[t] = n
    return groups


def kernel(x, mem, norm_mix, w_in, w_proj_a, w_proj_b, w_proj_c, w_mix_out, sinks, norm_cross,
           norm_mem, w_xq, w_xkv, w_xo, norm_ffn, ffn_gate, ffn_up, ffn_down, moe_router,
           moe_gate, moe_up, moe_down, final_norm):
    batch, seq, d = x.shape
    depth = norm_mix.shape[0]
    assert d == D_MODEL and seq % 1024 == 0 and mem.shape[1] == MEM_LEN
    t = batch * seq
    xf = x.reshape(t, d)
    memf = mem.reshape(batch * MEM_LEN, d)
    cos_t, sin_t = _rope_tables(seq)
    rope_groups = _rope_group_map()
    ones_tab = jnp.ones((MEM_LEN, LANES), F32)

    for l in range(depth):
        gain = lambda g: g[l].reshape(1, d)
        p = _norm_proj(xf, gain(norm_mix), _in_proj_weight(w_in[l]), cos_t, sin_t, seq,
                       tm=1024, tn=PROJ_TN, rope_groups=rope_groups)
        oa = _moba(p, batch, seq)
        ob = _swa(p, sinks[l], batch, seq)
        oc = _stick(p, batch, seq)
        xf = _merge(xf, oa, ob, oc, p, w_proj_a[l].astype(BF16), w_proj_b[l].astype(BF16),
                    w_proj_c[l].astype(BF16), w_mix_out[l].astype(BF16))
        kv = _norm_proj(memf, gain(norm_mem), w_xkv[l].astype(BF16), ones_tab, ones_tab, MEM_LEN,
                        tm=MEM_LEN, tn=2 * X_W, rope_groups={})
        xf = _cross(xf, gain(norm_cross), w_xq[l].astype(BF16), kv, w_xo[l].astype(BF16), seq)
        if l % 2 == 0:
            i = l // 2
            xf = _ffn(xf, gain(norm_ffn), ffn_gate[i].astype(BF16), ffn_up[i].astype(BF16),
                      ffn_down[i].astype(BF16), tf=FFN_TF)
        else:
            i = l // 2
            out_gain = final_norm.reshape(1, d) if l == depth - 1 else None
            xf = _moe(xf, gain(norm_ffn), moe_router[i], moe_gate[i].astype(BF16),
                      moe_up[i].astype(BF16), moe_down[i].astype(BF16), out_gain)
    if depth % 2 == 1:
        xf = _final_norm(xf, final_norm.reshape(1, d))
    return xf.reshape(batch, seq, d)
```

```python
import functools

import jax
import jax.numpy as jnp
from jax import lax
from jax.experimental import pallas as pl
from jax.experimental.pallas import tpu as pltpu

F32 = jnp.float32
BF16 = jnp.bfloat16
I32 = jnp.int32

D_MODEL = 1024
HEAD_DIM = 64
A_HEADS = 4
MOBA_BLOCK = 256
MOBA_TOPK = 3
B_HEADS = 8
B_KV_HEADS = 2
WINDOW = 128
C_HEADS = 4
SB_BLOCK = 256
MEM_LEN = 256
X_HEADS = 4
X_HEAD_DIM = 128
N_EXPERTS = 8
TOP_K = 2
ROPE_THETA = 10000.0
EPS = 1e-6

A_W = A_HEADS * HEAD_DIM
B_QW = B_HEADS * HEAD_DIM
B_KVW = B_KV_HEADS * HEAD_DIM
C_W = C_HEADS * HEAD_DIM
X_W = X_HEADS * X_HEAD_DIM
QKV_W = 3 * A_W + B_QW + 2 * B_KVW + 3 * C_W
IN_W = QKV_W + 3 * D_MODEL

LANES = 128
SUBLANES = 8
NEG = -1e30

COL_GA, COL_GB, COL_GC = 0, D_MODEL, 2 * D_MODEL
COL_QA = 3 * D_MODEL
COL_KA = COL_QA + A_W
COL_QB = COL_KA + A_W
COL_KB = COL_QB + B_QW
COL_VB = COL_KB + B_KVW
COL_VA = COL_VB + B_KVW
COL_QC = COL_VA + A_W
COL_KC = COL_QC + C_W
COL_VC = COL_KC + C_W
ROPE_LO, ROPE_HI = COL_QA, COL_VB

PROJ_TN = 768
FFN_TF = 256
MOE_TF = 512
MOE_TM = 1072
VMEM_LIMIT = 48 * 1024 * 1024


def _params(*sem):
    return pltpu.CompilerParams(dimension_semantics=sem, vmem_limit_bytes=VMEM_LIMIT)


def _rms(x, g):
    ms = jnp.mean(x * x, axis=-1, keepdims=True)
    return x * lax.rsqrt(ms + EPS) * g


def _dot(a, b):
    return jnp.dot(a, b, preferred_element_type=F32)


def _dot_t(a, b):
    return lax.dot_general(a, b, (((1,), (1,)), ((), ())), preferred_element_type=F32)


def _split_bf16(v):
    hi = v.astype(BF16)
    lo = (v - hi.astype(F32)).astype(BF16)
    return hi, lo


def _norm_proj_kernel(x_ref, g_ref, w_ref, cos_ref, sin_ref, o_ref, h_scr, *, rope_groups):
    j = pl.program_id(1)

    @pl.when(j == 0)
    def _():
        h_scr[...] = _rms(x_ref[...], g_ref[...]).astype(BF16)

    acc = _dot(h_scr[...], w_ref[...])
    n_groups = acc.shape[1] // LANES

    def rope(y):
        lane = lax.broadcasted_iota(I32, y.shape, 1)
        first_half = (lane % HEAD_DIM) < (HEAD_DIM // 2)
        sw = jnp.where(first_half, pltpu.roll(y, LANES - HEAD_DIM // 2, 1),
                       pltpu.roll(y, HEAD_DIM // 2, 1))
        return y * cos_ref[...] + sw * sin_ref[...]

    roped = sorted(rope_groups)

    for t in roped:
        @pl.when(j == t)
        def _(t=t):
            for gi in range(n_groups):
                y = acc[:, gi * LANES:(gi + 1) * LANES]
                if gi < rope_groups[t]:
                    y = rope(y)
                o_ref[:, gi * LANES:(gi + 1) * LANES] = y.astype(o_ref.dtype)

    is_plain = j >= 0
    for t in roped:
        is_plain = jnp.logical_and(is_plain, j != t)

    @pl.when(is_plain)
    def _():
        o_ref[...] = acc.astype(o_ref.dtype)


def _norm_proj(x, gain, w, cos, sin, seq, *, tm, tn, rope_groups):
    t, d = x.shape
    n = w.shape[1]
    pos_blocks = seq // tm
    kern = functools.partial(_norm_proj_kernel, rope_groups=rope_groups)
    return pl.pallas_call(
        kern,
        grid=(t // tm, n // tn),
        in_specs=[
            pl.BlockSpec((tm, d), lambda i, j: (i, 0)),
            pl.BlockSpec((1, d), lambda i, j: (0, 0)),
            pl.BlockSpec((d, tn), lambda i, j: (0, j)),
            pl.BlockSpec((tm, LANES), lambda i, j: (i % pos_blocks, 0)),
            pl.BlockSpec((tm, LANES), lambda i, j: (i % pos_blocks, 0)),
        ],
        out_specs=pl.BlockSpec((tm, tn), lambda i, j: (i, j)),
        out_shape=jax.ShapeDtypeStruct((t, n), BF16),
        scratch_shapes=[pltpu.VMEM((tm, d), BF16)],
        compiler_params=_params("parallel", "arbitrary"),
        name="norm_proj",
    )(x, gain, w, cos, sin)


MOBA_VROWS = HEAD_DIM + 16


def _moba_kernel(qt_ref, k_ref, vt_ref, o_ref, kmean_scr, kaug_scr, vaug_scr, qaug_scr, *, nblk):
    i = pl.program_id(1)
    blk = MOBA_BLOCK
    hd = HEAD_DIM
    aug = 2 * hd
    vr = MOBA_VROWS
    heads = range(A_HEADS)
    hsl = [slice(hh * hd, (hh + 1) * hd) for hh in heads]

    @pl.when(i == 0)
    def _():
        ones = jnp.ones((vr - hd, blk), BF16)
        blk_lane = lax.broadcasted_iota(I32, (blk, hd), 1)
        for n in range(nblk):
            rows = slice(n * blk, (n + 1) * blk)
            kmean_scr[n:n + 1, :] = jnp.mean(k_ref[rows, :].astype(F32), axis=0, keepdims=True)
            onehot = jnp.where(blk_lane == n, 1.0, 0.0).astype(BF16)
            for hh in heads:
                kaug_scr[rows, hh * aug:hh * aug + hd] = k_ref[rows, hsl[hh]]
                kaug_scr[rows, hh * aug + hd:(hh + 1) * aug] = onehot
                vaug_scr[n, hh * vr:hh * vr + hd, :] = vt_ref[n, hsl[hh], :]
                vaug_scr[n, hh * vr + hd:(hh + 1) * vr, :] = ones

    key = lax.broadcasted_iota(I32, (blk, blk), 0)
    qry = lax.broadcasted_iota(I32, (blk, blk), 1)
    causal = key <= qry
    blk_id = lax.broadcasted_iota(I32, (nblk, blk), 0)
    own0 = pl.multiple_of(i * blk, blk)

    init = []
    for hh in heads:
        qt = qt_ref[hsl[hh], :]
        km_hi, km_lo = _split_bf16(kmean_scr[:, hsl[hh]])
        g = _dot(km_hi, qt) + _dot(km_lo, qt)

        cnt = jnp.zeros((nblk, blk), I32)
        for m in range(nblk):
            gm = g[m:m + 1, :]
            beats = (gm > g) | ((gm == g) & (m < blk_id))
            cnt = cnt + jnp.where(beats, (m < i).astype(I32), 0)
        sel = (cnt < MOBA_TOPK) & (blk_id < i)
        qaug_scr[hh * aug:hh * aug + hd, :] = qt
        qaug_scr[hh * aug + hd:hh * aug + hd + nblk, :] = jnp.where(sel, 0.0, NEG).astype(BF16)
        qaug_scr[hh * aug + hd + nblk:(hh + 1) * aug, :] = jnp.zeros((hd - nblk, blk), BF16)

        s = _dot(k_ref[pl.ds(own0, blk), hsl[hh]], qt)
        s = jnp.where(causal, s, NEG)
        m0 = jnp.max(s, axis=0, keepdims=True)
        p = jnp.exp2(s - m0)
        init.append((m0, _dot(vaug_scr[i, hh * vr:(hh + 1) * vr, :], p.astype(BF16))))

    def step(blocks, carry):
        rows = [pl.ds(pl.multiple_of(j * blk, blk), blk) for j in blocks]
        ss = [[_dot(kaug_scr[r, hh * aug:(hh + 1) * aug], qaug_scr[hh * aug:(hh + 1) * aug, :])
               for r in rows] for hh in heads]
        ms = []
        for hh in heads:
            m_new = carry[hh][0]
            for s in ss[hh]:
                m_new = jnp.maximum(m_new, jnp.max(s, axis=0, keepdims=True))
            ms.append(m_new)
        ps = [[jnp.exp2(s - ms[hh]).astype(BF16) for s in ss[hh]] for hh in heads]
        pvs = []
        for hh in heads:
            pv = None
            for j, p in zip(blocks, ps[hh]):
                d = _dot(vaug_scr[j, hh * vr:(hh + 1) * vr, :], p)
                pv = d if pv is None else pv + d
            pvs.append(pv)
        return tuple((ms[hh], jnp.exp2(carry[hh][0] - ms[hh]) * carry[hh][1] + pvs[hh])
                     for hh in heads)

    fin = lax.fori_loop(0, i // 2, lambda jp, c: step([2 * jp, 2 * jp + 1], c), tuple(init))
    fin = lax.fori_loop(0, i % 2, lambda _, c: step([i - 1], c), fin)
    out_t = jnp.concatenate([acc[:hd, :] / acc[hd:hd + 1, :] for _, acc in fin], axis=0)
    o_ref[...] = out_t.T.astype(o_ref.dtype)


def _moba(p, batch, seq):
    nblk = seq // MOBA_BLOCK
    assert nblk <= HEAD_DIM
    qt = _blocks_t(p, COL_QA, A_W, batch, seq, MOBA_BLOCK)
    vt = _blocks_t(p, COL_VA, A_W, batch, seq, MOBA_BLOCK)
    kern = functools.partial(_moba_kernel, nblk=nblk)
    return pl.pallas_call(
        kern,
        grid=(batch, nblk),
        in_specs=[
            pl.BlockSpec((None, None, A_W, MOBA_BLOCK), lambda b, i: (b, i, 0, 0)),
            pl.BlockSpec((seq, A_W), lambda b, i: (b, COL_KA // A_W)),
            pl.BlockSpec((None, nblk, A_W, MOBA_BLOCK), lambda b, i: (b, 0, 0, 0)),
        ],
        out_specs=pl.BlockSpec((MOBA_BLOCK, A_W), lambda b, i: (b * nblk + i, 0)),
        out_shape=jax.ShapeDtypeStruct((batch * seq, A_W), BF16),
        scratch_shapes=[
            pltpu.VMEM((nblk, A_W), F32),
            pltpu.VMEM((seq, 2 * A_W), BF16),
            pltpu.VMEM((nblk, A_HEADS * MOBA_VROWS, MOBA_BLOCK), BF16),
            pltpu.VMEM((2 * A_W, MOBA_BLOCK), BF16),
        ],
        compiler_params=_params("parallel", "arbitrary"),
        name="moba",
    )(qt, p, vt)


def _swa_kernel(sinks_ref, q_ref, kv_ref, pkv_ref, o_ref, kv_scr, *, tq):
    i = pl.program_id(1)
    w = WINDOW
    kv_scr[0:w, :] = pkv_ref[...]
    kv_scr[w:, :] = kv_ref[...]

    row = lax.broadcasted_iota(I32, (w, 2 * w), 0)
    col = lax.broadcasted_iota(I32, (w, 2 * w), 1)
    rel = row + w - col
    band = (rel >= 0) & (rel < w)
    group = B_HEADS // B_KV_HEADS

    def body(n, carry):
        base = pl.multiple_of(n * w, w)
        kvt = kv_scr[pl.ds(base, 2 * w), :]
        qt = q_ref[pl.ds(base, w), :]
        col_min = jnp.where(jnp.logical_and(i == 0, n == 0), w, 0)
        mask = band & (col >= col_min)
        outs = []
        for kh in range(B_KV_HEADS):
            kk = kvt[:, kh * HEAD_DIM:(kh + 1) * HEAD_DIM]
            vv = kvt[:, B_KVW + kh * HEAD_DIM:B_KVW + (kh + 1) * HEAD_DIM]
            for gi in range(group):
                hd = kh * group + gi
                qh = qt[:, hd * HEAD_DIM:(hd + 1) * HEAD_DIM]
                s = jnp.where(mask, _dot_t(qh, kk), NEG)
                sink = sinks_ref[hd]
                m = jnp.maximum(jnp.max(s, axis=-1, keepdims=True), sink)
                p = jnp.exp(s - m)
                l = jnp.sum(p, axis=-1, keepdims=True) + jnp.exp(sink - m)
                outs.append(_dot(p.astype(BF16), vv) / l)
        o_ref[pl.ds(base, w), :] = jnp.concatenate(outs, axis=1).astype(o_ref.dtype)
        return carry

    lax.fori_loop(0, tq // w, body, 0)


def _swa(p, sinks, batch, seq, *, tq=512):
    nq = seq // tq
    sub = tq // WINDOW
    kern = functools.partial(_swa_kernel, tq=tq)
    return pl.pallas_call(
        kern,
        grid=(batch, nq),
        in_specs=[
            pl.BlockSpec(memory_space=pltpu.SMEM),
            pl.BlockSpec((tq, B_QW), lambda b, i: (b * nq + i, COL_QB // B_QW)),
            pl.BlockSpec((tq, 2 * B_KVW), lambda b, i: (b * nq + i, COL_KB // (2 * B_KVW))),
            pl.BlockSpec((WINDOW, 2 * B_KVW),
                         lambda b, i: (jnp.maximum((b * nq + i) * sub - 1, 0),
                                       COL_KB // (2 * B_KVW))),
        ],
        out_specs=pl.BlockSpec((tq, B_QW), lambda b, i: (b * nq + i, 0)),
        out_shape=jax.ShapeDtypeStruct((batch * seq, B_QW), BF16),
        scratch_shapes=[pltpu.VMEM((tq + WINDOW, 2 * B_KVW), BF16)],
        compiler_params=_params("parallel", "parallel"),
        name="swa",
    )(sinks, p, p, p)


def _stick_kernel(qt_ref, k_ref, vt_ref, o_ref):
    i = pl.program_id(1)
    blk = SB_BLOCK
    hd = HEAD_DIM
    heads = range(C_HEADS)
    hsl = [slice(hh * hd, (hh + 1) * hd) for hh in heads]
    key = lax.broadcasted_iota(I32, (blk, blk), 0)
    qry = lax.broadcasted_iota(I32, (blk, blk), 1)
    causal = key < qry
    tri = jnp.where(qry > key, 1.0, 0.0).astype(BF16)
    own0 = pl.multiple_of(i * blk, blk)

    def logs(z):
        t = jnp.log2(1.0 + jnp.exp2(-jnp.abs(z)))
        log_beta = jnp.minimum(z, 0.0) - t
        return log_beta, log_beta - z

    def suffix(lk):
        after = _dot(tri, lk.astype(BF16))
        total = after[0:1, :] + lk[0:1, :]
        return after, total

    qts = [qt_ref[hsl[hh], :] for hh in heads]
    init = []
    for hh in heads:
        z = _dot(k_ref[pl.ds(own0, blk), hsl[hh]], qts[hh])
        log_beta, log_keep = logs(z)
        after, carry0 = suffix(jnp.where(causal, log_keep, 0.0))
        wgt = jnp.where(causal, jnp.exp2(log_beta + after), 0.0)
        init.append((carry0, _dot(vt_ref[i, hsl[hh], :], wgt.astype(BF16))))

    def step(blocks, state):
        rows = [pl.ds(j * blk if isinstance(j, int) else pl.multiple_of(j * blk, blk), blk)
                for j in blocks]
        pairs = [(hh, b) for hh in heads for b in range(len(blocks))]
        zs = {(hh, b): _dot(k_ref[rows[b], hsl[hh]], qts[hh]) for hh, b in pairs}
        lgs = {k: logs(zs[k]) for k in pairs}
        sfx = {k: suffix(lgs[k][1]) for k in pairs}
        ws = {k: jnp.exp2(lgs[k][0] + sfx[k][0]).astype(BF16) for k in pairs}
        pvs = {(hh, b): _dot(vt_ref[blocks[b], hsl[hh], :], ws[(hh, b)]) for hh, b in pairs}
        out = []
        for hh in heads:
            carry, acc = state[hh]
            for b in range(len(blocks)):
                acc = acc + pvs[(hh, b)] * jnp.exp2(carry)
                carry = carry + sfx[(hh, b)][1]
            out.append((carry, acc))
        return tuple(out)

    fin = lax.fori_loop(0, i // 2, lambda jp, s: step([i - 1 - 2 * jp, i - 2 - 2 * jp], s),
                        tuple(init))
    fin = lax.fori_loop(0, i % 2, lambda _, s: step([0], s), fin)
    out_t = jnp.concatenate([acc for _, acc in fin], axis=0)
    o_ref[...] = out_t.T.astype(o_ref.dtype)


def _blocks_t(p, col, width, batch, seq, blk):
    t = p[:, col:col + width].reshape(batch, seq // blk, blk, width)
    return jnp.swapaxes(t, 2, 3)


def _stick(p, batch, seq):
    nblk = seq // SB_BLOCK
    qt = _blocks_t(p, COL_QC, C_W, batch, seq, SB_BLOCK)
    vt = _blocks_t(p, COL_VC, C_W, batch, seq, SB_BLOCK)
    return pl.pallas_call(
        _stick_kernel,
        grid=(batch, nblk),
        in_specs=[
            pl.BlockSpec((None, None, C_W, SB_BLOCK), lambda b, i: (b, i, 0, 0)),
            pl.BlockSpec((seq, C_W), lambda b, i: (b, COL_KC // C_W)),
            pl.BlockSpec((None, nblk, C_W, SB_BLOCK), lambda b, i: (b, 0, 0, 0)),
        ],
        out_specs=pl.BlockSpec((SB_BLOCK, C_W), lambda b, i: (b * nblk + i, 0)),
        out_shape=jax.ShapeDtypeStruct((batch * seq, C_W), BF16),
        compiler_params=_params("parallel", "parallel"),
        name="stick",
    )(qt, p, vt)


def _sigmoid(x):
    return 1.0 / (1.0 + jnp.exp(-x))


def _merge_kernel(x_ref, oa_ref, ob_ref, oc_ref, ga_ref, gb_ref, gc_ref,
                  wa_ref, wb_ref, wc_ref, wo_ref, o_ref):
    mixed = _sigmoid(ga_ref[...].astype(F32)) * _dot(oa_ref[...], wa_ref[...])
    mixed = mixed + _sigmoid(gb_ref[...].astype(F32)) * _dot(ob_ref[...], wb_ref[...])
    mixed = mixed + _sigmoid(gc_ref[...].astype(F32)) * _dot(oc_ref[...], wc_ref[...])
    o_ref[...] = x_ref[...] + _dot(mixed.astype(BF16), wo_ref[...])


def _merge(x, oa, ob, oc, p, wa, wb, wc, wo, *, tm=512):
    t, d = x.shape
    full = lambda a: pl.BlockSpec(a.shape, lambda i: (0, 0))
    return pl.pallas_call(
        _merge_kernel,
        grid=(t // tm,),
        in_specs=[
            pl.BlockSpec((tm, d), lambda i: (i, 0)),
            pl.BlockSpec((tm, A_W), lambda i: (i, 0)),
            pl.BlockSpec((tm, B_QW), lambda i: (i, 0)),
            pl.BlockSpec((tm, C_W), lambda i: (i, 0)),
            pl.BlockSpec((tm, d), lambda i: (i, COL_GA // D_MODEL)),
            pl.BlockSpec((tm, d), lambda i: (i, COL_GB // D_MODEL)),
            pl.BlockSpec((tm, d), lambda i: (i, COL_GC // D_MODEL)),
            full(wa), full(wb), full(wc), full(wo),
        ],
        out_specs=pl.BlockSpec((tm, d), lambda i: (i, 0)),
        out_shape=jax.ShapeDtypeStruct((t, d), F32),
        compiler_params=_params("parallel"),
        name="merge",
    )(x, oa, ob, oc, p, p, p, wa, wb, wc, wo)


def _cross_kernel(x_ref, g_ref, wq_ref, kv_ref, wo_ref, o_ref, *rows_ref):
    x = x_ref[...]
    h = _rms(x, g_ref[...]).astype(BF16)
    q = (_dot(h, wq_ref[...]) * (X_HEAD_DIM ** -0.5)).astype(BF16)
    outs = []
    for hd in range(X_HEADS):
        hs = slice(hd * X_HEAD_DIM, (hd + 1) * X_HEAD_DIM)
        s = _dot_t(q[:, hs], kv_ref[:, hs])
        m = jnp.max(s, axis=-1, keepdims=True)
        p = jnp.exp(s - m)
        l = jnp.sum(p, axis=-1, keepdims=True)
        vs = slice(X_W + hd * X_HEAD_DIM, X_W + (hd + 1) * X_HEAD_DIM)
        outs.append(_dot(p.astype(BF16), kv_ref[:, vs]) / l)
    att = jnp.concatenate(outs, axis=1).astype(BF16)
    o_ref[...] = x + _dot(att, wo_ref[...])
    if rows_ref:
        tm, d = o_ref.shape
        for c in range(d // LANES):
            rows_ref[0][pl.ds(c, tm, stride=SUBLANES), :] = o_ref[:, c * LANES:(c + 1) * LANES]


def _cross(x, gain, wq, kv, wo, seq, *, tm=512, with_row_tiles=False):
    t, d = x.shape
    per_batch = seq // tm
    full = lambda a: pl.BlockSpec(a.shape, lambda i: (0, 0))
    out_specs = [pl.BlockSpec((tm, d), lambda i: (i, 0))]
    out_shape = [jax.ShapeDtypeStruct((t, d), F32)]
    if with_row_tiles:
        out_specs.append(pl.BlockSpec((tm * SUBLANES, LANES), lambda i: (i, 0)))
        out_shape.append(jax.ShapeDtypeStruct((t * SUBLANES, LANES), F32))
    outs = pl.pallas_call(
        _cross_kernel,
        grid=(t // tm,),
        in_specs=[
            pl.BlockSpec((tm, d), lambda i: (i, 0)),
            full(gain), full(wq),
            pl.BlockSpec((MEM_LEN, 2 * X_W), lambda i: (i // per_batch, 0)),
            full(wo),
        ],
        out_specs=out_specs,
        out_shape=out_shape,
        compiler_params=_params("parallel"),
        name="cross",
    )(x, gain, wq, kv, wo)
    return outs if with_row_tiles else outs[0]


def _silu(x):
    return x * _sigmoid(x)


def _ffn_kernel(x_ref, g_ref, wg_ref, wu_ref, wd_ref, o_ref, h_scr, acc_scr):
    f = pl.program_id(1)

    @pl.when(f == 0)
    def _():
        h_scr[...] = _rms(x_ref[...], g_ref[...]).astype(BF16)
        acc_scr[...] = x_ref[...]

    h = h_scr[...]
    a = _silu(_dot(h, wg_ref[...])) * _dot(h, wu_ref[...])
    acc_scr[...] += _dot(a.astype(BF16), wd_ref[...])

    @pl.when(f == pl.num_programs(1) - 1)
    def _():
        o_ref[...] = acc_scr[...]


def _ffn(x, gain, wg, wu, wd, *, tm=1024, tf=256):
    t, d = x.shape
    ff = wd.shape[0]
    return pl.pallas_call(
        _ffn_kernel,
        grid=(t // tm, ff // tf),
        in_specs=[
            pl.BlockSpec((tm, d), lambda i, f: (i, 0)),
            pl.BlockSpec((1, d), lambda i, f: (0, 0)),
            pl.BlockSpec((d, tf), lambda i, f: (0, f)),
            pl.BlockSpec((d, tf), lambda i, f: (0, f)),
            pl.BlockSpec((tf, d), lambda i, f: (f, 0)),
        ],
        out_specs=pl.BlockSpec((tm, d), lambda i, f: (i, 0)),
        out_shape=jax.ShapeDtypeStruct((t, d), F32),
        scratch_shapes=[pltpu.VMEM((tm, d), BF16), pltpu.VMEM((tm, d), F32)],
        compiler_params=_params("parallel", "arbitrary"),
        name="ffn",
    )(x, gain, wg, wu, wd)


def _router_kernel(x_ref, g_ref, wr_ref, o_ref):
    h = _rms(x_ref[...], g_ref[...])
    h_hi, h_lo = _split_bf16(h)
    w_hi, w_lo = _split_bf16(wr_ref[...])
    logits = _dot(h_hi, w_hi) + (_dot(h_hi, w_lo) + _dot(h_lo, w_hi))
    lane = lax.broadcasted_iota(I32, logits.shape, 1)
    logits = jnp.where(lane < N_EXPERTS, logits, NEG)
    v1 = jnp.max(logits, axis=-1, keepdims=True)
    i1 = jnp.min(jnp.where(logits == v1, lane, LANES), axis=-1, keepdims=True)
    rest = jnp.where(lane == i1, NEG, logits)
    v2 = jnp.max(rest, axis=-1, keepdims=True)
    i2 = jnp.min(jnp.where(rest == v2, lane, LANES), axis=-1, keepdims=True)
    e = jnp.exp(v2 - v1)
    w1 = 1.0 / (1.0 + e)
    w2 = e / (1.0 + e)
    out = jnp.where(lane == 0, i1.astype(F32), 0.0)
    out = jnp.where(lane == 1, i2.astype(F32), out)
    out = jnp.where(lane == 2, w1, out)
    out = jnp.where(lane == 3, w2, out)
    o_ref[...] = out


def _router(x, gain, wr, *, tm=1024):
    t, d = x.shape
    return pl.pallas_call(
        _router_kernel,
        grid=(t // tm,),
        in_specs=[
            pl.BlockSpec((tm, d), lambda i: (i, 0)),
            pl.BlockSpec((1, d), lambda i: (0, 0)),
            pl.BlockSpec((d, LANES), lambda i: (0, 0)),
        ],
        out_specs=pl.BlockSpec((tm, LANES), lambda i: (i, 0)),
        out_shape=jax.ShapeDtypeStruct((t, LANES), F32),
        compiler_params=_params("parallel"),
        name="router",
    )(x, gain, wr)


def _moe_kernel(tok_ref, dst_ref, texp_ref, tval_ref, x_hbm, g_ref, wg_ref, wu_ref, wd_ref,
                y_hbm, xbuf, obuf, h_scr, acc_scr, gsem, ssem, *, tm, nf):
    m = pl.program_id(0)
    f = pl.program_id(1)
    last_phase = pl.num_programs(0) - 1
    slot = m % 2
    other = 1 - slot
    valid = tval_ref[m] > 0
    per_step = (tm // nf) // 8 * 8
    in_steps = per_step * nf

    def row_tile(r):
        return pl.ds(pl.multiple_of(r * SUBLANES, SUBLANES), SUBLANES)

    def gather_row(tile, r, s):
        tok = pl.multiple_of(tok_ref[tile * tm + r], SUBLANES)
        return pltpu.make_async_copy(x_hbm.at[pl.ds(tok, SUBLANES), :], xbuf.at[s, row_tile(r), :],
                                     gsem.at[s])

    def scatter_row(phase, r, s):
        dst = pl.multiple_of(dst_ref[phase * tm + r], SUBLANES)
        return pltpu.make_async_copy(obuf.at[s, row_tile(r), :], y_hbm.at[pl.ds(dst, SUBLANES), :],
                                     ssem.at[s])

    def chunk(c):
        return pl.ds(c, tm, stride=SUBLANES)

    wait_group = 16

    def gather_wait(tile, s):
        def body(it, c):
            for u in range(wait_group):
                gather_row(tile, it * wait_group + u, s).wait()
            return c
        lax.fori_loop(0, tm // wait_group, body, 0)

    def scatter_wait(phase, s):
        def body(it, c):
            for u in range(wait_group):
                scatter_row(phase, it * wait_group + u, s).wait()
            return c
        lax.fori_loop(0, tm // wait_group, body, 0)

    @pl.when(f == 0)
    def _():
        @pl.when(m == 0)
        def _():
            obuf[...] = jnp.zeros_like(obuf)

            def first(r, c):
                gather_row(0, r, 0).start()
                return c
            lax.fori_loop(0, tm, first, 0, unroll=8)

        @pl.when(m >= 1)
        def _():
            scatter_wait(m - 1, slot)
        gather_wait(m, slot)
        for r in range(in_steps, tm):
            gather_row(m + 1, r, other).start(priority=r % 2)
            scatter_row(m, r, other).start(priority=r % 2)
        n_chunks = h_scr.shape[1] // LANES
        xs = [xbuf[slot, chunk(c), :] for c in range(n_chunks)]
        ssq = xs[0] * xs[0]
        for xc in xs[1:]:
            ssq = ssq + xc * xc
        inv = lax.rsqrt(jnp.sum(ssq, axis=-1, keepdims=True) / h_scr.shape[1] + EPS)
        for c in range(n_chunks):
            lanes = slice(c * LANES, (c + 1) * LANES)
            h_scr[:, lanes] = (xs[c] * inv * g_ref[:, lanes]).astype(BF16)
        acc_scr[...] = jnp.zeros_like(acc_scr)

    def move_rows():
        r0 = pl.multiple_of(f * per_step, 8)
        for rr in range(per_step):
            r = r0 + rr
            gather_row(m + 1, r, other).start(priority=rr % 2)
            scatter_row(m, r, other).start(priority=rr % 2)

    @pl.when(valid)
    def _():
        move_rows()
        h = h_scr[...]
        a = _silu(_dot(h, wg_ref[...])) * _dot(h, wu_ref[...])
        acc_scr[...] += _dot(a.astype(BF16), wd_ref[...])

    @pl.when(jnp.logical_not(valid))
    def _():
        move_rows()

    @pl.when(f == nf - 1)
    def _():
        for c in range(acc_scr.shape[1] // LANES):
            obuf[slot, chunk(c), :] = jnp.where(valid, acc_scr[:, c * LANES:(c + 1) * LANES], 0.0)

        @pl.when(m == last_phase)
        def _():
            scatter_wait(m, other)
            gather_wait(m + 1, other)


def _moe_experts(x_rows, gain, wg, wu, wd, tok, dst, tile_expert, tile_valid, n_rows, *, tm, tf):
    d = gain.shape[1]
    assert d == SUBLANES * LANES
    nf = wg.shape[2] // tf
    n_phases = tile_expert.shape[0]
    kern = functools.partial(_moe_kernel, tm=tm, nf=nf)

    def fsel(m, f, tval):
        return jnp.where(tval[m] > 0, f, nf - 1)

    grid_spec = pltpu.PrefetchScalarGridSpec(
        num_scalar_prefetch=4,
        grid=(n_phases, nf),
        in_specs=[
            pl.BlockSpec(memory_space=pl.ANY),
            pl.BlockSpec((1, d), lambda m, f, tok, dst, texp, tval: (0, 0)),
            pl.BlockSpec((None, d, tf),
                         lambda m, f, tok, dst, texp, tval: (texp[m], 0, fsel(m, f, tval))),
            pl.BlockSpec((None, d, tf),
                         lambda m, f, tok, dst, texp, tval: (texp[m], 0, fsel(m, f, tval))),
            pl.BlockSpec((None, tf, d),
                         lambda m, f, tok, dst, texp, tval: (texp[m], fsel(m, f, tval), 0)),
        ],
        out_specs=pl.BlockSpec(memory_space=pl.ANY),
        scratch_shapes=[
            pltpu.VMEM((2, tm * SUBLANES, LANES), F32),
            pltpu.VMEM((2, tm * SUBLANES, LANES), F32),
            pltpu.VMEM((tm, d), BF16),
            pltpu.VMEM((tm, d), F32),
            pltpu.SemaphoreType.DMA((2,)),
            pltpu.SemaphoreType.DMA((2,)),
        ],
    )
    return pl.pallas_call(
        kern,
        grid_spec=grid_spec,
        out_shape=jax.ShapeDtypeStruct((n_rows * SUBLANES, LANES), F32),
        compiler_params=_params("arbitrary", "arbitrary"),
        name="moe_experts",
    )(tok, dst, tile_expert, tile_valid, x_rows, gain, wg, wu, wd)


def _combine_kernel(x_ref, r_ref, y0_ref, y1_ref, g_ref, o_ref, *, out_norm):
    tm, d = x_ref.shape
    route = r_ref[...]
    w0 = route[:, TOP_K:TOP_K + 1]
    w1 = route[:, TOP_K + 1:TOP_K + 2]
    for c in range(d // LANES):
        lanes = slice(c * LANES, (c + 1) * LANES)
        rows = pl.ds(c, tm, stride=SUBLANES)
        o_ref[:, lanes] = x_ref[:, lanes] + w0 * y0_ref[rows, :] + w1 * y1_ref[rows, :]
    if out_norm:
        o_ref[...] = _rms(o_ref[...], g_ref[...])


def _moe_combine(x, route, y_rows, out_gain, *, tm=512):
    t, d = x.shape
    assert TOP_K == 2
    out_norm = out_gain is not None
    gain = out_gain if out_norm else jnp.ones((1, d), F32)
    return pl.pallas_call(
        functools.partial(_combine_kernel, out_norm=out_norm),
        grid=(t // tm,),
        in_specs=[
            pl.BlockSpec((tm, d), lambda i: (i, 0)),
            pl.BlockSpec((tm, LANES), lambda i: (i, 0)),
            pl.BlockSpec((tm * SUBLANES, LANES), lambda i: (i, 0)),
            pl.BlockSpec((tm * SUBLANES, LANES), lambda i: (t // tm + i, 0)),
            pl.BlockSpec((1, d), lambda i: (0, 0)),
        ],
        out_specs=pl.BlockSpec((tm, d), lambda i: (i, 0)),
        out_shape=jax.ShapeDtypeStruct((t, d), F32),
        compiler_params=_params("parallel"),
        name="moe_combine",
    )(x, route, y_rows, y_rows, gain)


def _moe(x, x_rows, gain, w_router, wg, wu, wd, out_gain=None, *, tm=MOE_TM, tf=MOE_TF):
    t, d = x.shape
    wr = jnp.zeros((d, LANES), F32).at[:, :N_EXPERTS].set(w_router)
    route = _router(x, gain, wr)
    tok, dst, tile_expert, tile_valid, n_rows = _route_metadata(route[:, :TOP_K].astype(I32), tm)
    y_rows = _moe_experts(x_rows, gain, wg, wu, wd, tok * SUBLANES, dst * SUBLANES,
                          tile_expert, tile_valid, n_rows, tm=tm, tf=tf)
    return _moe_combine(x, route, y_rows, out_gain)


def _route_metadata(experts, tm):
    t = experts.shape[0]
    n_slots = TOP_K * t
    eflat = experts.T.reshape(-1)
    onehot = (eflat[:, None] == jnp.arange(N_EXPERTS, dtype=I32)[None, :]).astype(I32)
    csum = jnp.cumsum(onehot, axis=0)
    rank = jnp.sum((csum - 1) * onehot, axis=1)
    counts = csum[-1]
    tiles_per = (counts + tm - 1) // tm
    tile_end = jnp.cumsum(tiles_per)
    start = (tile_end - tiles_per) * tm
    pos = (jnp.sum(start[None, :] * onehot, axis=1) + rank).astype(I32)
    n_tiles = n_slots // tm + N_EXPERTS
    n_rows_pad = n_tiles * tm
    slot_of_row = jnp.full((n_rows_pad,), -1, I32).at[pos].set(jnp.arange(n_slots, dtype=I32))
    real = slot_of_row >= 0
    tok = jnp.concatenate([jnp.where(real, slot_of_row % t, 0), jnp.zeros((2 * tm,), I32)])
    real_ext = jnp.concatenate([jnp.zeros((tm,), bool), real])
    dump = n_slots + jnp.cumsum(jnp.logical_not(real_ext).astype(I32)) - 1
    dst = jnp.where(real_ext, jnp.concatenate([jnp.zeros((tm,), I32), slot_of_row]), dump)
    tile_id = jnp.arange(n_tiles + 1, dtype=I32)
    tile_expert = jnp.sum((tile_id[:, None] >= tile_end[None, :]).astype(I32), axis=1)
    tile_valid = (tile_id < tile_end[-1]).astype(I32)
    last_expert = jnp.max(jnp.where(counts > 0, jnp.arange(N_EXPERTS, dtype=I32), 0))
    tile_expert = jnp.where(tile_valid > 0, tile_expert, last_expert).astype(I32)
    return tok, dst.astype(I32), tile_expert, tile_valid, n_rows_pad + tm


def _final_norm_kernel(x_ref, g_ref, o_ref):
    o_ref[...] = _rms(x_ref[...], g_ref[...])


def _final_norm(x, gain, *, tm=1024):
    t, d = x.shape
    return pl.pallas_call(
        _final_norm_kernel,
        grid=(t // tm,),
        in_specs=[pl.BlockSpec((tm, d), lambda i: (i, 0)), pl.BlockSpec((1, d), lambda i: (0, 0))],
        out_specs=pl.BlockSpec((tm, d), lambda i: (i, 0)),
        out_shape=jax.ShapeDtypeStruct((t, d), F32),
        compiler_params=_params("parallel"),
        name="final_norm",
    )(x, gain)


def _rope_tables(seq):
    half = HEAD_DIM // 2
    inv_freq = ROPE_THETA ** (-jnp.arange(half, dtype=F32) / half)
    ang = jnp.arange(seq, dtype=F32)[:, None] * inv_freq[None, :]
    cos, sin = jnp.cos(ang), jnp.sin(ang)
    reps = LANES // HEAD_DIM
    cos_t = jnp.tile(jnp.concatenate([cos, cos], axis=1), (1, reps))
    sin_t = jnp.tile(jnp.concatenate([-sin, sin], axis=1), (1, reps))
    return cos_t, sin_t


def _in_proj_weight(w_in):
    o = 0
    offs = {}
    for name, width in (("qa", A_W), ("ka", A_W), ("va", A_W), ("qb", B_QW), ("kb", B_KVW),
                        ("vb", B_KVW), ("qc", C_W), ("kc", C_W), ("vc", C_W), ("g", 3 * D_MODEL)):
        offs[name] = (o, o + width)
        o += width
    scale = HEAD_DIM ** -0.5
    log2e = 1.4426950408889634
    q_scale = {"qa": scale * log2e, "qb": scale, "qc": scale * log2e}
    parts = []
    for name in ("g", "qa", "ka", "qb", "kb", "vb", "va", "qc", "kc", "vc"):
        lo, hi = offs[name]
        blk = w_in[:, lo:hi]
        if name in q_scale:
            blk = blk * q_scale[name]
        parts.append(blk)
    return jnp.concatenate(parts, axis=1).astype(BF16)


def _rope_group_map():
    groups = {}
    for t in range(IN_W // PROJ_TN):
        lo, hi = t * PROJ_TN, (t + 1) * PROJ_TN
        n = (min(hi, ROPE_HI) - max(lo, ROPE_LO)) // LANES
        if n > 0:
            assert max(lo, ROPE_LO) == lo
            groups[t] = n
    return groups


def kernel(x, mem, norm_mix, w_in, w_proj_a, w_proj_b, w_proj_c, w_mix_out, sinks, norm_cross,
           norm_mem, w_xq, w_xkv, w_xo, norm_ffn, ffn_gate, ffn_up, ffn_down, moe_router,
           moe_gate, moe_up, moe_down, final_norm):
    batch, seq, d = x.shape
    depth = norm_mix.shape[0]
    assert d == D_MODEL and seq % 1024 == 0 and mem.shape[1] == MEM_LEN
    t = batch * seq
    xf = x.reshape(t, d)
    memf = mem.reshape(batch * MEM_LEN, d)
    cos_t, sin_t = _rope_tables(seq)
    rope_groups = _rope_group_map()
    ones_tab = jnp.ones((MEM_LEN, LANES), F32)

    for l in range(depth):
        gain = lambda g: g[l].reshape(1, d)
        p = _norm_proj(xf, gain(norm_mix), _in_proj_weight(w_in[l]), cos_t, sin_t, seq,
                       tm=1024, tn=PROJ_TN, rope_groups=rope_groups)
        oa = _moba(p, batch, seq)
        ob = _swa(p, sinks[l], batch, seq)
        oc = _stick(p, batch, seq)
        xf = _merge(xf, oa, ob, oc, p, w_proj_a[l].astype(BF16), w_proj_b[l].astype(BF16),
                    w_proj_c[l].astype(BF16), w_mix_out[l].astype(BF16))
        kv = _norm_proj(memf, gain(norm_mem), w_xkv[l].astype(BF16), ones_tab, ones_tab, MEM_LEN,
                        tm=MEM_LEN, tn=2 * X_W, rope_groups={})
        cross_args = (gain(norm_cross), w_xq[l].astype(BF16), kv, w_xo[l].astype(BF16), seq)
        if l % 2 == 0:
            i = l // 2
            xf = _cross(xf, *cross_args)
            xf = _ffn(xf, gain(norm_ffn), ffn_gate[i].astype(BF16), ffn_up[i].astype(BF16),
                      ffn_down[i].astype(BF16), tf=FFN_TF)
        else:
            i = l // 2
            xf, x_rows = _cross(xf, *cross_args, with_row_tiles=True)
            out_gain = final_norm.reshape(1, d) if l == depth - 1 else None
            xf = _moe(xf, x_rows, gain(norm_ffn), moe_router[i], moe_gate[i].astype(BF16),
                      moe_up[i].astype(BF16), moe_down[i].astype(BF16), out_gain)
    if depth % 2 == 1:
        xf = _final_norm(xf, final_norm.reshape(1, d))
    return xf.reshape(batch, seq, d)
```

```python
import functools

import jax
import jax.numpy as jnp
from jax import lax
from jax.experimental import pallas as pl
from jax.experimental.pallas import tpu as pltpu

F32 = jnp.float32
BF16 = jnp.bfloat16
I32 = jnp.int32

D_MODEL = 1024
HEAD_DIM = 64
A_HEADS = 4
MOBA_BLOCK = 256
MOBA_TOPK = 3
B_HEADS = 8
B_KV_HEADS = 2
WINDOW = 128
C_HEADS = 4
SB_BLOCK = 256
MEM_LEN = 256
X_HEADS = 4
X_HEAD_DIM = 128
N_EXPERTS = 8
TOP_K = 2
ROPE_THETA = 10000.0
EPS = 1e-6

A_W = A_HEADS * HEAD_DIM
B_QW = B_HEADS * HEAD_DIM
B_KVW = B_KV_HEADS * HEAD_DIM
C_W = C_HEADS * HEAD_DIM
X_W = X_HEADS * X_HEAD_DIM
QKV_W = 3 * A_W + B_QW + 2 * B_KVW + 3 * C_W
IN_W = QKV_W + 3 * D_MODEL

LANES = 128
SUBLANES = 8
NEG = -1e30

COL_GA, COL_GB, COL_GC = 0, D_MODEL, 2 * D_MODEL
COL_QA = 3 * D_MODEL
COL_KA = COL_QA + A_W
COL_QB = COL_KA + A_W
COL_KB = COL_QB + B_QW
COL_VB = COL_KB + B_KVW
COL_VA = COL_VB + B_KVW
COL_QC = COL_VA + A_W
COL_KC = COL_QC + C_W
COL_VC = COL_KC + C_W
ROPE_LO, ROPE_HI = COL_QA, COL_VB

PROJ_TN = 768
FFN_TF = 256
MOE_TF = 512
MOE_TM = 1072
VMEM_LIMIT = 48 * 1024 * 1024


def _params(*sem):
    return pltpu.CompilerParams(dimension_semantics=sem, vmem_limit_bytes=VMEM_LIMIT)


def _rms(x, g):
    ms = jnp.mean(x * x, axis=-1, keepdims=True)
    return x * lax.rsqrt(ms + EPS) * g


def _dot(a, b):
    return jnp.dot(a, b, preferred_element_type=F32)


def _dot_t(a, b):
    return lax.dot_general(a, b, (((1,), (1,)), ((), ())), preferred_element_type=F32)


def _split_bf16(v):
    hi = v.astype(BF16)
    lo = (v - hi.astype(F32)).astype(BF16)
    return hi, lo


def _norm_proj_kernel(x_ref, g_ref, w_ref, cos_ref, sin_ref, o_ref, h_scr, *, rope_groups):
    j = pl.program_id(1)

    @pl.when(j == 0)
    def _():
        h_scr[...] = _rms(x_ref[...], g_ref[...]).astype(BF16)

    acc = _dot(h_scr[...], w_ref[...])
    n_groups = acc.shape[1] // LANES

    def rope(y):
        lane = lax.broadcasted_iota(I32, y.shape, 1)
        first_half = (lane % HEAD_DIM) < (HEAD_DIM // 2)
        sw = jnp.where(first_half, pltpu.roll(y, LANES - HEAD_DIM // 2, 1),
                       pltpu.roll(y, HEAD_DIM // 2, 1))
        return y * cos_ref[...] + sw * sin_ref[...]

    roped = sorted(rope_groups)

    for t in roped:
        @pl.when(j == t)
        def _(t=t):
            for gi in range(n_groups):
                y = acc[:, gi * LANES:(gi + 1) * LANES]
                if gi < rope_groups[t]:
                    y = rope(y)
                o_ref[:, gi * LANES:(gi + 1) * LANES] = y.astype(o_ref.dtype)

    is_plain = j >= 0
    for t in roped:
        is_plain = jnp.logical_and(is_plain, j != t)

    @pl.when(is_plain)
    def _():
        o_ref[...] = acc.astype(o_ref.dtype)


def _norm_proj(x, gain, w, cos, sin, seq, *, tm, tn, rope_groups):
    t, d = x.shape
    n = w.shape[1]
    pos_blocks = seq // tm
    kern = functools.partial(_norm_proj_kernel, rope_groups=rope_groups)
    return pl.pallas_call(
        kern,
        grid=(t // tm, n // tn),
        in_specs=[
            pl.BlockSpec((tm, d), lambda i, j: (i, 0)),
            pl.BlockSpec((1, d), lambda i, j: (0, 0)),
            pl.BlockSpec((d, tn), lambda i, j: (0, j)),
            pl.BlockSpec((tm, LANES), lambda i, j: (i % pos_blocks, 0)),
            pl.BlockSpec((tm, LANES), lambda i, j: (i % pos_blocks, 0)),
        ],
        out_specs=pl.BlockSpec((tm, tn), lambda i, j: (i, j)),
        out_shape=jax.ShapeDtypeStruct((t, n), BF16),
        scratch_shapes=[pltpu.VMEM((tm, d), BF16)],
        compiler_params=_params("parallel", "arbitrary"),
        name="norm_proj",
    )(x, gain, w, cos, sin)


MOBA_VROWS = HEAD_DIM + 16


def _moba_kernel(qt_ref, k_ref, vt_ref, o_ref, kmean_scr, kaug_scr, vaug_scr, qaug_scr, *, nblk):
    i = pl.program_id(1)
    blk = MOBA_BLOCK
    hd = HEAD_DIM
    aug = 2 * hd
    vr = MOBA_VROWS
    heads = range(A_HEADS)
    hsl = [slice(hh * hd, (hh + 1) * hd) for hh in heads]

    @pl.when(i == 0)
    def _():
        ones = jnp.ones((vr - hd, blk), BF16)
        blk_lane = lax.broadcasted_iota(I32, (blk, hd), 1)
        for n in range(nblk):
            rows = slice(n * blk, (n + 1) * blk)
            kmean_scr[n:n + 1, :] = jnp.mean(k_ref[rows, :].astype(F32), axis=0, keepdims=True)
            onehot = jnp.where(blk_lane == n, 1.0, 0.0).astype(BF16)
            for hh in heads:
                kaug_scr[rows, hh * aug:hh * aug + hd] = k_ref[rows, hsl[hh]]
                kaug_scr[rows, hh * aug + hd:(hh + 1) * aug] = onehot
                vaug_scr[n, hh * vr:hh * vr + hd, :] = vt_ref[n, hsl[hh], :]
                vaug_scr[n, hh * vr + hd:(hh + 1) * vr, :] = ones

    key = lax.broadcasted_iota(I32, (blk, blk), 0)
    qry = lax.broadcasted_iota(I32, (blk, blk), 1)
    causal = key <= qry
    blk_id = lax.broadcasted_iota(I32, (nblk, blk), 0)
    own0 = pl.multiple_of(i * blk, blk)

    init = []
    for hh in heads:
        qt = qt_ref[hsl[hh], :]
        km_hi, km_lo = _split_bf16(kmean_scr[:, hsl[hh]])
        g = _dot(km_hi, qt) + _dot(km_lo, qt)

        cnt = jnp.zeros((nblk, blk), I32)
        for m in range(nblk):
            gm = g[m:m + 1, :]
            beats = (gm > g) | ((gm == g) & (m < blk_id))
            cnt = cnt + jnp.where(beats, (m < i).astype(I32), 0)
        sel = (cnt < MOBA_TOPK) & (blk_id < i)
        qaug_scr[hh * aug:hh * aug + hd, :] = qt
        qaug_scr[hh * aug + hd:hh * aug + hd + nblk, :] = jnp.where(sel, 0.0, NEG).astype(BF16)
        qaug_scr[hh * aug + hd + nblk:(hh + 1) * aug, :] = jnp.zeros((hd - nblk, blk), BF16)

        s = _dot(k_ref[pl.ds(own0, blk), hsl[hh]], qt)
        s = jnp.where(causal, s, NEG)
        m0 = jnp.max(s, axis=0, keepdims=True)
        p = jnp.exp2(s - m0)
        init.append((m0, _dot(vaug_scr[i, hh * vr:(hh + 1) * vr, :], p.astype(BF16))))

    def step(blocks, carry):
        rows = [pl.ds(pl.multiple_of(j * blk, blk), blk) for j in blocks]
        ss = [[_dot(kaug_scr[r, hh * aug:(hh + 1) * aug], qaug_scr[hh * aug:(hh + 1) * aug, :])
               for r in rows] for hh in heads]
        ms = []
        for hh in heads:
            m_new = carry[hh][0]
            for s in ss[hh]:
                m_new = jnp.maximum(m_new, jnp.max(s, axis=0, keepdims=True))
            ms.append(m_new)
        ps = [[jnp.exp2(s - ms[hh]).astype(BF16) for s in ss[hh]] for hh in heads]
        pvs = []
        for hh in heads:
            pv = None
            for j, p in zip(blocks, ps[hh]):
                d = _dot(vaug_scr[j, hh * vr:(hh + 1) * vr, :], p)
                pv = d if pv is None else pv + d
            pvs.append(pv)
        return tuple((ms[hh], jnp.exp2(carry[hh][0] - ms[hh]) * carry[hh][1] + pvs[hh])
                     for hh in heads)

    fin = lax.fori_loop(0, i // 2, lambda jp, c: step([2 * jp, 2 * jp + 1], c), tuple(init))
    fin = lax.fori_loop(0, i % 2, lambda _, c: step([i - 1], c), fin)
    out_t = jnp.concatenate([acc[:hd, :] / acc[hd:hd + 1, :] for _, acc in fin], axis=0)
    o_ref[...] = out_t.T.astype(o_ref.dtype)


def _moba(p, batch, seq):
    nblk = seq // MOBA_BLOCK
    assert nblk <= HEAD_DIM
    qt = _blocks_t(p, COL_QA, A_W, batch, seq, MOBA_BLOCK)
    vt = _blocks_t(p, COL_VA, A_W, batch, seq, MOBA_BLOCK)
    kern = functools.partial(_moba_kernel, nblk=nblk)
    return pl.pallas_call(
        kern,
        grid=(batch, nblk),
        in_specs=[
            pl.BlockSpec((None, None, A_W, MOBA_BLOCK), lambda b, i: (b, i, 0, 0)),
            pl.BlockSpec((seq, A_W), lambda b, i: (b, COL_KA // A_W)),
            pl.BlockSpec((None, nblk, A_W, MOBA_BLOCK), lambda b, i: (b, 0, 0, 0)),
        ],
        out_specs=pl.BlockSpec((MOBA_BLOCK, A_W), lambda b, i: (b * nblk + i, 0)),
        out_shape=jax.ShapeDtypeStruct((batch * seq, A_W), BF16),
        scratch_shapes=[
            pltpu.VMEM((nblk, A_W), F32),
            pltpu.VMEM((seq, 2 * A_W), BF16),
            pltpu.VMEM((nblk, A_HEADS * MOBA_VROWS, MOBA_BLOCK), BF16),
            pltpu.VMEM((2 * A_W, MOBA_BLOCK), BF16),
        ],
        compiler_params=_params("parallel", "arbitrary"),
        name="moba",
    )(qt, p, vt)


def _swa_kernel(sinks_ref, q_ref, kv_ref, pkv_ref, o_ref, kv_scr, *, tq):
    i = pl.program_id(1)
    w = WINDOW
    kv_scr[0:w, :] = pkv_ref[...]
    kv_scr[w:, :] = kv_ref[...]

    row = lax.broadcasted_iota(I32, (w, 2 * w), 0)
    col = lax.broadcasted_iota(I32, (w, 2 * w), 1)
    rel = row + w - col
    band = (rel >= 0) & (rel < w)
    group = B_HEADS // B_KV_HEADS

    def body(n, carry):
        base = pl.multiple_of(n * w, w)
        kvt = kv_scr[pl.ds(base, 2 * w), :]
        qt = q_ref[pl.ds(base, w), :]
        col_min = jnp.where(jnp.logical_and(i == 0, n == 0), w, 0)
        mask = band & (col >= col_min)
        outs = []
        for kh in range(B_KV_HEADS):
            kk = kvt[:, kh * HEAD_DIM:(kh + 1) * HEAD_DIM]
            vv = kvt[:, B_KVW + kh * HEAD_DIM:B_KVW + (kh + 1) * HEAD_DIM]
            for gi in range(group):
                hd = kh * group + gi
                qh = qt[:, hd * HEAD_DIM:(hd + 1) * HEAD_DIM]
                s = jnp.where(mask, _dot_t(qh, kk), NEG)
                sink = sinks_ref[hd]
                m = jnp.maximum(jnp.max(s, axis=-1, keepdims=True), sink)
                p = jnp.exp(s - m)
                l = jnp.sum(p, axis=-1, keepdims=True) + jnp.exp(sink - m)
                outs.append(_dot(p.astype(BF16), vv) / l)
        o_ref[pl.ds(base, w), :] = jnp.concatenate(outs, axis=1).astype(o_ref.dtype)
        return carry

    lax.fori_loop(0, tq // w, body, 0)


def _swa(p, sinks, batch, seq, *, tq=512):
    nq = seq // tq
    sub = tq // WINDOW
    kern = functools.partial(_swa_kernel, tq=tq)
    return pl.pallas_call(
        kern,
        grid=(batch, nq),
        in_specs=[
            pl.BlockSpec(memory_space=pltpu.SMEM),
            pl.BlockSpec((tq, B_QW), lambda b, i: (b * nq + i, COL_QB // B_QW)),
            pl.BlockSpec((tq, 2 * B_KVW), lambda b, i: (b * nq + i, COL_KB // (2 * B_KVW))),
            pl.BlockSpec((WINDOW, 2 * B_KVW),
                         lambda b, i: (jnp.maximum((b * nq + i) * sub - 1, 0),
                                       COL_KB // (2 * B_KVW))),
        ],
        out_specs=pl.BlockSpec((tq, B_QW), lambda b, i: (b * nq + i, 0)),
        out_shape=jax.ShapeDtypeStruct((batch * seq, B_QW), BF16),
        scratch_shapes=[pltpu.VMEM((tq + WINDOW, 2 * B_KVW), BF16)],
        compiler_params=_params("parallel", "parallel"),
        name="swa",
    )(sinks, p, p, p)


def _stick_kernel(qt_ref, k_ref, vt_ref, o_ref):
    i = pl.program_id(1)
    blk = SB_BLOCK
    hd = HEAD_DIM
    heads = range(C_HEADS)
    hsl = [slice(hh * hd, (hh + 1) * hd) for hh in heads]
    key = lax.broadcasted_iota(I32, (blk, blk), 0)
    qry = lax.broadcasted_iota(I32, (blk, blk), 1)
    causal = key < qry
    ntri = jnp.where(qry >= key, -1.0, 0.0).astype(BF16)
    own0 = pl.multiple_of(i * blk, blk)

    def softplus2(z):
        return jnp.maximum(z, 0.0) + jnp.log2(1.0 + jnp.exp2(-jnp.abs(z)))

    def suffix(sp):
        incl = _dot(ntri, sp.astype(BF16))
        return incl, incl[0:1, :]

    qts = [qt_ref[hsl[hh], :] for hh in heads]
    init = []
    for hh in heads:
        z = _dot(k_ref[pl.ds(own0, blk), hsl[hh]], qts[hh])
        incl, carry0 = suffix(jnp.where(causal, softplus2(z), 0.0))
        wgt = jnp.where(causal, jnp.exp2(z + incl), 0.0)
        init.append((carry0, _dot(vt_ref[i, hsl[hh], :], wgt.astype(BF16))))

    def step(blocks, state):
        rows = [pl.ds(j * blk if isinstance(j, int) else pl.multiple_of(j * blk, blk), blk)
                for j in blocks]
        pairs = [(hh, b) for hh in heads for b in range(len(blocks))]
        zs = {(hh, b): _dot(k_ref[rows[b], hsl[hh]], qts[hh]) for hh, b in pairs}
        sfx = {k: suffix(softplus2(zs[k])) for k in pairs}
        ws = {k: jnp.exp2(zs[k] + sfx[k][0]).astype(BF16) for k in pairs}
        pvs = {(hh, b): _dot(vt_ref[blocks[b], hsl[hh], :], ws[(hh, b)]) for hh, b in pairs}
        out = []
        for hh in heads:
            carry, acc = state[hh]
            for b in range(len(blocks)):
                acc = acc + pvs[(hh, b)] * jnp.exp2(carry)
                carry = carry + sfx[(hh, b)][1]
            out.append((carry, acc))
        return tuple(out)

    fin = lax.fori_loop(0, i // 2, lambda jp, s: step([i - 1 - 2 * jp, i - 2 - 2 * jp], s),
                        tuple(init))
    fin = lax.fori_loop(0, i % 2, lambda _, s: step([0], s), fin)
    out_t = jnp.concatenate([acc for _, acc in fin], axis=0)
    o_ref[...] = out_t.T.astype(o_ref.dtype)


def _blocks_t(p, col, width, batch, seq, blk):
    t = p[:, col:col + width].reshape(batch, seq // blk, blk, width)
    return jnp.swapaxes(t, 2, 3)


def _stick(p, batch, seq):
    nblk = seq // SB_BLOCK
    qt = _blocks_t(p, COL_QC, C_W, batch, seq, SB_BLOCK)
    vt = _blocks_t(p, COL_VC, C_W, batch, seq, SB_BLOCK)
    return pl.pallas_call(
        _stick_kernel,
        grid=(batch, nblk),
        in_specs=[
            pl.BlockSpec((None, None, C_W, SB_BLOCK), lambda b, i: (b, i, 0, 0)),
            pl.BlockSpec((seq, C_W), lambda b, i: (b, COL_KC // C_W)),
            pl.BlockSpec((None, nblk, C_W, SB_BLOCK), lambda b, i: (b, 0, 0, 0)),
        ],
        out_specs=pl.BlockSpec((SB_BLOCK, C_W), lambda b, i: (b * nblk + i, 0)),
        out_shape=jax.ShapeDtypeStruct((batch * seq, C_W), BF16),
        compiler_params=_params("parallel", "parallel"),
        name="stick",
    )(qt, p, vt)


def _sigmoid(x):
    return 1.0 / (1.0 + jnp.exp(-x))


def _merge_kernel(x_ref, oa_ref, ob_ref, oc_ref, ga_ref, gb_ref, gc_ref,
                  wa_ref, wb_ref, wc_ref, wo_ref, o_ref):
    mixed = _sigmoid(ga_ref[...].astype(F32)) * _dot(oa_ref[...], wa_ref[...])
    mixed = mixed + _sigmoid(gb_ref[...].astype(F32)) * _dot(ob_ref[...], wb_ref[...])
    mixed = mixed + _sigmoid(gc_ref[...].astype(F32)) * _dot(oc_ref[...], wc_ref[...])
    o_ref[...] = x_ref[...] + _dot(mixed.astype(BF16), wo_ref[...])


def _merge(x, oa, ob, oc, p, wa, wb, wc, wo, *, tm=512):
    t, d = x.shape
    full = lambda a: pl.BlockSpec(a.shape, lambda i: (0, 0))
    return pl.pallas_call(
        _merge_kernel,
        grid=(t // tm,),
        in_specs=[
            pl.BlockSpec((tm, d), lambda i: (i, 0)),
            pl.BlockSpec((tm, A_W), lambda i: (i, 0)),
            pl.BlockSpec((tm, B_QW), lambda i: (i, 0)),
            pl.BlockSpec((tm, C_W), lambda i: (i, 0)),
            pl.BlockSpec((tm, d), lambda i: (i, COL_GA // D_MODEL)),
            pl.BlockSpec((tm, d), lambda i: (i, COL_GB // D_MODEL)),
            pl.BlockSpec((tm, d), lambda i: (i, COL_GC // D_MODEL)),
            full(wa), full(wb), full(wc), full(wo),
        ],
        out_specs=pl.BlockSpec((tm, d), lambda i: (i, 0)),
        out_shape=jax.ShapeDtypeStruct((t, d), F32),
        compiler_params=_params("parallel"),
        name="merge",
    )(x, oa, ob, oc, p, p, p, wa, wb, wc, wo)


def _cross_kernel(x_ref, g_ref, wq_ref, kv_ref, wo_ref, o_ref, *rows_ref):
    x = x_ref[...]
    h = _rms(x, g_ref[...]).astype(BF16)
    q = (_dot(h, wq_ref[...]) * (X_HEAD_DIM ** -0.5)).astype(BF16)
    outs = []
    for hd in range(X_HEADS):
        hs = slice(hd * X_HEAD_DIM, (hd + 1) * X_HEAD_DIM)
        s = _dot_t(q[:, hs], kv_ref[:, hs])
        m = jnp.max(s, axis=-1, keepdims=True)
        p = jnp.exp(s - m)
        l = jnp.sum(p, axis=-1, keepdims=True)
        vs = slice(X_W + hd * X_HEAD_DIM, X_W + (hd + 1) * X_HEAD_DIM)
        outs.append(_dot(p.astype(BF16), kv_ref[:, vs]) / l)
    att = jnp.concatenate(outs, axis=1).astype(BF16)
    o_ref[...] = x + _dot(att, wo_ref[...])
    if rows_ref:
        tm, d = o_ref.shape
        for c in range(d // LANES):
            rows_ref[0][pl.ds(c, tm, stride=SUBLANES), :] = o_ref[:, c * LANES:(c + 1) * LANES]


def _cross(x, gain, wq, kv, wo, seq, *, tm=512, with_row_tiles=False):
    t, d = x.shape
    per_batch = seq // tm
    full = lambda a: pl.BlockSpec(a.shape, lambda i: (0, 0))
    out_specs = [pl.BlockSpec((tm, d), lambda i: (i, 0))]
    out_shape = [jax.ShapeDtypeStruct((t, d), F32)]
    if with_row_tiles:
        out_specs.append(pl.BlockSpec((tm * SUBLANES, LANES), lambda i: (i, 0)))
        out_shape.append(jax.ShapeDtypeStruct((t * SUBLANES, LANES), F32))
    outs = pl.pallas_call(
        _cross_kernel,
        grid=(t // tm,),
        in_specs=[
            pl.BlockSpec((tm, d), lambda i: (i, 0)),
            full(gain), full(wq),
            pl.BlockSpec((MEM_LEN, 2 * X_W), lambda i: (i // per_batch, 0)),
            full(wo),
        ],
        out_specs=out_specs,
        out_shape=out_shape,
        compiler_params=_params("parallel"),
        name="cross",
    )(x, gain, wq, kv, wo)
    return outs if with_row_tiles else outs[0]


def _silu(x):
    return x * _sigmoid(x)


def _ffn_kernel(x_ref, g_ref, wg_ref, wu_ref, wd_ref, o_ref, h_scr, acc_scr):
    f = pl.program_id(1)

    @pl.when(f == 0)
    def _():
        h_scr[...] = _rms(x_ref[...], g_ref[...]).astype(BF16)
        acc_scr[...] = x_ref[...]

    h = h_scr[...]
    a = _silu(_dot(h, wg_ref[...])) * _dot(h, wu_ref[...])
    acc_scr[...] += _dot(a.astype(BF16), wd_ref[...])

    @pl.when(f == pl.num_programs(1) - 1)
    def _():
        o_ref[...] = acc_scr[...]


def _ffn(x, gain, wg, wu, wd, *, tm=1024, tf=256):
    t, d = x.shape
    ff = wd.shape[0]
    return pl.pallas_call(
        _ffn_kernel,
        grid=(t // tm, ff // tf),
        in_specs=[
            pl.BlockSpec((tm, d), lambda i, f: (i, 0)),
            pl.BlockSpec((1, d), lambda i, f: (0, 0)),
            pl.BlockSpec((d, tf), lambda i, f: (0, f)),
            pl.BlockSpec((d, tf), lambda i, f: (0, f)),
            pl.BlockSpec((tf, d), lambda i, f: (f, 0)),
        ],
        out_specs=pl.BlockSpec((tm, d), lambda i, f: (i, 0)),
        out_shape=jax.ShapeDtypeStruct((t, d), F32),
        scratch_shapes=[pltpu.VMEM((tm, d), BF16), pltpu.VMEM((tm, d), F32)],
        compiler_params=_params("parallel", "arbitrary"),
        name="ffn",
    )(x, gain, wg, wu, wd)


def _router_kernel(x_ref, g_ref, wr_ref, o_ref):
    h = _rms(x_ref[...], g_ref[...])
    h_hi, h_lo = _split_bf16(h)
    w_hi, w_lo = _split_bf16(wr_ref[...])
    logits = _dot(h_hi, w_hi) + (_dot(h_hi, w_lo) + _dot(h_lo, w_hi))
    lane = lax.broadcasted_iota(I32, logits.shape, 1)
    logits = jnp.where(lane < N_EXPERTS, logits, NEG)
    v1 = jnp.max(logits, axis=-1, keepdims=True)
    i1 = jnp.min(jnp.where(logits == v1, lane, LANES), axis=-1, keepdims=True)
    rest = jnp.where(lane == i1, NEG, logits)
    v2 = jnp.max(rest, axis=-1, keepdims=True)
    i2 = jnp.min(jnp.where(rest == v2, lane, LANES), axis=-1, keepdims=True)
    e = jnp.exp(v2 - v1)
    w1 = 1.0 / (1.0 + e)
    w2 = e / (1.0 + e)
    out = jnp.where(lane == 0, i1.astype(F32), 0.0)
    out = jnp.where(lane == 1, i2.astype(F32), out)
    out = jnp.where(lane == 2, w1, out)
    out = jnp.where(lane == 3, w2, out)
    o_ref[...] = out


def _router(x, gain, wr, *, tm=1024):
    t, d = x.shape
    return pl.pallas_call(
        _router_kernel,
        grid=(t // tm,),
        in_specs=[
            pl.BlockSpec((tm, d), lambda i: (i, 0)),
            pl.BlockSpec((1, d), lambda i: (0, 0)),
            pl.BlockSpec((d, LANES), lambda i: (0, 0)),
        ],
        out_specs=pl.BlockSpec((tm, LANES), lambda i: (i, 0)),
        out_shape=jax.ShapeDtypeStruct((t, LANES), F32),
        compiler_params=_params("parallel"),
        name="router",
    )(x, gain, wr)


def _moe_kernel(tok_ref, dst_ref, texp_ref, tval_ref, x_hbm, g_ref, wg_ref, wu_ref, wd_ref,
                y_hbm, xbuf, obuf, h_scr, acc_scr, gsem, ssem, *, tm, nf):
    m = pl.program_id(0)
    f = pl.program_id(1)
    last_phase = pl.num_programs(0) - 1
    slot = m % 2
    other = 1 - slot
    valid = tval_ref[m] > 0
    per_step = (tm // nf) // 8 * 8
    in_steps = per_step * nf

    def row_tile(r):
        return pl.ds(pl.multiple_of(r * SUBLANES, SUBLANES), SUBLANES)

    def gather_row(tile, r, s):
        tok = pl.multiple_of(tok_ref[tile * tm + r], SUBLANES)
        return pltpu.make_async_copy(x_hbm.at[pl.ds(tok, SUBLANES), :], xbuf.at[s, row_tile(r), :],
                                     gsem.at[s])

    def scatter_row(phase, r, s):
        dst = pl.multiple_of(dst_ref[phase * tm + r], SUBLANES)
        return pltpu.make_async_copy(obuf.at[s, row_tile(r), :], y_hbm.at[pl.ds(dst, SUBLANES), :],
                                     ssem.at[s])

    def chunk(c):
        return pl.ds(c, tm, stride=SUBLANES)

    wait_group = 16

    def gather_wait(tile, s):
        def body(it, c):
            for u in range(wait_group):
                gather_row(tile, it * wait_group + u, s).wait()
            return c
        lax.fori_loop(0, tm // wait_group, body, 0)

    def scatter_wait(phase, s):
        def body(it, c):
            for u in range(wait_group):
                scatter_row(phase, it * wait_group + u, s).wait()
            return c
        lax.fori_loop(0, tm // wait_group, body, 0)

    @pl.when(f == 0)
    def _():
        @pl.when(m == 0)
        def _():
            obuf[...] = jnp.zeros_like(obuf)

            def first(r, c):
                gather_row(0, r, 0).start()
                return c
            lax.fori_loop(0, tm, first, 0, unroll=8)

        @pl.when(m >= 1)
        def _():
            scatter_wait(m - 1, slot)
        gather_wait(m, slot)
        for r in range(in_steps, tm):
            gather_row(m + 1, r, other).start(priority=r % 2)
            scatter_row(m, r, other).start(priority=r % 2)
        n_chunks = h_scr.shape[1] // LANES
        xs = [xbuf[slot, chunk(c), :] for c in range(n_chunks)]
        ssq = xs[0] * xs[0]
        for xc in xs[1:]:
            ssq = ssq + xc * xc
        inv = lax.rsqrt(jnp.sum(ssq, axis=-1, keepdims=True) / h_scr.shape[1] + EPS)
        for c in range(n_chunks):
            lanes = slice(c * LANES, (c + 1) * LANES)
            h_scr[:, lanes] = (xs[c] * inv * g_ref[:, lanes]).astype(BF16)
        acc_scr[...] = jnp.zeros_like(acc_scr)

    def move_rows():
        r0 = pl.multiple_of(f * per_step, 8)
        for rr in range(per_step):
            r = r0 + rr
            gather_row(m + 1, r, other).start(priority=rr % 2)
            scatter_row(m, r, other).start(priority=rr % 2)

    @pl.when(valid)
    def _():
        move_rows()
        h = h_scr[...]
        a = _silu(_dot(h, wg_ref[...].astype(BF16))) * _dot(h, wu_ref[...].astype(BF16))
        acc_scr[...] += _dot(a.astype(BF16), wd_ref[...].astype(BF16))

    @pl.when(jnp.logical_not(valid))
    def _():
        move_rows()

    @pl.when(f == nf - 1)
    def _():
        for c in range(acc_scr.shape[1] // LANES):
            obuf[slot, chunk(c), :] = jnp.where(valid, acc_scr[:, c * LANES:(c + 1) * LANES], 0.0)

        @pl.when(m == last_phase)
        def _():
            scatter_wait(m, other)
            gather_wait(m + 1, other)


def _moe_experts(x_rows, gain, wg, wu, wd, tok, dst, tile_expert, tile_valid, n_rows, *, tm, tf):
    d = gain.shape[1]
    assert d == SUBLANES * LANES
    nf = wg.shape[2] // tf
    n_phases = tile_expert.shape[0]
    kern = functools.partial(_moe_kernel, tm=tm, nf=nf)

    def fsel(m, f, tval):
        return jnp.where(tval[m] > 0, f, nf - 1)

    grid_spec = pltpu.PrefetchScalarGridSpec(
        num_scalar_prefetch=4,
        grid=(n_phases, nf),
        in_specs=[
            pl.BlockSpec(memory_space=pl.ANY),
            pl.BlockSpec((1, d), lambda m, f, tok, dst, texp, tval: (0, 0)),
            pl.BlockSpec((None, d, tf),
                         lambda m, f, tok, dst, texp, tval: (texp[m], 0, fsel(m, f, tval))),
            pl.BlockSpec((None, d, tf),
                         lambda m, f, tok, dst, texp, tval: (texp[m], 0, fsel(m, f, tval))),
            pl.BlockSpec((None, tf, d),
                         lambda m, f, tok, dst, texp, tval: (texp[m], fsel(m, f, tval), 0)),
        ],
        out_specs=pl.BlockSpec(memory_space=pl.ANY),
        scratch_shapes=[
            pltpu.VMEM((2, tm * SUBLANES, LANES), F32),
            pltpu.VMEM((2, tm * SUBLANES, LANES), F32),
            pltpu.VMEM((tm, d), BF16),
            pltpu.VMEM((tm, d), F32),
            pltpu.SemaphoreType.DMA((2,)),
            pltpu.SemaphoreType.DMA((2,)),
        ],
    )
    return pl.pallas_call(
        kern,
        grid_spec=grid_spec,
        out_shape=jax.ShapeDtypeStruct((n_rows * SUBLANES, LANES), F32),
        compiler_params=_params("arbitrary", "arbitrary"),
        name="moe_experts",
    )(tok, dst, tile_expert, tile_valid, x_rows, gain, wg, wu, wd)


def _combine_kernel(x_ref, r_ref, y0_ref, y1_ref, g_ref, o_ref, *, out_norm):
    tm, d = x_ref.shape
    route = r_ref[...]
    w0 = route[:, TOP_K:TOP_K + 1]
    w1 = route[:, TOP_K + 1:TOP_K + 2]
    for c in range(d // LANES):
        lanes = slice(c * LANES, (c + 1) * LANES)
        rows = pl.ds(c, tm, stride=SUBLANES)
        o_ref[:, lanes] = x_ref[:, lanes] + w0 * y0_ref[rows, :] + w1 * y1_ref[rows, :]
    if out_norm:
        o_ref[...] = _rms(o_ref[...], g_ref[...])


def _moe_combine(x, route, y_rows, out_gain, *, tm=512):
    t, d = x.shape
    assert TOP_K == 2
    out_norm = out_gain is not None
    gain = out_gain if out_norm else jnp.ones((1, d), F32)
    return pl.pallas_call(
        functools.partial(_combine_kernel, out_norm=out_norm),
        grid=(t // tm,),
        in_specs=[
            pl.BlockSpec((tm, d), lambda i: (i, 0)),
            pl.BlockSpec((tm, LANES), lambda i: (i, 0)),
            pl.BlockSpec((tm * SUBLANES, LANES), lambda i: (i, 0)),
            pl.BlockSpec((tm * SUBLANES, LANES), lambda i: (t // tm + i, 0)),
            pl.BlockSpec((1, d), lambda i: (0, 0)),
        ],
        out_specs=pl.BlockSpec((tm, d), lambda i: (i, 0)),
        out_shape=jax.ShapeDtypeStruct((t, d), F32),
        compiler_params=_params("parallel"),
        name="moe_combine",
    )(x, route, y_rows, y_rows, gain)


def _moe(x, x_rows, gain, w_router, wg, wu, wd, out_gain=None, *, tm=MOE_TM, tf=MOE_TF):
    t, d = x.shape
    wr = jnp.zeros((d, LANES), F32).at[:, :N_EXPERTS].set(w_router)
    route = _router(x, gain, wr)
    tok, dst, tile_expert, tile_valid, n_rows = _route_metadata(route[:, :TOP_K].astype(I32), tm)
    y_rows = _moe_experts(x_rows, gain, wg, wu, wd, tok * SUBLANES, dst * SUBLANES,
                          tile_expert, tile_valid, n_rows, tm=tm, tf=tf)
    return _moe_combine(x, route, y_rows, out_gain)


def _route_metadata(experts, tm):
    t = experts.shape[0]
    n_slots = TOP_K * t
    eflat = experts.T.reshape(-1)
    onehot = (eflat[:, None] == jnp.arange(N_EXPERTS, dtype=I32)[None, :]).astype(I32)
    csum = jnp.cumsum(onehot, axis=0)
    rank = jnp.sum((csum - 1) * onehot, axis=1)
    counts = csum[-1]
    tiles_per = (counts + tm - 1) // tm
    tile_end = jnp.cumsum(tiles_per)
    start = (tile_end - tiles_per) * tm
    pos = (jnp.sum(start[None, :] * onehot, axis=1) + rank).astype(I32)
    n_tiles = n_slots // tm + N_EXPERTS
    n_rows_pad = n_tiles * tm
    slot_of_row = jnp.full((n_rows_pad,), -1, I32).at[pos].set(jnp.arange(n_slots, dtype=I32))
    real = slot_of_row >= 0
    tok = jnp.concatenate([jnp.where(real, slot_of_row % t, 0), jnp.zeros((2 * tm,), I32)])
    real_ext = jnp.concatenate([jnp.zeros((tm,), bool), real])
    dump = n_slots + jnp.cumsum(jnp.logical_not(real_ext).astype(I32)) - 1
    dst = jnp.where(real_ext, jnp.concatenate([jnp.zeros((tm,), I32), slot_of_row]), dump)
    tile_id = jnp.arange(n_tiles + 1, dtype=I32)
    tile_expert = jnp.sum((tile_id[:, None] >= tile_end[None, :]).astype(I32), axis=1)
    tile_valid = (tile_id < tile_end[-1]).astype(I32)
    last_expert = jnp.max(jnp.where(counts > 0, jnp.arange(N_EXPERTS, dtype=I32), 0))
    tile_expert = jnp.where(tile_valid > 0, tile_expert, last_expert).astype(I32)
    return tok, dst.astype(I32), tile_expert, tile_valid, n_rows_pad + tm


def _final_norm_kernel(x_ref, g_ref, o_ref):
    o_ref[...] = _rms(x_ref[...], g_ref[...])


def _final_norm(x, gain, *, tm=1024):
    t, d = x.shape
    return pl.pallas_call(
        _final_norm_kernel,
        grid=(t // tm,),
        in_specs=[pl.BlockSpec((tm, d), lambda i: (i, 0)), pl.BlockSpec((1, d), lambda i: (0, 0))],
        out_specs=pl.BlockSpec((tm, d), lambda i: (i, 0)),
        out_shape=jax.ShapeDtypeStruct((t, d), F32),
        compiler_params=_params("parallel"),
        name="final_norm",
    )(x, gain)


def _rope_tables(seq):
    half = HEAD_DIM // 2
    inv_freq = ROPE_THETA ** (-jnp.arange(half, dtype=F32) / half)
    ang = jnp.arange(seq, dtype=F32)[:, None] * inv_freq[None, :]
    cos, sin = jnp.cos(ang), jnp.sin(ang)
    reps = LANES // HEAD_DIM
    cos_t = jnp.tile(jnp.concatenate([cos, cos], axis=1), (1, reps))
    sin_t = jnp.tile(jnp.concatenate([-sin, sin], axis=1), (1, reps))
    return cos_t, sin_t


def _in_proj_weight(w_in):
    o = 0
    offs = {}
    for name, width in (("qa", A_W), ("ka", A_W), ("va", A_W), ("qb", B_QW), ("kb", B_KVW),
                        ("vb", B_KVW), ("qc", C_W), ("kc", C_W), ("vc", C_W), ("g", 3 * D_MODEL)):
        offs[name] = (o, o + width)
        o += width
    scale = HEAD_DIM ** -0.5
    log2e = 1.4426950408889634
    q_scale = {"qa": scale * log2e, "qb": scale, "qc": scale * log2e}
    parts = []
    for name in ("g", "qa", "ka", "qb", "kb", "vb", "va", "qc", "kc", "vc"):
        lo, hi = offs[name]
        blk = w_in[:, lo:hi]
        if name in q_scale:
            blk = blk * q_scale[name]
        parts.append(blk)
    return jnp.concatenate(parts, axis=1).astype(BF16)


def _rope_group_map():
    groups = {}
    for t in range(IN_W // PROJ_TN):
        lo, hi = t * PROJ_TN, (t + 1) * PROJ_TN
        n = (min(hi, ROPE_HI) - max(lo, ROPE_LO)) // LANES
        if n > 0:
            assert max(lo, ROPE_LO) == lo
            groups[t] = n
    return groups


def kernel(x, mem, norm_mix, w_in, w_proj_a, w_proj_b, w_proj_c, w_mix_out, sinks, norm_cross,
           norm_mem, w_xq, w_xkv, w_xo, norm_ffn, ffn_gate, ffn_up, ffn_down, moe_router,
           moe_gate, moe_up, moe_down, final_norm):
    batch, seq, d = x.shape
    depth = norm_mix.shape[0]
    assert d == D_MODEL and seq % 1024 == 0 and mem.shape[1] == MEM_LEN
    t = batch * seq
    xf = x.reshape(t, d)
    memf = mem.reshape(batch * MEM_LEN, d)
    cos_t, sin_t = _rope_tables(seq)
    rope_groups = _rope_group_map()
    ones_tab = jnp.ones((MEM_LEN, LANES), F32)

    for l in range(depth):
        gain = lambda g: g[l].reshape(1, d)
        p = _norm_proj(xf, gain(norm_mix), _in_proj_weight(w_in[l]), cos_t, sin_t, seq,
                       tm=1024, tn=PROJ_TN, rope_groups=rope_groups)
        oa = _moba(p, batch, seq)
        ob = _swa(p, sinks[l], batch, seq)
        oc = _stick(p, batch, seq)
        xf = _merge(xf, oa, ob, oc, p, w_proj_a[l].astype(BF16), w_proj_b[l].astype(BF16),
                    w_proj_c[l].astype(BF16), w_mix_out[l].astype(BF16))
        kv = _norm_proj(memf, gain(norm_mem), w_xkv[l].astype(BF16), ones_tab, ones_tab, MEM_LEN,
                        tm=MEM_LEN, tn=2 * X_W, rope_groups={})
        cross_args = (gain(norm_cross), w_xq[l].astype(BF16), kv, w_xo[l].astype(BF16), seq)
        if l % 2 == 0:
            i = l // 2
            xf = _cross(xf, *cross_args)
            xf = _ffn(xf, gain(norm_ffn), ffn_gate[i].astype(BF16), ffn_up[i].astype(BF16),
                      ffn_down[i].astype(BF16), tf=FFN_TF)
        else:
            i = l // 2
            xf, x_rows = _cross(xf, *cross_args, with_row_tiles=True)
            out_gain = final_norm.reshape(1, d) if l == depth - 1 else None
            xf = _moe(xf, x_rows, gain(norm_ffn), moe_router[i], moe_gate[i], moe_up[i],
                      moe_down[i], out_gain)
    if depth % 2 == 1:
        xf = _final_norm(xf, final_norm.reshape(1, d))
    return xf.reshape(batch, seq, d)
```

```python
import functools

import jax
import jax.numpy as jnp
from jax import lax
from jax.experimental import pallas as pl
from jax.experimental.pallas import tpu as pltpu

F32 = jnp.float32
BF16 = jnp.bfloat16
I32 = jnp.int32

D_MODEL = 1024
HEAD_DIM = 64
A_HEADS = 4
MOBA_BLOCK = 256
MOBA_TOPK = 3
B_HEADS = 8
B_KV_HEADS = 2
WINDOW = 128
C_HEADS = 4
SB_BLOCK = 256
MEM_LEN = 256
X_HEADS = 4
X_HEAD_DIM = 128
N_EXPERTS = 8
TOP_K = 2
ROPE_THETA = 10000.0
EPS = 1e-6

A_W = A_HEADS * HEAD_DIM
B_QW = B_HEADS * HEAD_DIM
B_KVW = B_KV_HEADS * HEAD_DIM
C_W = C_HEADS * HEAD_DIM
X_W = X_HEADS * X_HEAD_DIM
QKV_W = 3 * A_W + B_QW + 2 * B_KVW + 3 * C_W
IN_W = QKV_W + 3 * D_MODEL

LANES = 128
SUBLANES = 8
NEG = -1e30
LOG2E = 1.4426950408889634

COL_GA, COL_GB, COL_GC = 0, D_MODEL, 2 * D_MODEL
COL_QA = 3 * D_MODEL
COL_KA = COL_QA + A_W
COL_QB = COL_KA + A_W
COL_KB = COL_QB + B_QW
COL_VB = COL_KB + B_KVW
COL_VA = COL_VB + B_KVW
COL_QC = COL_VA + A_W
COL_KC = COL_QC + C_W
COL_VC = COL_KC + C_W
ROPE_LO, ROPE_HI = COL_QA, COL_VB

PROJ_TN = 768
FFN_TF = 256
MOE_TF = 512
MOE_TM = 1072
VMEM_LIMIT = 48 * 1024 * 1024


def _params(*sem):
    return pltpu.CompilerParams(dimension_semantics=sem, vmem_limit_bytes=VMEM_LIMIT)


def _rms(x, g):
    ms = jnp.mean(x * x, axis=-1, keepdims=True)
    return x * lax.rsqrt(ms + EPS) * g


def _dot(a, b):
    return jnp.dot(a, b, preferred_element_type=F32)


def _dot_t(a, b):
    return lax.dot_general(a, b, (((1,), (1,)), ((), ())), preferred_element_type=F32)


def _split_bf16(v):
    hi = v.astype(BF16)
    lo = (v - hi.astype(F32)).astype(BF16)
    return hi, lo


def _norm_proj_kernel(x_ref, g_ref, w_ref, cos_ref, sin_ref, o_ref, h_scr, *, rope_groups):
    j = pl.program_id(1)

    @pl.when(j == 0)
    def _():
        h_scr[...] = _rms(x_ref[...], g_ref[...]).astype(BF16)

    acc = _dot(h_scr[...], w_ref[...])
    n_groups = acc.shape[1] // LANES

    def rope(y):
        lane = lax.broadcasted_iota(I32, y.shape, 1)
        first_half = (lane % HEAD_DIM) < (HEAD_DIM // 2)
        sw = jnp.where(first_half, pltpu.roll(y, LANES - HEAD_DIM // 2, 1),
                       pltpu.roll(y, HEAD_DIM // 2, 1))
        return y * cos_ref[...] + sw * sin_ref[...]

    roped = sorted(rope_groups)

    for t in roped:
        @pl.when(j == t)
        def _(t=t):
            for gi in range(n_groups):
                y = acc[:, gi * LANES:(gi + 1) * LANES]
                if gi < rope_groups[t]:
                    y = rope(y)
                o_ref[:, gi * LANES:(gi + 1) * LANES] = y.astype(o_ref.dtype)

    is_plain = j >= 0
    for t in roped:
        is_plain = jnp.logical_and(is_plain, j != t)

    @pl.when(is_plain)
    def _():
        o_ref[...] = acc.astype(o_ref.dtype)


def _norm_proj(x, gain, w, cos, sin, seq, *, tm, tn, rope_groups):
    t, d = x.shape
    n = w.shape[1]
    pos_blocks = seq // tm
    kern = functools.partial(_norm_proj_kernel, rope_groups=rope_groups)
    return pl.pallas_call(
        kern,
        grid=(t // tm, n // tn),
        in_specs=[
            pl.BlockSpec((tm, d), lambda i, j: (i, 0)),
            pl.BlockSpec((1, d), lambda i, j: (0, 0)),
            pl.BlockSpec((d, tn), lambda i, j: (0, j)),
            pl.BlockSpec((tm, LANES), lambda i, j: (i % pos_blocks, 0)),
            pl.BlockSpec((tm, LANES), lambda i, j: (i % pos_blocks, 0)),
        ],
        out_specs=pl.BlockSpec((tm, tn), lambda i, j: (i, j)),
        out_shape=jax.ShapeDtypeStruct((t, n), BF16),
        scratch_shapes=[pltpu.VMEM((tm, d), BF16)],
        compiler_params=_params("parallel", "arbitrary"),
        name="norm_proj",
    )(x, gain, w, cos, sin)


MOBA_VROWS = HEAD_DIM + 16


def _moba_kernel(qt_ref, k_ref, vt_ref, o_ref, kmean_scr, kaug_scr, vaug_scr, qaug_scr, *, nblk):
    i = pl.program_id(1)
    blk = MOBA_BLOCK
    hd = HEAD_DIM
    aug = 2 * hd
    vr = MOBA_VROWS
    heads = range(A_HEADS)
    hsl = [slice(hh * hd, (hh + 1) * hd) for hh in heads]

    @pl.when(i == 0)
    def _():
        ones = jnp.ones((vr - hd, blk), BF16)
        blk_lane = lax.broadcasted_iota(I32, (blk, hd), 1)
        for n in range(nblk):
            rows = slice(n * blk, (n + 1) * blk)
            kmean_scr[n:n + 1, :] = jnp.mean(k_ref[rows, :].astype(F32), axis=0, keepdims=True)
            onehot = jnp.where(blk_lane == n, 1.0, 0.0).astype(BF16)
            for hh in heads:
                kaug_scr[rows, hh * aug:hh * aug + hd] = k_ref[rows, hsl[hh]]
                kaug_scr[rows, hh * aug + hd:(hh + 1) * aug] = onehot
                vaug_scr[n, hh * vr:hh * vr + hd, :] = vt_ref[n, hsl[hh], :]
                vaug_scr[n, hh * vr + hd:(hh + 1) * vr, :] = ones

    key = lax.broadcasted_iota(I32, (blk, blk), 0)
    qry = lax.broadcasted_iota(I32, (blk, blk), 1)
    causal = key <= qry
    blk_id = lax.broadcasted_iota(I32, (nblk, blk), 0)
    own0 = pl.multiple_of(i * blk, blk)

    init = []
    for hh in heads:
        qt = qt_ref[hsl[hh], :]
        km_hi, km_lo = _split_bf16(kmean_scr[:, hsl[hh]])
        g = _dot(km_hi, qt) + _dot(km_lo, qt)

        cnt = jnp.zeros((nblk, blk), I32)
        for m in range(nblk):
            gm = g[m:m + 1, :]
            beats = (gm > g) | ((gm == g) & (m < blk_id))
            cnt = cnt + jnp.where(beats, (m < i).astype(I32), 0)
        sel = (cnt < MOBA_TOPK) & (blk_id < i)
        qaug_scr[hh * aug:hh * aug + hd, :] = qt
        qaug_scr[hh * aug + hd:hh * aug + hd + nblk, :] = jnp.where(sel, 0.0, NEG).astype(BF16)
        qaug_scr[hh * aug + hd + nblk:(hh + 1) * aug, :] = jnp.zeros((hd - nblk, blk), BF16)

        s = _dot(k_ref[pl.ds(own0, blk), hsl[hh]], qt)
        s = jnp.where(causal, s, NEG)
        m0 = jnp.max(s, axis=0, keepdims=True)
        p = jnp.exp2(s - m0)
        init.append((m0, _dot(vaug_scr[i, hh * vr:(hh + 1) * vr, :], p.astype(BF16))))

    def step(blocks, carry):
        rows = [pl.ds(pl.multiple_of(j * blk, blk), blk) for j in blocks]
        ss = [[_dot(kaug_scr[r, hh * aug:(hh + 1) * aug], qaug_scr[hh * aug:(hh + 1) * aug, :])
               for r in rows] for hh in heads]
        ms = []
        for hh in heads:
            m_new = carry[hh][0]
            for s in ss[hh]:
                m_new = jnp.maximum(m_new, jnp.max(s, axis=0, keepdims=True))
            ms.append(m_new)
        ps = [[jnp.exp2(s - ms[hh]).astype(BF16) for s in ss[hh]] for hh in heads]
        pvs = []
        for hh in heads:
            pv = None
            for j, p in zip(blocks, ps[hh]):
                d = _dot(vaug_scr[j, hh * vr:(hh + 1) * vr, :], p)
                pv = d if pv is None else pv + d
            pvs.append(pv)
        return tuple((ms[hh], jnp.exp2(carry[hh][0] - ms[hh]) * carry[hh][1] + pvs[hh])
                     for hh in heads)

    fin = lax.fori_loop(0, i // 2, lambda jp, c: step([2 * jp, 2 * jp + 1], c), tuple(init))
    fin = lax.fori_loop(0, i % 2, lambda _, c: step([i - 1], c), fin)
    out_t = jnp.concatenate([acc[:hd, :] / acc[hd:hd + 1, :] for _, acc in fin], axis=0)
    o_ref[...] = out_t.T.astype(o_ref.dtype)


def _moba(p, batch, seq):
    nblk = seq // MOBA_BLOCK
    assert nblk <= HEAD_DIM
    qt = _blocks_t(p, COL_QA, A_W, batch, seq, MOBA_BLOCK)
    vt = _blocks_t(p, COL_VA, A_W, batch, seq, MOBA_BLOCK)
    kern = functools.partial(_moba_kernel, nblk=nblk)
    return pl.pallas_call(
        kern,
        grid=(batch, nblk),
        in_specs=[
            pl.BlockSpec((None, None, A_W, MOBA_BLOCK), lambda b, i: (b, i, 0, 0)),
            pl.BlockSpec((seq, A_W), lambda b, i: (b, COL_KA // A_W)),
            pl.BlockSpec((None, nblk, A_W, MOBA_BLOCK), lambda b, i: (b, 0, 0, 0)),
        ],
        out_specs=pl.BlockSpec((MOBA_BLOCK, A_W), lambda b, i: (b * nblk + i, 0)),
        out_shape=jax.ShapeDtypeStruct((batch * seq, A_W), BF16),
        scratch_shapes=[
            pltpu.VMEM((nblk, A_W), F32),
            pltpu.VMEM((seq, 2 * A_W), BF16),
            pltpu.VMEM((nblk, A_HEADS * MOBA_VROWS, MOBA_BLOCK), BF16),
            pltpu.VMEM((2 * A_W, MOBA_BLOCK), BF16),
        ],
        compiler_params=_params("parallel", "arbitrary"),
        name="moba",
    )(qt, p, vt)


def _swa_kernel(sinks_ref, q_ref, kv_ref, pkv_ref, o_ref, kv_scr, *, tq):
    i = pl.program_id(1)
    w = WINDOW
    kv_scr[0:w, :] = pkv_ref[...]
    kv_scr[w:, :] = kv_ref[...]

    row = lax.broadcasted_iota(I32, (w, 2 * w), 0)
    col = lax.broadcasted_iota(I32, (w, 2 * w), 1)
    rel = row + w - col
    band = (rel >= 0) & (rel < w)
    group = B_HEADS // B_KV_HEADS

    ones = jnp.ones((2 * w, HEAD_DIM), BF16)

    def body(n, carry):
        base = pl.multiple_of(n * w, w)
        kvt = kv_scr[pl.ds(base, 2 * w), :]
        qt = q_ref[pl.ds(base, w), :]
        col_min = jnp.where(jnp.logical_and(i == 0, n == 0), w, 0)
        mask = band & (col >= col_min)
        heads = [(kh, gi) for kh in range(B_KV_HEADS) for gi in range(group)]
        hcol = lambda hd: slice(hd * HEAD_DIM, (hd + 1) * HEAD_DIM)
        ss = []
        for kh in range(B_KV_HEADS):
            q4 = jnp.concatenate([qt[:, hcol(kh * group + gi)] for gi in range(group)], axis=0)
            ss.append(_dot_t(q4, kvt[:, hcol(kh)]))
        ms, ps = {}, {}
        for kh, gi in heads:
            s = jnp.where(mask, ss[kh][gi * w:(gi + 1) * w], NEG)
            m = jnp.maximum(jnp.max(s, axis=-1, keepdims=True), sinks_ref[kh * group + gi])
            ms[(kh, gi)] = m
            ps[(kh, gi)] = jnp.exp2(s - m).astype(BF16)
        pvs = []
        for kh in range(B_KV_HEADS):
            vaug = jnp.concatenate([kvt[:, B_KVW + kh * HEAD_DIM:B_KVW + (kh + 1) * HEAD_DIM], ones],
                                   axis=1)
            p4 = jnp.concatenate([ps[(kh, gi)] for gi in range(group)], axis=0)
            pvs.append(_dot(p4, vaug))
        outs = []
        for kh, gi in heads:
            pv = pvs[kh][gi * w:(gi + 1) * w]
            l = pv[:, HEAD_DIM:] + jnp.exp2(sinks_ref[kh * group + gi] - ms[(kh, gi)])
            outs.append(pv[:, :HEAD_DIM] / l)
        o_ref[pl.ds(base, w), :] = jnp.concatenate(outs, axis=1).astype(o_ref.dtype)
        return carry

    lax.fori_loop(0, tq // w, body, 0)


def _swa(p, sinks, batch, seq, *, tq=512):
    nq = seq // tq
    sub = tq // WINDOW
    kern = functools.partial(_swa_kernel, tq=tq)
    return pl.pallas_call(
        kern,
        grid=(batch, nq),
        in_specs=[
            pl.BlockSpec(memory_space=pltpu.SMEM),
            pl.BlockSpec((tq, B_QW), lambda b, i: (b * nq + i, COL_QB // B_QW)),
            pl.BlockSpec((tq, 2 * B_KVW), lambda b, i: (b * nq + i, COL_KB // (2 * B_KVW))),
            pl.BlockSpec((WINDOW, 2 * B_KVW),
                         lambda b, i: (jnp.maximum((b * nq + i) * sub - 1, 0),
                                       COL_KB // (2 * B_KVW))),
        ],
        out_specs=pl.BlockSpec((tq, B_QW), lambda b, i: (b * nq + i, 0)),
        out_shape=jax.ShapeDtypeStruct((batch * seq, B_QW), BF16),
        scratch_shapes=[pltpu.VMEM((tq + WINDOW, 2 * B_KVW), BF16)],
        compiler_params=_params("parallel", "parallel"),
        name="swa",
    )(sinks, p, p, p)


def _stick_kernel(qt_ref, k_ref, vt_ref, o_ref):
    i = pl.program_id(1)
    blk = SB_BLOCK
    hd = HEAD_DIM
    heads = range(C_HEADS)
    hsl = [slice(hh * hd, (hh + 1) * hd) for hh in heads]
    key = lax.broadcasted_iota(I32, (blk, blk), 0)
    qry = lax.broadcasted_iota(I32, (blk, blk), 1)
    causal = key < qry
    ntri = jnp.where(qry >= key, -1.0, 0.0).astype(BF16)
    own0 = pl.multiple_of(i * blk, blk)

    def softplus2(z):
        return jnp.maximum(z, 0.0) + jnp.log2(1.0 + jnp.exp2(-jnp.abs(z)))

    def suffix(sp):
        incl = _dot(ntri, sp.astype(BF16))
        return incl, incl[0:1, :]

    qts = [qt_ref[hsl[hh], :] for hh in heads]
    init = []
    for hh in heads:
        z = _dot(k_ref[pl.ds(own0, blk), hsl[hh]], qts[hh])
        incl, carry0 = suffix(jnp.where(causal, softplus2(z), 0.0))
        wgt = jnp.where(causal, jnp.exp2(z + incl), 0.0)
        init.append((carry0, _dot(vt_ref[i, hsl[hh], :], wgt.astype(BF16))))

    def step(blocks, state):
        rows = [pl.ds(j * blk if isinstance(j, int) else pl.multiple_of(j * blk, blk), blk)
                for j in blocks]
        pairs = [(hh, b) for hh in heads for b in range(len(blocks))]
        zs = {(hh, b): _dot(k_ref[rows[b], hsl[hh]], qts[hh]) for hh, b in pairs}
        sfx = {k: suffix(softplus2(zs[k])) for k in pairs}
        ws = {k: jnp.exp2(zs[k] + sfx[k][0]).astype(BF16) for k in pairs}
        pvs = {(hh, b): _dot(vt_ref[blocks[b], hsl[hh], :], ws[(hh, b)]) for hh, b in pairs}
        out = []
        for hh in heads:
            carry, acc = state[hh]
            for b in range(len(blocks)):
                acc = acc + pvs[(hh, b)] * jnp.exp2(carry)
                carry = carry + sfx[(hh, b)][1]
            out.append((carry, acc))
        return tuple(out)

    fin = lax.fori_loop(0, i // 2, lambda jp, s: step([i - 1 - 2 * jp, i - 2 - 2 * jp], s),
                        tuple(init))
    fin = lax.fori_loop(0, i % 2, lambda _, s: step([0], s), fin)
    out_t = jnp.concatenate([acc for _, acc in fin], axis=0)
    o_ref[...] = out_t.T.astype(o_ref.dtype)


def _blocks_t(p, col, width, batch, seq, blk):
    t = p[:, col:col + width].reshape(batch, seq // blk, blk, width)
    return jnp.swapaxes(t, 2, 3)


def _stick(p, batch, seq):
    nblk = seq // SB_BLOCK
    qt = _blocks_t(p, COL_QC, C_W, batch, seq, SB_BLOCK)
    vt = _blocks_t(p, COL_VC, C_W, batch, seq, SB_BLOCK)
    return pl.pallas_call(
        _stick_kernel,
        grid=(batch, nblk),
        in_specs=[
            pl.BlockSpec((None, None, C_W, SB_BLOCK), lambda b, i: (b, i, 0, 0)),
            pl.BlockSpec((seq, C_W), lambda b, i: (b, COL_KC // C_W)),
            pl.BlockSpec((None, nblk, C_W, SB_BLOCK), lambda b, i: (b, 0, 0, 0)),
        ],
        out_specs=pl.BlockSpec((SB_BLOCK, C_W), lambda b, i: (b * nblk + i, 0)),
        out_shape=jax.ShapeDtypeStruct((batch * seq, C_W), BF16),
        compiler_params=_params("parallel", "parallel"),
        name="stick",
    )(qt, p, vt)


def _sigmoid(x):
    return 1.0 / (1.0 + jnp.exp(-x))


def _merge_kernel(x_ref, oa_ref, ob_ref, oc_ref, ga_ref, gb_ref, gc_ref,
                  wa_ref, wb_ref, wc_ref, wo_ref, o_ref):
    mixed = _sigmoid(ga_ref[...].astype(F32)) * _dot(oa_ref[...], wa_ref[...])
    mixed = mixed + _sigmoid(gb_ref[...].astype(F32)) * _dot(ob_ref[...], wb_ref[...])
    mixed = mixed + _sigmoid(gc_ref[...].astype(F32)) * _dot(oc_ref[...], wc_ref[...])
    o_ref[...] = x_ref[...] + _dot(mixed.astype(BF16), wo_ref[...])


def _merge(x, oa, ob, oc, p, wa, wb, wc, wo, *, tm=512):
    t, d = x.shape
    full = lambda a: pl.BlockSpec(a.shape, lambda i: (0, 0))
    return pl.pallas_call(
        _merge_kernel,
        grid=(t // tm,),
        in_specs=[
            pl.BlockSpec((tm, d), lambda i: (i, 0)),
            pl.BlockSpec((tm, A_W), lambda i: (i, 0)),
            pl.BlockSpec((tm, B_QW), lambda i: (i, 0)),
            pl.BlockSpec((tm, C_W), lambda i: (i, 0)),
            pl.BlockSpec((tm, d), lambda i: (i, COL_GA // D_MODEL)),
            pl.BlockSpec((tm, d), lambda i: (i, COL_GB // D_MODEL)),
            pl.BlockSpec((tm, d), lambda i: (i, COL_GC // D_MODEL)),
            full(wa), full(wb), full(wc), full(wo),
        ],
        out_specs=pl.BlockSpec((tm, d), lambda i: (i, 0)),
        out_shape=jax.ShapeDtypeStruct((t, d), F32),
        compiler_params=_params("parallel"),
        name="merge",
    )(x, oa, ob, oc, p, p, p, wa, wb, wc, wo)


def _cross_kernel(x_ref, g_ref, wq_ref, kv_ref, wo_ref, o_ref, *rows_ref):
    x = x_ref[...]
    h = _rms(x, g_ref[...]).astype(BF16)
    q = (_dot(h, wq_ref[...]) * (X_HEAD_DIM ** -0.5)).astype(BF16)
    outs = []
    for hd in range(X_HEADS):
        hs = slice(hd * X_HEAD_DIM, (hd + 1) * X_HEAD_DIM)
        s = _dot_t(q[:, hs], kv_ref[:, hs])
        m = jnp.max(s, axis=-1, keepdims=True)
        p = jnp.exp(s - m)
        l = jnp.sum(p, axis=-1, keepdims=True)
        vs = slice(X_W + hd * X_HEAD_DIM, X_W + (hd + 1) * X_HEAD_DIM)
        outs.append(_dot(p.astype(BF16), kv_ref[:, vs]) / l)
    att = jnp.concatenate(outs, axis=1).astype(BF16)
    o_ref[...] = x + _dot(att, wo_ref[...])
    if rows_ref:
        tm, d = o_ref.shape
        for c in range(d // LANES):
            rows_ref[0][pl.ds(c, tm, stride=SUBLANES), :] = o_ref[:, c * LANES:(c + 1) * LANES]


def _cross(x, gain, wq, kv, wo, seq, *, tm=512, with_row_tiles=False):
    t, d = x.shape
    per_batch = seq // tm
    full = lambda a: pl.BlockSpec(a.shape, lambda i: (0, 0))
    out_specs = [pl.BlockSpec((tm, d), lambda i: (i, 0))]
    out_shape = [jax.ShapeDtypeStruct((t, d), F32)]
    if with_row_tiles:
        out_specs.append(pl.BlockSpec((tm * SUBLANES, LANES), lambda i: (i, 0)))
        out_shape.append(jax.ShapeDtypeStruct((t * SUBLANES, LANES), F32))
    outs = pl.pallas_call(
        _cross_kernel,
        grid=(t // tm,),
        in_specs=[
            pl.BlockSpec((tm, d), lambda i: (i, 0)),
            full(gain), full(wq),
            pl.BlockSpec((MEM_LEN, 2 * X_W), lambda i: (i // per_batch, 0)),
            full(wo),
        ],
        out_specs=out_specs,
        out_shape=out_shape,
        compiler_params=_params("parallel"),
        name="cross",
    )(x, gain, wq, kv, wo)
    return outs if with_row_tiles else outs[0]


def _silu(x):
    return x * _sigmoid(x)


def _ffn_kernel(x_ref, g_ref, wg_ref, wu_ref, wd_ref, o_ref, h_scr, acc_scr):
    f = pl.program_id(1)

    @pl.when(f == 0)
    def _():
        h_scr[...] = _rms(x_ref[...], g_ref[...]).astype(BF16)
        acc_scr[...] = x_ref[...]

    h = h_scr[...]
    a = _silu(_dot(h, wg_ref[...])) * _dot(h, wu_ref[...])
    acc_scr[...] += _dot(a.astype(BF16), wd_ref[...])

    @pl.when(f == pl.num_programs(1) - 1)
    def _():
        o_ref[...] = acc_scr[...]


def _ffn(x, gain, wg, wu, wd, *, tm=1024, tf=256):
    t, d = x.shape
    ff = wd.shape[0]
    return pl.pallas_call(
        _ffn_kernel,
        grid=(t // tm, ff // tf),
        in_specs=[
            pl.BlockSpec((tm, d), lambda i, f: (i, 0)),
            pl.BlockSpec((1, d), lambda i, f: (0, 0)),
            pl.BlockSpec((d, tf), lambda i, f: (0, f)),
            pl.BlockSpec((d, tf), lambda i, f: (0, f)),
            pl.BlockSpec((tf, d), lambda i, f: (f, 0)),
        ],
        out_specs=pl.BlockSpec((tm, d), lambda i, f: (i, 0)),
        out_shape=jax.ShapeDtypeStruct((t, d), F32),
        scratch_shapes=[pltpu.VMEM((tm, d), BF16), pltpu.VMEM((tm, d), F32)],
        compiler_params=_params("parallel", "arbitrary"),
        name="ffn",
    )(x, gain, wg, wu, wd)


def _router_kernel(x_ref, g_ref, wr_ref, o_ref):
    h = _rms(x_ref[...], g_ref[...])
    h_hi, h_lo = _split_bf16(h)
    w_hi, w_lo = _split_bf16(wr_ref[...])
    logits = _dot(h_hi, w_hi) + (_dot(h_hi, w_lo) + _dot(h_lo, w_hi))
    lane = lax.broadcasted_iota(I32, logits.shape, 1)
    logits = jnp.where(lane < N_EXPERTS, logits, NEG)
    v1 = jnp.max(logits, axis=-1, keepdims=True)
    i1 = jnp.min(jnp.where(logits == v1, lane, LANES), axis=-1, keepdims=True)
    rest = jnp.where(lane == i1, NEG, logits)
    v2 = jnp.max(rest, axis=-1, keepdims=True)
    i2 = jnp.min(jnp.where(rest == v2, lane, LANES), axis=-1, keepdims=True)
    e = jnp.exp(v2 - v1)
    w1 = 1.0 / (1.0 + e)
    w2 = e / (1.0 + e)
    out = jnp.where(lane == 0, i1.astype(F32), 0.0)
    out = jnp.where(lane == 1, i2.astype(F32), out)
    out = jnp.where(lane == 2, w1, out)
    out = jnp.where(lane == 3, w2, out)
    o_ref[...] = out


def _router(x, gain, wr, *, tm=1024):
    t, d = x.shape
    return pl.pallas_call(
        _router_kernel,
        grid=(t // tm,),
        in_specs=[
            pl.BlockSpec((tm, d), lambda i: (i, 0)),
            pl.BlockSpec((1, d), lambda i: (0, 0)),
            pl.BlockSpec((d, LANES), lambda i: (0, 0)),
        ],
        out_specs=pl.BlockSpec((tm, LANES), lambda i: (i, 0)),
        out_shape=jax.ShapeDtypeStruct((t, LANES), F32),
        compiler_params=_params("parallel"),
        name="router",
    )(x, gain, wr)


def _moe_kernel(tok_ref, dst_ref, texp_ref, tval_ref, x_hbm, g_ref, wg_ref, wu_ref, wd_ref,
                y_hbm, xbuf, obuf, h_scr, acc_scr, gsem, ssem, *, tm, nf):
    m = pl.program_id(0)
    f = pl.program_id(1)
    last_phase = pl.num_programs(0) - 1
    slot = m % 2
    other = 1 - slot
    valid = tval_ref[m] > 0
    per_step = (tm // nf) // 8 * 8
    in_steps = per_step * nf

    def row_tile(r):
        return pl.ds(pl.multiple_of(r * SUBLANES, SUBLANES), SUBLANES)

    def gather_row(tile, r, s):
        tok = pl.multiple_of(tok_ref[tile * tm + r], SUBLANES)
        return pltpu.make_async_copy(x_hbm.at[pl.ds(tok, SUBLANES), :], xbuf.at[s, row_tile(r), :],
                                     gsem.at[s])

    def scatter_row(phase, r, s):
        dst = pl.multiple_of(dst_ref[phase * tm + r], SUBLANES)
        return pltpu.make_async_copy(obuf.at[s, row_tile(r), :], y_hbm.at[pl.ds(dst, SUBLANES), :],
                                     ssem.at[s])

    def chunk(c):
        return pl.ds(c, tm, stride=SUBLANES)

    wait_group = 16

    def gather_wait(tile, s):
        def body(it, c):
            for u in range(wait_group):
                gather_row(tile, it * wait_group + u, s).wait()
            return c
        lax.fori_loop(0, tm // wait_group, body, 0)

    def scatter_wait(phase, s):
        def body(it, c):
            for u in range(wait_group):
                scatter_row(phase, it * wait_group + u, s).wait()
            return c
        lax.fori_loop(0, tm // wait_group, body, 0)

    next_valid = tval_ref[m + 1] > 0
    prev_valid = jnp.logical_and(m >= 1, tval_ref[jnp.maximum(m - 1, 0)] > 0)
    prev2_valid = jnp.logical_and(m >= 2, tval_ref[jnp.maximum(m - 2, 0)] > 0)
    pad_only = jnp.logical_and(jnp.logical_not(valid), m < last_phase)

    def pad_fill():
        first = pl.multiple_of(dst_ref[(m + 1) * tm], SUBLANES)
        return pltpu.make_async_copy(obuf.at[slot], y_hbm.at[pl.ds(first, tm * SUBLANES), :],
                                     ssem.at[slot])

    @pl.when(jnp.logical_and(f == 0, prev2_valid))
    def _():
        scatter_wait(m - 1, slot)

    @pl.when(jnp.logical_and(f == 0, pad_only))
    def _():
        obuf[slot] = jnp.zeros(obuf.shape[1:], obuf.dtype)
        pad_fill().start()

    @pl.when(jnp.logical_and(f == 0, next_valid))
    def _():
        for r in range(in_steps, tm):
            gather_row(m + 1, r, other).start(priority=r % 2)

    @pl.when(jnp.logical_and(f == 0, prev_valid))
    def _():
        for r in range(in_steps, tm):
            scatter_row(m, r, other).start(priority=r % 2)

    @pl.when(jnp.logical_and(f == 0, valid))
    def _():
        @pl.when(m == 0)
        def _():
            def first(r, c):
                gather_row(0, r, 0).start()
                return c
            lax.fori_loop(0, tm, first, 0, unroll=8)

        gather_wait(m, slot)
        n_chunks = h_scr.shape[1] // LANES
        xs = [xbuf[slot, chunk(c), :] for c in range(n_chunks)]
        ssq = xs[0] * xs[0]
        for xc in xs[1:]:
            ssq = ssq + xc * xc
        inv = lax.rsqrt(jnp.sum(ssq, axis=-1, keepdims=True) / h_scr.shape[1] + EPS)
        for c in range(n_chunks):
            lanes = slice(c * LANES, (c + 1) * LANES)
            h_scr[:, lanes] = (xs[c] * inv * g_ref[:, lanes]).astype(BF16)
        acc_scr[...] = jnp.zeros_like(acc_scr)

    def expert_step():
        h = h_scr[...]
        a = _silu(_dot(h, wg_ref[...].astype(BF16))) * _dot(h, wu_ref[...].astype(BF16))
        acc_scr[...] += _dot(a.astype(BF16), wd_ref[...].astype(BF16))

    def gather_rows():
        r0 = pl.multiple_of(f * per_step, 8)
        for rr in range(per_step):
            gather_row(m + 1, r0 + rr, other).start(priority=rr % 2)

    def scatter_rows():
        r0 = pl.multiple_of(f * per_step, 8)
        for rr in range(per_step):
            scatter_row(m, r0 + rr, other).start(priority=rr % 2)

    interior = jnp.logical_and(valid, jnp.logical_and(next_valid, prev_valid))
    edge = jnp.logical_not(interior)

    @pl.when(interior)
    def _():
        gather_rows()
        scatter_rows()
        expert_step()

    @pl.when(jnp.logical_and(edge, valid))
    def _():
        expert_step()

    @pl.when(jnp.logical_and(edge, next_valid))
    def _():
        gather_rows()

    @pl.when(jnp.logical_and(edge, prev_valid))
    def _():
        scatter_rows()

    @pl.when(jnp.logical_and(f == nf - 1, valid))
    def _():
        for c in range(acc_scr.shape[1] // LANES):
            obuf[slot, chunk(c), :] = acc_scr[:, c * LANES:(c + 1) * LANES]

    @pl.when(jnp.logical_and(f == nf - 1, pad_only))
    def _():
        pad_fill().wait()

    @pl.when(jnp.logical_and(jnp.logical_and(f == nf - 1, m == last_phase), prev_valid))
    def _():
        scatter_wait(m, other)


def _moe_experts(x_rows, gain, wg, wu, wd, tok, dst, tile_expert, tile_valid, n_rows, *, tm, tf):
    d = gain.shape[1]
    assert d == SUBLANES * LANES
    nf = wg.shape[2] // tf
    n_phases = tile_expert.shape[0]
    kern = functools.partial(_moe_kernel, tm=tm, nf=nf)

    def fsel(m, f, tval):
        return jnp.where(tval[m] > 0, f, nf - 1)

    grid_spec = pltpu.PrefetchScalarGridSpec(
        num_scalar_prefetch=4,
        grid=(n_phases, nf),
        in_specs=[
            pl.BlockSpec(memory_space=pl.ANY),
            pl.BlockSpec((1, d), lambda m, f, tok, dst, texp, tval: (0, 0)),
            pl.BlockSpec((None, d, tf),
                         lambda m, f, tok, dst, texp, tval: (texp[m], 0, fsel(m, f, tval))),
            pl.BlockSpec((None, d, tf),
                         lambda m, f, tok, dst, texp, tval: (texp[m], 0, fsel(m, f, tval))),
            pl.BlockSpec((None, tf, d),
                         lambda m, f, tok, dst, texp, tval: (texp[m], fsel(m, f, tval), 0)),
        ],
        out_specs=pl.BlockSpec(memory_space=pl.ANY),
        scratch_shapes=[
            pltpu.VMEM((2, tm * SUBLANES, LANES), F32),
            pltpu.VMEM((2, tm * SUBLANES, LANES), F32),
            pltpu.VMEM((tm, d), BF16),
            pltpu.VMEM((tm, d), F32),
            pltpu.SemaphoreType.DMA((2,)),
            pltpu.SemaphoreType.DMA((2,)),
        ],
    )
    return pl.pallas_call(
        kern,
        grid_spec=grid_spec,
        out_shape=jax.ShapeDtypeStruct((n_rows * SUBLANES, LANES), F32),
        compiler_params=_params("arbitrary", "arbitrary"),
        name="moe_experts",
    )(tok, dst, tile_expert, tile_valid, x_rows, gain, wg, wu, wd)


def _combine_kernel(x_ref, r_ref, y0_ref, y1_ref, g_ref, o_ref, *, out_norm):
    tm, d = x_ref.shape
    route = r_ref[...]
    w0 = route[:, TOP_K:TOP_K + 1]
    w1 = route[:, TOP_K + 1:TOP_K + 2]
    for c in range(d // LANES):
        lanes = slice(c * LANES, (c + 1) * LANES)
        rows = pl.ds(c, tm, stride=SUBLANES)
        o_ref[:, lanes] = x_ref[:, lanes] + w0 * y0_ref[rows, :] + w1 * y1_ref[rows, :]
    if out_norm:
        o_ref[...] = _rms(o_ref[...], g_ref[...])


def _moe_combine(x, route, y_rows, out_gain, *, tm=512):
    t, d = x.shape
    assert TOP_K == 2
    out_norm = out_gain is not None
    gain = out_gain if out_norm else jnp.ones((1, d), F32)
    return pl.pallas_call(
        functools.partial(_combine_kernel, out_norm=out_norm),
        grid=(t // tm,),
        in_specs=[
            pl.BlockSpec((tm, d), lambda i: (i, 0)),
            pl.BlockSpec((tm, LANES), lambda i: (i, 0)),
            pl.BlockSpec((tm * SUBLANES, LANES), lambda i: (i, 0)),
            pl.BlockSpec((tm * SUBLANES, LANES), lambda i: (t // tm + i, 0)),
            pl.BlockSpec((1, d), lambda i: (0, 0)),
        ],
        out_specs=pl.BlockSpec((tm, d), lambda i: (i, 0)),
        out_shape=jax.ShapeDtypeStruct((t, d), F32),
        compiler_params=_params("parallel"),
        name="moe_combine",
    )(x, route, y_rows, y_rows, gain)


def _moe(x, x_rows, gain, w_router, wg, wu, wd, out_gain=None, *, tm=MOE_TM, tf=MOE_TF):
    t, d = x.shape
    wr = jnp.zeros((d, LANES), F32).at[:, :N_EXPERTS].set(w_router)
    route = _router(x, gain, wr)
    tok, dst, tile_expert, tile_valid, n_rows = _route_metadata(route[:, :TOP_K].astype(I32), tm)
    y_rows = _moe_experts(x_rows, gain, wg, wu, wd, tok * SUBLANES, dst * SUBLANES,
                          tile_expert, tile_valid, n_rows, tm=tm, tf=tf)
    return _moe_combine(x, route, y_rows, out_gain)


def _route_metadata(experts, tm):
    t = experts.shape[0]
    n_slots = TOP_K * t
    eflat = experts.T.reshape(-1)
    onehot = (eflat[:, None] == jnp.arange(N_EXPERTS, dtype=I32)[None, :]).astype(I32)
    csum = jnp.cumsum(onehot, axis=0)
    rank = jnp.sum((csum - 1) * onehot, axis=1)
    counts = csum[-1]
    tiles_per = (counts + tm - 1) // tm
    tile_end = jnp.cumsum(tiles_per)
    start = (tile_end - tiles_per) * tm
    pos = (jnp.sum(start[None, :] * onehot, axis=1) + rank).astype(I32)
    n_tiles = n_slots // tm + N_EXPERTS
    n_rows_pad = n_tiles * tm
    slot_of_row = jnp.full((n_rows_pad,), -1, I32).at[pos].set(jnp.arange(n_slots, dtype=I32))
    real = slot_of_row >= 0
    tok = jnp.concatenate([jnp.where(real, slot_of_row % t, 0), jnp.zeros((2 * tm,), I32)])
    dump = n_slots + jnp.cumsum(jnp.logical_not(real).astype(I32)) - 1
    dst = jnp.concatenate([jnp.zeros((tm,), I32), jnp.where(real, slot_of_row, dump)])
    tile_id = jnp.arange(n_tiles + 2, dtype=I32)
    tile_expert = jnp.sum((tile_id[:, None] >= tile_end[None, :]).astype(I32), axis=1)
    tile_valid = (tile_id < tile_end[-1]).astype(I32)
    last_expert = jnp.max(jnp.where(counts > 0, jnp.arange(N_EXPERTS, dtype=I32), 0))
    tile_expert = jnp.where(tile_valid > 0, tile_expert, last_expert).astype(I32)
    return tok, dst.astype(I32), tile_expert[:n_tiles + 1], tile_valid, n_rows_pad


def _final_norm_kernel(x_ref, g_ref, o_ref):
    o_ref[...] = _rms(x_ref[...], g_ref[...])


def _final_norm(x, gain, *, tm=1024):
    t, d = x.shape
    return pl.pallas_call(
        _final_norm_kernel,
        grid=(t // tm,),
        in_specs=[pl.BlockSpec((tm, d), lambda i: (i, 0)), pl.BlockSpec((1, d), lambda i: (0, 0))],
        out_specs=pl.BlockSpec((tm, d), lambda i: (i, 0)),
        out_shape=jax.ShapeDtypeStruct((t, d), F32),
        compiler_params=_params("parallel"),
        name="final_norm",
    )(x, gain)


def _rope_tables(seq):
    half = HEAD_DIM // 2
    inv_freq = ROPE_THETA ** (-jnp.arange(half, dtype=F32) / half)
    ang = jnp.arange(seq, dtype=F32)[:, None] * inv_freq[None, :]
    cos, sin = jnp.cos(ang), jnp.sin(ang)
    reps = LANES // HEAD_DIM
    cos_t = jnp.tile(jnp.concatenate([cos, cos], axis=1), (1, reps))
    sin_t = jnp.tile(jnp.concatenate([-sin, sin], axis=1), (1, reps))
    return cos_t, sin_t


def _in_proj_weight(w_in):
    o = 0
    offs = {}
    for name, width in (("qa", A_W), ("ka", A_W), ("va", A_W), ("qb", B_QW), ("kb", B_KVW),
                        ("vb", B_KVW), ("qc", C_W), ("kc", C_W), ("vc", C_W), ("g", 3 * D_MODEL)):
        offs[name] = (o, o + width)
        o += width
    q_scale = {name: HEAD_DIM ** -0.5 * LOG2E for name in ("qa", "qb", "qc")}
    parts = []
    for name in ("g", "qa", "ka", "qb", "kb", "vb", "va", "qc", "kc", "vc"):
        lo, hi = offs[name]
        blk = w_in[:, lo:hi]
        if name in q_scale:
            blk = blk * q_scale[name]
        parts.append(blk)
    return jnp.concatenate(parts, axis=1).astype(BF16)


def _rope_group_map():
    groups = {}
    for t in range(IN_W // PROJ_TN):
        lo, hi = t * PROJ_TN, (t + 1) * PROJ_TN
        n = (min(hi, ROPE_HI) - max(lo, ROPE_LO)) // LANES
        if n > 0:
            assert max(lo, ROPE_LO) == lo
            groups[t] = n
    return groups


def kernel(x, mem, norm_mix, w_in, w_proj_a, w_proj_b, w_proj_c, w_mix_out, sinks, norm_cross,
           norm_mem, w_xq, w_xkv, w_xo, norm_ffn, ffn_gate, ffn_up, ffn_down, moe_router,
           moe_gate, moe_up, moe_down, final_norm):
    batch, seq, d = x.shape
    depth = norm_mix.shape[0]
    assert d == D_MODEL and seq % 1024 == 0 and mem.shape[1] == MEM_LEN
    t = batch * seq
    xf = x.reshape(t, d)
    memf = mem.reshape(batch * MEM_LEN, d)
    cos_t, sin_t = _rope_tables(seq)
    rope_groups = _rope_group_map()
    ones_tab = jnp.ones((MEM_LEN, LANES), F32)

    for l in range(depth):
        gain = lambda g: g[l].reshape(1, d)
        p = _norm_proj(xf, gain(norm_mix), _in_proj_weight(w_in[l]), cos_t, sin_t, seq,
                       tm=1024, tn=PROJ_TN, rope_groups=rope_groups)
        oa = _moba(p, batch, seq)
        ob = _swa(p, sinks[l] * LOG2E, batch, seq)
        oc = _stick(p, batch, seq)
        xf = _merge(xf, oa, ob, oc, p, w_proj_a[l].astype(BF16), w_proj_b[l].astype(BF16),
                    w_proj_c[l].astype(BF16), w_mix_out[l].astype(BF16))
        kv = _norm_proj(memf, gain(norm_mem), w_xkv[l].astype(BF16), ones_tab, ones_tab, MEM_LEN,
                        tm=MEM_LEN, tn=2 * X_W, rope_groups={})
        cross_args = (gain(norm_cross), w_xq[l].astype(BF16), kv, w_xo[l].astype(BF16), seq)
        if l % 2 == 0:
            i = l // 2
            xf = _cross(xf, *cross_args)
            xf = _ffn(xf, gain(norm_ffn), ffn_gate[i].astype(BF16), ffn_up[i].astype(BF16),
                      ffn_down[i].astype(BF16), tf=FFN_TF)
        else:
            i = l // 2
            xf, x_rows = _cross(xf, *cross_args, with_row_tiles=True)
            out_gain = final_norm.reshape(1, d) if l == depth - 1 else None
            xf = _moe(xf, x_rows, gain(norm_ffn), moe_router[i], moe_gate[i], moe_up[i],
                      moe_down[i], out_gain)
    if depth % 2 == 1:
        xf = _final_norm(xf, final_norm.reshape(1, d))
    return xf.reshape(batch, seq, d)
```

```python
import functools

import jax
import jax.numpy as jnp
from jax import lax
from jax.experimental import pallas as pl
from jax.experimental.pallas import tpu as pltpu

F32 = jnp.float32
BF16 = jnp.bfloat16
I32 = jnp.int32

D_MODEL = 1024
HEAD_DIM = 64
A_HEADS = 4
MOBA_BLOCK = 256
MOBA_TOPK = 3
B_HEADS = 8
B_KV_HEADS = 2
WINDOW = 128
C_HEADS = 4
SB_BLOCK = 256
MEM_LEN = 256
X_HEADS = 4
X_HEAD_DIM = 128
N_EXPERTS = 8
TOP_K = 2
ROPE_THETA = 10000.0
EPS = 1e-6

A_W = A_HEADS * HEAD_DIM
B_QW = B_HEADS * HEAD_DIM
B_KVW = B_KV_HEADS * HEAD_DIM
C_W = C_HEADS * HEAD_DIM
X_W = X_HEADS * X_HEAD_DIM
QKV_W = 3 * A_W + B_QW + 2 * B_KVW + 3 * C_W
IN_W = QKV_W + 3 * D_MODEL

LANES = 128
SUBLANES = 8
NEG = -1e30
LOG2E = 1.4426950408889634

COL_GA, COL_GB, COL_GC = 0, D_MODEL, 2 * D_MODEL
COL_QA = 3 * D_MODEL
COL_KA = COL_QA + A_W
COL_QB = COL_KA + A_W
COL_KB = COL_QB + B_QW
COL_VB = COL_KB + B_KVW
COL_VA = COL_VB + B_KVW
COL_QC = COL_VA + A_W
COL_KC = COL_QC + C_W
COL_VC = COL_KC + C_W
ROPE_LO, ROPE_HI = COL_QA, COL_VB

PROJ_TN = 768
FFN_TF = 256
MOE_TF = 512
MOE_TM = 1072
VMEM_LIMIT = 48 * 1024 * 1024


def _params(*sem):
    return pltpu.CompilerParams(dimension_semantics=sem, vmem_limit_bytes=VMEM_LIMIT)


def _rms(x, g):
    ms = jnp.mean(x * x, axis=-1, keepdims=True)
    return x * lax.rsqrt(ms + EPS) * g


def _dot(a, b):
    return jnp.dot(a, b, preferred_element_type=F32)


def _dot_t(a, b):
    return lax.dot_general(a, b, (((1,), (1,)), ((), ())), preferred_element_type=F32)


def _split_bf16(v):
    hi = v.astype(BF16)
    lo = (v - hi.astype(F32)).astype(BF16)
    return hi, lo


def _norm_proj_kernel(x_ref, g_ref, w_ref, cos_ref, sin_ref, o_ref, h_scr, *, rope_groups):
    j = pl.program_id(1)

    @pl.when(j == 0)
    def _():
        h_scr[...] = _rms(x_ref[...], g_ref[...]).astype(BF16)

    acc = _dot(h_scr[...], w_ref[...])
    n_groups = acc.shape[1] // LANES

    def rope(y):
        lane = lax.broadcasted_iota(I32, y.shape, 1)
        first_half = (lane % HEAD_DIM) < (HEAD_DIM // 2)
        sw = jnp.where(first_half, pltpu.roll(y, LANES - HEAD_DIM // 2, 1),
                       pltpu.roll(y, HEAD_DIM // 2, 1))
        return y * cos_ref[...] + sw * sin_ref[...]

    roped = sorted(rope_groups)

    for t in roped:
        @pl.when(j == t)
        def _(t=t):
            for gi in range(n_groups):
                y = acc[:, gi * LANES:(gi + 1) * LANES]
                if gi < rope_groups[t]:
                    y = rope(y)
                o_ref[:, gi * LANES:(gi + 1) * LANES] = y.astype(o_ref.dtype)

    is_plain = j >= 0
    for t in roped:
        is_plain = jnp.logical_and(is_plain, j != t)

    @pl.when(is_plain)
    def _():
        o_ref[...] = acc.astype(o_ref.dtype)


def _norm_proj(x, gain, w, cos, sin, seq, *, tm, tn, rope_groups):
    t, d = x.shape
    n = w.shape[1]
    pos_blocks = seq // tm
    kern = functools.partial(_norm_proj_kernel, rope_groups=rope_groups)
    return pl.pallas_call(
        kern,
        grid=(t // tm, n // tn),
        in_specs=[
            pl.BlockSpec((tm, d), lambda i, j: (i, 0)),
            pl.BlockSpec((1, d), lambda i, j: (0, 0)),
            pl.BlockSpec((d, tn), lambda i, j: (0, j)),
            pl.BlockSpec((tm, LANES), lambda i, j: (i % pos_blocks, 0)),
            pl.BlockSpec((tm, LANES), lambda i, j: (i % pos_blocks, 0)),
        ],
        out_specs=pl.BlockSpec((tm, tn), lambda i, j: (i, j)),
        out_shape=jax.ShapeDtypeStruct((t, n), BF16),
        scratch_shapes=[pltpu.VMEM((tm, d), BF16)],
        compiler_params=_params("parallel", "arbitrary"),
        name="norm_proj",
    )(x, gain, w, cos, sin)


MOBA_VROWS = HEAD_DIM + 16


def _moba_kernel(qt_ref, k_ref, vt_ref, o_ref, kmean_scr, kaug_scr, vaug_scr, qaug_scr, *, nblk):
    i = pl.program_id(1)
    blk = MOBA_BLOCK
    hd = HEAD_DIM
    aug = 2 * hd
    vr = MOBA_VROWS
    heads = range(A_HEADS)
    hsl = [slice(hh * hd, (hh + 1) * hd) for hh in heads]

    @pl.when(i == 0)
    def _():
        ones = jnp.ones((vr - hd, blk), BF16)
        blk_lane = lax.broadcasted_iota(I32, (blk, hd), 1)
        for n in range(nblk):
            rows = slice(n * blk, (n + 1) * blk)
            kmean_scr[n:n + 1, :] = jnp.mean(k_ref[rows, :].astype(F32), axis=0, keepdims=True)
            onehot = jnp.where(blk_lane == n, 1.0, 0.0).astype(BF16)
            for hh in heads:
                kaug_scr[rows, hh * aug:hh * aug + hd] = k_ref[rows, hsl[hh]]
                kaug_scr[rows, hh * aug + hd:(hh + 1) * aug] = onehot
                vaug_scr[n, hh * vr:hh * vr + hd, :] = vt_ref[n, hsl[hh], :]
                vaug_scr[n, hh * vr + hd:(hh + 1) * vr, :] = ones

    key = lax.broadcasted_iota(I32, (blk, blk), 0)
    qry = lax.broadcasted_iota(I32, (blk, blk), 1)
    causal = key <= qry
    blk_id = lax.broadcasted_iota(I32, (nblk, blk), 0)
    own0 = pl.multiple_of(i * blk, blk)

    init = []
    for hh in heads:
        qt = qt_ref[hsl[hh], :]
        km_hi, km_lo = _split_bf16(kmean_scr[:, hsl[hh]])
        g = _dot(km_hi, qt) + _dot(km_lo, qt)

        cnt = jnp.zeros((nblk, blk), I32)
        for m in range(nblk):
            gm = g[m:m + 1, :]
            beats = (gm > g) | ((gm == g) & (m < blk_id))
            cnt = cnt + jnp.where(beats, (m < i).astype(I32), 0)
        sel = (cnt < MOBA_TOPK) & (blk_id < i)
        qaug_scr[hh * aug:hh * aug + hd, :] = qt
        qaug_scr[hh * aug + hd:hh * aug + hd + nblk, :] = jnp.where(sel, 0.0, NEG).astype(BF16)
        qaug_scr[hh * aug + hd + nblk:(hh + 1) * aug, :] = jnp.zeros((hd - nblk, blk), BF16)

        s = _dot(k_ref[pl.ds(own0, blk), hsl[hh]], qt)
        s = jnp.where(causal, s, NEG)
        m0 = jnp.max(s, axis=0, keepdims=True)
        p = jnp.exp2(s - m0)
        init.append((m0, _dot(vaug_scr[i, hh * vr:(hh + 1) * vr, :], p.astype(BF16))))

    def step(blocks, carry):
        rows = [pl.ds(pl.multiple_of(j * blk, blk), blk) for j in blocks]
        ss = [[_dot(kaug_scr[r, hh * aug:(hh + 1) * aug], qaug_scr[hh * aug:(hh + 1) * aug, :])
               for r in rows] for hh in heads]
        ms = []
        for hh in heads:
            m_new = carry[hh][0]
            for s in ss[hh]:
                m_new = jnp.maximum(m_new, jnp.max(s, axis=0, keepdims=True))
            ms.append(m_new)
        ps = [[jnp.exp2(s - ms[hh]).astype(BF16) for s in ss[hh]] for hh in heads]
        pvs = []
        for hh in heads:
            pv = None
            for j, p in zip(blocks, ps[hh]):
                d = _dot(vaug_scr[j, hh * vr:(hh + 1) * vr, :], p)
                pv = d if pv is None else pv + d
            pvs.append(pv)
        return tuple((ms[hh], jnp.exp2(carry[hh][0] - ms[hh]) * carry[hh][1] + pvs[hh])
                     for hh in heads)

    fin = lax.fori_loop(0, i // 2, lambda jp, c: step([2 * jp, 2 * jp + 1], c), tuple(init))
    fin = lax.fori_loop(0, i % 2, lambda _, c: step([i - 1], c), fin)
    out_t = jnp.concatenate([acc[:hd, :] / acc[hd:hd + 1, :] for _, acc in fin], axis=0)
    o_ref[...] = out_t.T.astype(o_ref.dtype)


def _moba(p, batch, seq):
    nblk = seq // MOBA_BLOCK
    assert nblk <= HEAD_DIM
    qt = _blocks_t(p, COL_QA, A_W, batch, seq, MOBA_BLOCK)
    vt = _blocks_t(p, COL_VA, A_W, batch, seq, MOBA_BLOCK)
    kern = functools.partial(_moba_kernel, nblk=nblk)
    return pl.pallas_call(
        kern,
        grid=(batch, nblk),
        in_specs=[
            pl.BlockSpec((None, None, A_W, MOBA_BLOCK), lambda b, i: (b, i, 0, 0)),
            pl.BlockSpec((seq, A_W), lambda b, i: (b, COL_KA // A_W)),
            pl.BlockSpec((None, nblk, A_W, MOBA_BLOCK), lambda b, i: (b, 0, 0, 0)),
        ],
        out_specs=pl.BlockSpec((MOBA_BLOCK, A_W), lambda b, i: (b * nblk + i, 0)),
        out_shape=jax.ShapeDtypeStruct((batch * seq, A_W), BF16),
        scratch_shapes=[
            pltpu.VMEM((nblk, A_W), F32),
            pltpu.VMEM((seq, 2 * A_W), BF16),
            pltpu.VMEM((nblk, A_HEADS * MOBA_VROWS, MOBA_BLOCK), BF16),
            pltpu.VMEM((2 * A_W, MOBA_BLOCK), BF16),
        ],
        compiler_params=_params("parallel", "arbitrary"),
        name="moba",
    )(qt, p, vt)


def _swa_kernel(sinks_ref, q_ref, kv_ref, pkv_ref, o_ref, kv_scr, *, tq):
    i = pl.program_id(1)
    w = WINDOW
    kv_scr[0:w, :] = pkv_ref[...]
    kv_scr[w:, :] = kv_ref[...]

    row = lax.broadcasted_iota(I32, (w, 2 * w), 0)
    col = lax.broadcasted_iota(I32, (w, 2 * w), 1)
    rel = row + w - col
    band = (rel >= 0) & (rel < w)
    group = B_HEADS // B_KV_HEADS

    ones = jnp.ones((2 * w, HEAD_DIM), BF16)

    def body(n, carry):
        base = pl.multiple_of(n * w, w)
        kvt = kv_scr[pl.ds(base, 2 * w), :]
        qt = q_ref[pl.ds(base, w), :]
        col_min = jnp.where(jnp.logical_and(i == 0, n == 0), w, 0)
        mask = band & (col >= col_min)
        heads = [(kh, gi) for kh in range(B_KV_HEADS) for gi in range(group)]
        hcol = lambda hd: slice(hd * HEAD_DIM, (hd + 1) * HEAD_DIM)
        ss = []
        for kh in range(B_KV_HEADS):
            q4 = jnp.concatenate([qt[:, hcol(kh * group + gi)] for gi in range(group)], axis=0)
            ss.append(_dot_t(q4, kvt[:, hcol(kh)]))
        ms, ps = {}, {}
        for kh, gi in heads:
            s = jnp.where(mask, ss[kh][gi * w:(gi + 1) * w], NEG)
            m = jnp.maximum(jnp.max(s, axis=-1, keepdims=True), sinks_ref[kh * group + gi])
            ms[(kh, gi)] = m
            ps[(kh, gi)] = jnp.exp2(s - m).astype(BF16)
        pvs = []
        for kh in range(B_KV_HEADS):
            vaug = jnp.concatenate([kvt[:, B_KVW + kh * HEAD_DIM:B_KVW + (kh + 1) * HEAD_DIM], ones],
                                   axis=1)
            p4 = jnp.concatenate([ps[(kh, gi)] for gi in range(group)], axis=0)
            pvs.append(_dot(p4, vaug))
        outs = []
        for kh, gi in heads:
            pv = pvs[kh][gi * w:(gi + 1) * w]
            l = pv[:, HEAD_DIM:] + jnp.exp2(sinks_ref[kh * group + gi] - ms[(kh, gi)])
            outs.append(pv[:, :HEAD_DIM] / l)
        o_ref[pl.ds(base, w), :] = jnp.concatenate(outs, axis=1).astype(o_ref.dtype)
        return carry

    lax.fori_loop(0, tq // w, body, 0)


def _swa(p, sinks, batch, seq, *, tq=512):
    nq = seq // tq
    sub = tq // WINDOW
    kern = functools.partial(_swa_kernel, tq=tq)
    return pl.pallas_call(
        kern,
        grid=(batch, nq),
        in_specs=[
            pl.BlockSpec(memory_space=pltpu.SMEM),
            pl.BlockSpec((tq, B_QW), lambda b, i: (b * nq + i, COL_QB // B_QW)),
            pl.BlockSpec((tq, 2 * B_KVW), lambda b, i: (b * nq + i, COL_KB // (2 * B_KVW))),
            pl.BlockSpec((WINDOW, 2 * B_KVW),
                         lambda b, i: (jnp.maximum((b * nq + i) * sub - 1, 0),
                                       COL_KB // (2 * B_KVW))),
        ],
        out_specs=pl.BlockSpec((tq, B_QW), lambda b, i: (b * nq + i, 0)),
        out_shape=jax.ShapeDtypeStruct((batch * seq, B_QW), BF16),
        scratch_shapes=[pltpu.VMEM((tq + WINDOW, 2 * B_KVW), BF16)],
        compiler_params=_params("parallel", "parallel"),
        name="swa",
    )(sinks, p, p, p)


def _stick_kernel(qt_ref, k_ref, vt_ref, o_ref, z_scr, sp_scr, kpad_scr):
    i = pl.program_id(1)
    blk = SB_BLOCK
    hd = HEAD_DIM
    heads = range(C_HEADS)
    hsl = [slice(hh * hd, (hh + 1) * hd) for hh in heads]
    key = lax.broadcasted_iota(I32, (blk, blk), 0)
    qry = lax.broadcasted_iota(I32, (blk, blk), 1)
    causal = key < qry
    ntri = jnp.where(qry >= key, -1.0, 0.0).astype(BF16)
    own0 = pl.multiple_of(i * blk, blk)

    def softplus2(z):
        return jnp.maximum(z, 0.0) + jnp.log2(1.0 + jnp.exp2(-jnp.abs(z)))

    def suffix(sp):
        incl = _dot(ntri, sp.astype(BF16))
        return incl, incl[0:1, :]

    def weight(x):
        return jnp.exp2(x).astype(BF16)

    @pl.when(i == 0)
    def _():
        pad = jnp.zeros((blk, LANES - hd), BF16)
        for n in range(k_ref.shape[0] // blk):
            rows = slice(n * blk, (n + 1) * blk)
            for hh in heads:
                kpad_scr[rows, hh * LANES:hh * LANES + hd] = k_ref[rows, hsl[hh]]
                kpad_scr[rows, hh * LANES + hd:(hh + 1) * LANES] = pad

    ksl = [slice(hh * LANES, (hh + 1) * LANES) for hh in heads]
    qts = [jnp.concatenate([qt_ref[hsl[hh], :], jnp.zeros((LANES - hd, blk), BF16)], axis=0)
           for hh in heads]
    init = []
    for hh in heads:
        z = _dot(kpad_scr[pl.ds(own0, blk), ksl[hh]], qts[hh])
        incl, carry0 = suffix(jnp.where(causal, softplus2(z), 0.0))
        wgt = jnp.where(causal, weight(z + incl), 0.0)
        init.append((carry0, _dot(vt_ref[i, hsl[hh], :], wgt.astype(BF16))))

    def keys(nb):
        return [(hh, b) for hh in heads for b in range(nb)]

    def logits(blocks):
        rows = [pl.ds(pl.multiple_of(j * blk, blk), blk) for j in blocks]
        return {(hh, b): _dot(kpad_scr[rows[b], ksl[hh]], qts[hh])
                for hh, b in keys(len(blocks))}

    def values(blocks, zs, incls):
        ws = {k: weight(zs[k] + incls[k]) for k in zs}
        return {(hh, b): _dot(vt_ref[blocks[b], hsl[hh], :], ws[(hh, b)]) for hh, b in zs}

    def accumulate(state, pvs, incls, nb):
        out = []
        for hh in heads:
            carry, acc = state[hh]
            for b in range(nb):
                acc = acc + pvs[(hh, b)] * jnp.exp2(carry)
                carry = carry + incls[(hh, b)][0:1, :]
            out.append((carry, acc))
        return tuple(out)

    def single(j, state):
        zs = logits([j])
        incls = {k: suffix(softplus2(zs[k]))[0] for k in zs}
        return accumulate(state, values([j], zs, incls), incls, 1)

    odd = i % 2
    state = lax.fori_loop(0, odd, lambda _, s: single(i - 1, s), tuple(init))
    n_pairs = i // 2
    nearest = i - 1 - odd

    def pair(p):
        return [nearest - 2 * p, nearest - 1 - 2 * p]

    def slot_of(p, k):
        return (p % 2) * len(keys(2)) + keys(2).index(k)

    def stage_in(p):
        zs = logits(pair(p))
        for k in zs:
            z_scr[slot_of(p, k)] = zs[k]
            sp_scr[slot_of(p, k)] = softplus2(zs[k]).astype(BF16)

    def stage_out(p, state, prefetch):
        incls = {k: _dot(ntri, sp_scr[slot_of(p, k)]) for k in keys(2)}
        if prefetch:
            zs_next = logits(pair(p + 1))
        zs = {k: z_scr[slot_of(p, k)] for k in keys(2)}
        pvs = values(pair(p), zs, incls)
        if prefetch:
            for k in zs_next:
                z_scr[slot_of(p + 1, k)] = zs_next[k]
                sp_scr[slot_of(p + 1, k)] = softplus2(zs_next[k]).astype(BF16)
        return accumulate(state, pvs, incls, 2)

    @pl.when(n_pairs > 0)
    def _():
        stage_in(0)

    state = lax.fori_loop(0, n_pairs - 1, lambda p, s: stage_out(p, s, True), state)
    fin = lax.fori_loop(0, jnp.minimum(n_pairs, 1),
                        lambda _, s: stage_out(n_pairs - 1, s, False), state)
    out_t = jnp.concatenate([acc for _, acc in fin], axis=0)
    o_ref[...] = out_t.T.astype(o_ref.dtype)


def _blocks_t(p, col, width, batch, seq, blk):
    t = p[:, col:col + width].reshape(batch, seq // blk, blk, width)
    return jnp.swapaxes(t, 2, 3)


def _stick(p, batch, seq):
    nblk = seq // SB_BLOCK
    qt = _blocks_t(p, COL_QC, C_W, batch, seq, SB_BLOCK)
    vt = _blocks_t(p, COL_VC, C_W, batch, seq, SB_BLOCK)
    return pl.pallas_call(
        _stick_kernel,
        grid=(batch, nblk),
        in_specs=[
            pl.BlockSpec((None, None, C_W, SB_BLOCK), lambda b, i: (b, i, 0, 0)),
            pl.BlockSpec((seq, C_W), lambda b, i: (b, COL_KC // C_W)),
            pl.BlockSpec((None, nblk, C_W, SB_BLOCK), lambda b, i: (b, 0, 0, 0)),
        ],
        out_specs=pl.BlockSpec((SB_BLOCK, C_W), lambda b, i: (b * nblk + i, 0)),
        out_shape=jax.ShapeDtypeStruct((batch * seq, C_W), BF16),
        scratch_shapes=[pltpu.VMEM((4 * C_HEADS, SB_BLOCK, SB_BLOCK), F32),
                        pltpu.VMEM((4 * C_HEADS, SB_BLOCK, SB_BLOCK), BF16),
                        pltpu.VMEM((seq, C_HEADS * LANES), BF16)],
        compiler_params=_params("parallel", "arbitrary"),
        name="stick",
    )(qt, p, vt)


def _sigmoid(x):
    return 1.0 / (1.0 + jnp.exp(-x))


def _merge_kernel(x_ref, oa_ref, ob_ref, oc_ref, ga_ref, gb_ref, gc_ref,
                  wa_ref, wb_ref, wc_ref, wo_ref, o_ref):
    mixed = _sigmoid(ga_ref[...].astype(F32)) * _dot(oa_ref[...], wa_ref[...])
    mixed = mixed + _sigmoid(gb_ref[...].astype(F32)) * _dot(ob_ref[...], wb_ref[...])
    mixed = mixed + _sigmoid(gc_ref[...].astype(F32)) * _dot(oc_ref[...], wc_ref[...])
    o_ref[...] = x_ref[...] + _dot(mixed.astype(BF16), wo_ref[...])


def _merge(x, oa, ob, oc, p, wa, wb, wc, wo, *, tm=512):
    t, d = x.shape
    full = lambda a: pl.BlockSpec(a.shape, lambda i: (0, 0))
    return pl.pallas_call(
        _merge_kernel,
        grid=(t // tm,),
        in_specs=[
            pl.BlockSpec((tm, d), lambda i: (i, 0)),
            pl.BlockSpec((tm, A_W), lambda i: (i, 0)),
            pl.BlockSpec((tm, B_QW), lambda i: (i, 0)),
            pl.BlockSpec((tm, C_W), lambda i: (i, 0)),
            pl.BlockSpec((tm, d), lambda i: (i, COL_GA // D_MODEL)),
            pl.BlockSpec((tm, d), lambda i: (i, COL_GB // D_MODEL)),
            pl.BlockSpec((tm, d), lambda i: (i, COL_GC // D_MODEL)),
            full(wa), full(wb), full(wc), full(wo),
        ],
        out_specs=pl.BlockSpec((tm, d), lambda i: (i, 0)),
        out_shape=jax.ShapeDtypeStruct((t, d), F32),
        compiler_params=_params("parallel"),
        name="merge",
    )(x, oa, ob, oc, p, p, p, wa, wb, wc, wo)


def _cross_kernel(x_ref, g_ref, wq_ref, kv_ref, wo_ref, o_ref, *rows_ref):
    x = x_ref[...]
    h = _rms(x, g_ref[...]).astype(BF16)
    q = (_dot(h, wq_ref[...]) * (X_HEAD_DIM ** -0.5)).astype(BF16)
    outs = []
    for hd in range(X_HEADS):
        hs = slice(hd * X_HEAD_DIM, (hd + 1) * X_HEAD_DIM)
        s = _dot_t(q[:, hs], kv_ref[:, hs])
        m = jnp.max(s, axis=-1, keepdims=True)
        p = jnp.exp(s - m)
        l = jnp.sum(p, axis=-1, keepdims=True)
        vs = slice(X_W + hd * X_HEAD_DIM, X_W + (hd + 1) * X_HEAD_DIM)
        outs.append(_dot(p.astype(BF16), kv_ref[:, vs]) / l)
    att = jnp.concatenate(outs, axis=1).astype(BF16)
    o_ref[...] = x + _dot(att, wo_ref[...])
    if rows_ref:
        tm, d = o_ref.shape
        for c in range(d // LANES):
            rows_ref[0][pl.ds(c, tm, stride=SUBLANES), :] = o_ref[:, c * LANES:(c + 1) * LANES]


def _cross(x, gain, wq, kv, wo, seq, *, tm=512, with_row_tiles=False):
    t, d = x.shape
    per_batch = seq // tm
    full = lambda a: pl.BlockSpec(a.shape, lambda i: (0, 0))
    out_specs = [pl.BlockSpec((tm, d), lambda i: (i, 0))]
    out_shape = [jax.ShapeDtypeStruct((t, d), F32)]
    if with_row_tiles:
        out_specs.append(pl.BlockSpec((tm * SUBLANES, LANES), lambda i: (i, 0)))
        out_shape.append(jax.ShapeDtypeStruct((t * SUBLANES, LANES), F32))
    outs = pl.pallas_call(
        _cross_kernel,
        grid=(t // tm,),
        in_specs=[
            pl.BlockSpec((tm, d), lambda i: (i, 0)),
            full(gain), full(wq),
            pl.BlockSpec((MEM_LEN, 2 * X_W), lambda i: (i // per_batch, 0)),
            full(wo),
        ],
        out_specs=out_specs,
        out_shape=out_shape,
        compiler_params=_params("parallel"),
        name="cross",
    )(x, gain, wq, kv, wo)
    return outs if with_row_tiles else outs[0]


def _silu(x):
    return x * _sigmoid(x)


def _ffn_kernel(x_ref, g_ref, wg_ref, wu_ref, wd_ref, o_ref, h_scr, acc_scr):
    f = pl.program_id(1)

    @pl.when(f == 0)
    def _():
        h_scr[...] = _rms(x_ref[...], g_ref[...]).astype(BF16)
        acc_scr[...] = x_ref[...]

    h = h_scr[...]
    a = _silu(_dot(h, wg_ref[...])) * _dot(h, wu_ref[...])
    acc_scr[...] += _dot(a.astype(BF16), wd_ref[...])

    @pl.when(f == pl.num_programs(1) - 1)
    def _():
        o_ref[...] = acc_scr[...]


def _ffn(x, gain, wg, wu, wd, *, tm=1024, tf=256):
    t, d = x.shape
    ff = wd.shape[0]
    return pl.pallas_call(
        _ffn_kernel,
        grid=(t // tm, ff // tf),
        in_specs=[
            pl.BlockSpec((tm, d), lambda i, f: (i, 0)),
            pl.BlockSpec((1, d), lambda i, f: (0, 0)),
            pl.BlockSpec((d, tf), lambda i, f: (0, f)),
            pl.BlockSpec((d, tf), lambda i, f: (0, f)),
            pl.BlockSpec((tf, d), lambda i, f: (f, 0)),
        ],
        out_specs=pl.BlockSpec((tm, d), lambda i, f: (i, 0)),
        out_shape=jax.ShapeDtypeStruct((t, d), F32),
        scratch_shapes=[pltpu.VMEM((tm, d), BF16), pltpu.VMEM((tm, d), F32)],
        compiler_params=_params("parallel", "arbitrary"),
        name="ffn",
    )(x, gain, wg, wu, wd)


def _router_kernel(x_ref, g_ref, wr_ref, o_ref):
    h = _rms(x_ref[...], g_ref[...])
    h_hi, h_lo = _split_bf16(h)
    w_hi, w_lo = _split_bf16(wr_ref[...])
    logits = _dot(h_hi, w_hi) + (_dot(h_hi, w_lo) + _dot(h_lo, w_hi))
    lane = lax.broadcasted_iota(I32, logits.shape, 1)
    logits = jnp.where(lane < N_EXPERTS, logits, NEG)
    v1 = jnp.max(logits, axis=-1, keepdims=True)
    i1 = jnp.min(jnp.where(logits == v1, lane, LANES), axis=-1, keepdims=True)
    rest = jnp.where(lane == i1, NEG, logits)
    v2 = jnp.max(rest, axis=-1, keepdims=True)
    i2 = jnp.min(jnp.where(rest == v2, lane, LANES), axis=-1, keepdims=True)
    e = jnp.exp(v2 - v1)
    w1 = 1.0 / (1.0 + e)
    w2 = e / (1.0 + e)
    out = jnp.where(lane == 0, i1.astype(F32), 0.0)
    out = jnp.where(lane == 1, i2.astype(F32), out)
    out = jnp.where(lane == 2, w1, out)
    out = jnp.where(lane == 3, w2, out)
    o_ref[...] = out


def _router(x, gain, wr, *, tm=1024):
    t, d = x.shape
    return pl.pallas_call(
        _router_kernel,
        grid=(t // tm,),
        in_specs=[
            pl.BlockSpec((tm, d), lambda i: (i, 0)),
            pl.BlockSpec((1, d), lambda i: (0, 0)),
            pl.BlockSpec((d, LANES), lambda i: (0, 0)),
        ],
        out_specs=pl.BlockSpec((tm, LANES), lambda i: (i, 0)),
        out_shape=jax.ShapeDtypeStruct((t, LANES), F32),
        compiler_params=_params("parallel"),
        name="router",
    )(x, gain, wr)


def _moe_kernel(tok_ref, dst_ref, texp_ref, tval_ref, x_hbm, g_ref, wg_ref, wu_ref, wd_ref,
                y_hbm, xbuf, obuf, h_scr, acc_scr, gsem, ssem, *, tm, nf):
    m = pl.program_id(0)
    f = pl.program_id(1)
    last_phase = pl.num_programs(0) - 1
    slot = m % 2
    other = 1 - slot
    valid = tval_ref[m] > 0
    per_step = (tm // nf) // 8 * 8
    in_steps = per_step * nf

    def row_tile(r):
        return pl.ds(pl.multiple_of(r * SUBLANES, SUBLANES), SUBLANES)

    def gather_row(tile, r, s):
        tok = pl.multiple_of(tok_ref[tile * tm + r], SUBLANES)
        return pltpu.make_async_copy(x_hbm.at[pl.ds(tok, SUBLANES), :], xbuf.at[s, row_tile(r), :],
                                     gsem.at[s])

    def scatter_row(phase, r, s):
        dst = pl.multiple_of(dst_ref[phase * tm + r], SUBLANES)
        return pltpu.make_async_copy(obuf.at[s, row_tile(r), :], y_hbm.at[pl.ds(dst, SUBLANES), :],
                                     ssem.at[s])

    def chunk(c):
        return pl.ds(c, tm, stride=SUBLANES)

    wait_group = 16

    def gather_wait(tile, s):
        def body(it, c):
            for u in range(wait_group):
                gather_row(tile, it * wait_group + u, s).wait()
            return c
        lax.fori_loop(0, tm // wait_group, body, 0)

    def scatter_wait(phase, s):
        def body(it, c):
            for u in range(wait_group):
                scatter_row(phase, it * wait_group + u, s).wait()
            return c
        lax.fori_loop(0, tm // wait_group, body, 0)

    next_valid = tval_ref[m + 1] > 0
    prev_valid = jnp.logical_and(m >= 1, tval_ref[jnp.maximum(m - 1, 0)] > 0)
    prev2_valid = jnp.logical_and(m >= 2, tval_ref[jnp.maximum(m - 2, 0)] > 0)
    pad_only = jnp.logical_and(jnp.logical_not(valid), m < last_phase)

    def pad_fill():
        first = pl.multiple_of(dst_ref[(m + 1) * tm], SUBLANES)
        return pltpu.make_async_copy(obuf.at[slot], y_hbm.at[pl.ds(first, tm * SUBLANES), :],
                                     ssem.at[slot])

    @pl.when(jnp.logical_and(f == 0, prev2_valid))
    def _():
        scatter_wait(m - 1, slot)

    @pl.when(jnp.logical_and(f == 0, pad_only))
    def _():
        obuf[slot] = jnp.zeros(obuf.shape[1:], obuf.dtype)
        pad_fill().start()

    @pl.when(jnp.logical_and(f == 0, next_valid))
    def _():
        for r in range(in_steps, tm):
            gather_row(m + 1, r, other).start(priority=r % 2)

    @pl.when(jnp.logical_and(f == 0, prev_valid))
    def _():
        for r in range(in_steps, tm):
            scatter_row(m, r, other).start(priority=r % 2)

    @pl.when(jnp.logical_and(f == 0, valid))
    def _():
        @pl.when(m == 0)
        def _():
            def first(r, c):
                gather_row(0, r, 0).start()
                return c
            lax.fori_loop(0, tm, first, 0, unroll=8)

        gather_wait(m, slot)
        n_chunks = h_scr.shape[1] // LANES
        xs = [xbuf[slot, chunk(c), :] for c in range(n_chunks)]
        ssq = xs[0] * xs[0]
        for xc in xs[1:]:
            ssq = ssq + xc * xc
        inv = lax.rsqrt(jnp.sum(ssq, axis=-1, keepdims=True) / h_scr.shape[1] + EPS)
        for c in range(n_chunks):
            lanes = slice(c * LANES, (c + 1) * LANES)
            h_scr[:, lanes] = (xs[c] * inv * g_ref[:, lanes]).astype(BF16)
        acc_scr[...] = jnp.zeros_like(acc_scr)

    def expert_step():
        h = h_scr[...]
        a = _silu(_dot(h, wg_ref[...].astype(BF16))) * _dot(h, wu_ref[...].astype(BF16))
        acc_scr[...] += _dot(a.astype(BF16), wd_ref[...].astype(BF16))

    def gather_rows():
        r0 = pl.multiple_of(f * per_step, 8)
        for rr in range(per_step):
            gather_row(m + 1, r0 + rr, other).start(priority=rr % 2)

    def scatter_rows():
        r0 = pl.multiple_of(f * per_step, 8)
        for rr in range(per_step):
            scatter_row(m, r0 + rr, other).start(priority=rr % 2)

    interior = jnp.logical_and(valid, jnp.logical_and(next_valid, prev_valid))
    edge = jnp.logical_not(interior)

    @pl.when(interior)
    def _():
        gather_rows()
        scatter_rows()
        expert_step()

    @pl.when(jnp.logical_and(edge, valid))
    def _():
        expert_step()

    @pl.when(jnp.logical_and(edge, next_valid))
    def _():
        gather_rows()

    @pl.when(jnp.logical_and(edge, prev_valid))
    def _():
        scatter_rows()

    @pl.when(jnp.logical_and(f == nf - 1, valid))
    def _():
        for c in range(acc_scr.shape[1] // LANES):
            obuf[slot, chunk(c), :] = acc_scr[:, c * LANES:(c + 1) * LANES]

    @pl.when(jnp.logical_and(f == nf - 1, pad_only))
    def _():
        pad_fill().wait()

    @pl.when(jnp.logical_and(jnp.logical_and(f == nf - 1, m == last_phase), prev_valid))
    def _():
        scatter_wait(m, other)


def _moe_experts(x_rows, gain, wg, wu, wd, tok, dst, tile_expert, tile_valid, n_rows, *, tm, tf):
    d = gain.shape[1]
    assert d == SUBLANES * LANES
    nf = wg.shape[2] // tf
    n_phases = tile_expert.shape[0]
    kern = functools.partial(_moe_kernel, tm=tm, nf=nf)

    def fsel(m, f, tval):
        return jnp.where(tval[m] > 0, f, nf - 1)

    grid_spec = pltpu.PrefetchScalarGridSpec(
        num_scalar_prefetch=4,
        grid=(n_phases, nf),
        in_specs=[
            pl.BlockSpec(memory_space=pl.ANY),
            pl.BlockSpec((1, d), lambda m, f, tok, dst, texp, tval: (0, 0)),
            pl.BlockSpec((None, d, tf),
                         lambda m, f, tok, dst, texp, tval: (texp[m], 0, fsel(m, f, tval))),
            pl.BlockSpec((None, d, tf),
                         lambda m, f, tok, dst, texp, tval: (texp[m], 0, fsel(m, f, tval))),
            pl.BlockSpec((None, tf, d),
                         lambda m, f, tok, dst, texp, tval: (texp[m], fsel(m, f, tval), 0)),
        ],
        out_specs=pl.BlockSpec(memory_space=pl.ANY),
        scratch_shapes=[
            pltpu.VMEM((2, tm * SUBLANES, LANES), F32),
            pltpu.VMEM((2, tm * SUBLANES, LANES), F32),
            pltpu.VMEM((tm, d), BF16),
            pltpu.VMEM((tm, d), F32),
            pltpu.SemaphoreType.DMA((2,)),
            pltpu.SemaphoreType.DMA((2,)),
        ],
    )
    return pl.pallas_call(
        kern,
        grid_spec=grid_spec,
        out_shape=jax.ShapeDtypeStruct((n_rows * SUBLANES, LANES), F32),
        compiler_params=_params("arbitrary", "arbitrary"),
        name="moe_experts",
    )(tok, dst, tile_expert, tile_valid, x_rows, gain, wg, wu, wd)


def _combine_kernel(x_ref, r_ref, y0_ref, y1_ref, g_ref, o_ref, *, out_norm):
    tm, d = x_ref.shape
    route = r_ref[...]
    w0 = route[:, TOP_K:TOP_K + 1]
    w1 = route[:, TOP_K + 1:TOP_K + 2]
    for c in range(d // LANES):
        lanes = slice(c * LANES, (c + 1) * LANES)
        rows = pl.ds(c, tm, stride=SUBLANES)
        o_ref[:, lanes] = x_ref[:, lanes] + w0 * y0_ref[rows, :] + w1 * y1_ref[rows, :]
    if out_norm:
        o_ref[...] = _rms(o_ref[...], g_ref[...])


def _moe_combine(x, route, y_rows, out_gain, *, tm=512):
    t, d = x.shape
    assert TOP_K == 2
    out_norm = out_gain is not None
    gain = out_gain if out_norm else jnp.ones((1, d), F32)
    return pl.pallas_call(
        functools.partial(_combine_kernel, out_norm=out_norm),
        grid=(t // tm,),
        in_specs=[
            pl.BlockSpec((tm, d), lambda i: (i, 0)),
            pl.BlockSpec((tm, LANES), lambda i: (i, 0)),
            pl.BlockSpec((tm * SUBLANES, LANES), lambda i: (i, 0)),
            pl.BlockSpec((tm * SUBLANES, LANES), lambda i: (t // tm + i, 0)),
            pl.BlockSpec((1, d), lambda i: (0, 0)),
        ],
        out_specs=pl.BlockSpec((tm, d), lambda i: (i, 0)),
        out_shape=jax.ShapeDtypeStruct((t, d), F32),
        compiler_params=_params("parallel"),
        name="moe_combine",
    )(x, route, y_rows, y_rows, gain)


def _moe(x, x_rows, gain, w_router, wg, wu, wd, out_gain=None, *, tm=MOE_TM, tf=MOE_TF):
    t, d = x.shape
    wr = jnp.zeros((d, LANES), F32).at[:, :N_EXPERTS].set(w_router)
    route = _router(x, gain, wr)
    tok, dst, tile_expert, tile_valid, n_rows = _route_metadata(route[:, :TOP_K].astype(I32), tm)
    y_rows = _moe_experts(x_rows, gain, wg, wu, wd, tok * SUBLANES, dst * SUBLANES,
                          tile_expert, tile_valid, n_rows, tm=tm, tf=tf)
    return _moe_combine(x, route, y_rows, out_gain)


def _route_metadata(experts, tm):
    t = experts.shape[0]
    n_slots = TOP_K * t
    eflat = experts.T.reshape(-1)
    onehot = (eflat[:, None] == jnp.arange(N_EXPERTS, dtype=I32)[None, :]).astype(I32)
    csum = jnp.cumsum(onehot, axis=0)
    rank = jnp.sum((csum - 1) * onehot, axis=1)
    counts = csum[-1]
    tiles_per = (counts + tm - 1) // tm
    tile_end = jnp.cumsum(tiles_per)
    start = (tile_end - tiles_per) * tm
    pos = (jnp.sum(start[None, :] * onehot, axis=1) + rank).astype(I32)
    n_tiles = n_slots // tm + N_EXPERTS
    n_rows_pad = n_tiles * tm
    slot_of_row = jnp.full((n_rows_pad,), -1, I32).at[pos].set(jnp.arange(n_slots, dtype=I32))
    real = slot_of_row >= 0
    tok = jnp.concatenate([jnp.where(real, slot_of_row % t, 0), jnp.zeros((2 * tm,), I32)])
    dump = n_slots + jnp.cumsum(jnp.logical_not(real).astype(I32)) - 1
    dst = jnp.concatenate([jnp.zeros((tm,), I32), jnp.where(real, slot_of_row, dump)])
    tile_id = jnp.arange(n_tiles + 2, dtype=I32)
    tile_expert = jnp.sum((tile_id[:, None] >= tile_end[None, :]).astype(I32), axis=1)
    tile_valid = (tile_id < tile_end[-1]).astype(I32)
    last_expert = jnp.max(jnp.where(counts > 0, jnp.arange(N_EXPERTS, dtype=I32), 0))
    tile_expert = jnp.where(tile_valid > 0, tile_expert, last_expert).astype(I32)
    return tok, dst.astype(I32), tile_expert[:n_tiles + 1], tile_valid, n_rows_pad


def _final_norm_kernel(x_ref, g_ref, o_ref):
    o_ref[...] = _rms(x_ref[...], g_ref[...])


def _final_norm(x, gain, *, tm=1024):
    t, d = x.shape
    return pl.pallas_call(
        _final_norm_kernel,
        grid=(t // tm,),
        in_specs=[pl.BlockSpec((tm, d), lambda i: (i, 0)), pl.BlockSpec((1, d), lambda i: (0, 0))],
        out_specs=pl.BlockSpec((tm, d), lambda i: (i, 0)),
        out_shape=jax.ShapeDtypeStruct((t, d), F32),
        compiler_params=_params("parallel"),
        name="final_norm",
    )(x, gain)


def _rope_tables(seq):
    half = HEAD_DIM // 2
    inv_freq = ROPE_THETA ** (-jnp.arange(half, dtype=F32) / half)
    ang = jnp.arange(seq, dtype=F32)[:, None] * inv_freq[None, :]
    cos, sin = jnp.cos(ang), jnp.sin(ang)
    reps = LANES // HEAD_DIM
    cos_t = jnp.tile(jnp.concatenate([cos, cos], axis=1), (1, reps))
    sin_t = jnp.tile(jnp.concatenate([-sin, sin], axis=1), (1, reps))
    return cos_t, sin_t


def _in_proj_weight(w_in):
    o = 0
    offs = {}
    for name, width in (("qa", A_W), ("ka", A_W), ("va", A_W), ("qb", B_QW), ("kb", B_KVW),
                        ("vb", B_KVW), ("qc", C_W), ("kc", C_W), ("vc", C_W), ("g", 3 * D_MODEL)):
        offs[name] = (o, o + width)
        o += width
    q_scale = {name: HEAD_DIM ** -0.5 * LOG2E for name in ("qa", "qb", "qc")}
    parts = []
    for name in ("g", "qa", "ka", "qb", "kb", "vb", "va", "qc", "kc", "vc"):
        lo, hi = offs[name]
        blk = w_in[:, lo:hi]
        if name in q_scale:
            blk = blk * q_scale[name]
        parts.append(blk)
    return jnp.concatenate(parts, axis=1).astype(BF16)


def _rope_group_map():
    groups = {}
    for t in range(IN_W // PROJ_TN):
        lo, hi = t * PROJ_TN, (t + 1) * PROJ_TN
        n = (min(hi, ROPE_HI) - max(lo, ROPE_LO)) // LANES
        if n > 0:
            assert max(lo, ROPE_LO) == lo
            groups[t] = n
    return groups


def kernel(x, mem, norm_mix, w_in, w_proj_a, w_proj_b, w_proj_c, w_mix_out, sinks, norm_cross,
           norm_mem, w_xq, w_xkv, w_xo, norm_ffn, ffn_gate, ffn_up, ffn_down, moe_router,
           moe_gate, moe_up, moe_down, final_norm):
    batch, seq, d = x.shape
    depth = norm_mix.shape[0]
    assert d == D_MODEL and seq % 1024 == 0 and mem.shape[1] == MEM_LEN
    t = batch * seq
    xf = x.reshape(t, d)
    memf = mem.reshape(batch * MEM_LEN, d)
    cos_t, sin_t = _rope_tables(seq)
    rope_groups = _rope_group_map()
    ones_tab = jnp.ones((MEM_LEN, LANES), F32)

    for l in range(depth):
        gain = lambda g: g[l].reshape(1, d)
        p = _norm_proj(xf, gain(norm_mix), _in_proj_weight(w_in[l]), cos_t, sin_t, seq,
                       tm=1024, tn=PROJ_TN, rope_groups=rope_groups)
        oa = _moba(p, batch, seq)
        ob = _swa(p, sinks[l] * LOG2E, batch, seq)
        oc = _stick(p, batch, seq)
        xf = _merge(xf, oa, ob, oc, p, w_proj_a[l].astype(BF16), w_proj_b[l].astype(BF16),
                    w_proj_c[l].astype(BF16), w_mix_out[l].astype(BF16))
        kv = _norm_proj(memf, gain(norm_mem), w_xkv[l].astype(BF16), ones_tab, ones_tab, MEM_LEN,
                        tm=MEM_LEN, tn=2 * X_W, rope_groups={})
        cross_args = (gain(norm_cross), w_xq[l].astype(BF16), kv, w_xo[l].astype(BF16), seq)
        if l % 2 == 0:
            i = l // 2
            xf = _cross(xf, *cross_args)
            xf = _ffn(xf, gain(norm_ffn), ffn_gate[i].astype(BF16), ffn_up[i].astype(BF16),
                      ffn_down[i].astype(BF16), tf=FFN_TF)
        else:
            i = l // 2
            xf, x_rows = _cross(xf, *cross_args, with_row_tiles=True)
            out_gain = final_norm.reshape(1, d) if l == depth - 1 else None
            xf = _moe(xf, x_rows, gain(norm_ffn), moe_router[i], moe_gate[i], moe_up[i],
                      moe_down[i], out_gain)
    if depth % 2 == 1:
        xf = _final_norm(xf, final_norm.reshape(1, d))
    return xf.reshape(batch, seq, d)
```

```python
import functools

import jax
import jax.numpy as jnp
from jax import lax
from jax.experimental import pallas as pl
from jax.experimental.pallas import tpu as pltpu

F32 = jnp.float32
BF16 = jnp.bfloat16
I32 = jnp.int32

D_MODEL = 1024
HEAD_DIM = 64
A_HEADS = 4
MOBA_BLOCK = 256
MOBA_TOPK = 3
B_HEADS = 8
B_KV_HEADS = 2
WINDOW = 128
C_HEADS = 4
SB_BLOCK = 256
MEM_LEN = 256
X_HEADS = 4
X_HEAD_DIM = 128
N_EXPERTS = 8
TOP_K = 2
ROPE_THETA = 10000.0
EPS = 1e-6

A_W = A_HEADS * HEAD_DIM
B_QW = B_HEADS * HEAD_DIM
B_KVW = B_KV_HEADS * HEAD_DIM
C_W = C_HEADS * HEAD_DIM
X_W = X_HEADS * X_HEAD_DIM
QKV_W = 3 * A_W + B_QW + 2 * B_KVW + 3 * C_W
IN_W = QKV_W + 3 * D_MODEL

LANES = 128
SUBLANES = 8
NEG = -1e30
LOG2E = 1.4426950408889634

COL_GA, COL_GB, COL_GC = 0, D_MODEL, 2 * D_MODEL
COL_QA = 3 * D_MODEL
COL_KA = COL_QA + A_W
COL_QB = COL_KA + A_W
COL_KB = COL_QB + B_QW
COL_VB = COL_KB + B_KVW
COL_VA = COL_VB + B_KVW
COL_QC = COL_VA + A_W
COL_KC = COL_QC + C_W
COL_VC = COL_KC + C_W
ROPE_LO, ROPE_HI = COL_QA, COL_VB

PROJ_TM = 2048
PROJ_TN = 768
FFN_TF = 256
MOE_TF = 512
MOE_TM = 1072
VMEM_LIMIT = 48 * 1024 * 1024


def _params(*sem):
    return pltpu.CompilerParams(dimension_semantics=sem, vmem_limit_bytes=VMEM_LIMIT)


def _rms(x, g):
    ms = jnp.mean(x * x, axis=-1, keepdims=True)
    return x * lax.rsqrt(ms + EPS) * g


def _dot(a, b):
    return jnp.dot(a, b, preferred_element_type=F32)


def _dot_t(a, b):
    return lax.dot_general(a, b, (((1,), (1,)), ((), ())), preferred_element_type=F32)


def _split_bf16(v):
    hi = v.astype(BF16)
    lo = (v - hi.astype(F32)).astype(BF16)
    return hi, lo


def _norm_proj_kernel(x_ref, g_ref, w_ref, cos_ref, sin_ref, o_ref, h_scr, *, rope_groups):
    j = pl.program_id(1)

    @pl.when(j == 0)
    def _():
        h_scr[...] = _rms(x_ref[...], g_ref[...]).astype(BF16)

    acc = _dot(h_scr[...], w_ref[...])
    n_groups = acc.shape[1] // LANES

    def rope(y):
        lane = lax.broadcasted_iota(I32, y.shape, 1)
        first_half = (lane % HEAD_DIM) < (HEAD_DIM // 2)
        sw = jnp.where(first_half, pltpu.roll(y, LANES - HEAD_DIM // 2, 1),
                       pltpu.roll(y, HEAD_DIM // 2, 1))
        return y * cos_ref[...] + sw * sin_ref[...]

    roped = sorted(rope_groups)

    for t in roped:
        @pl.when(j == t)
        def _(t=t):
            for gi in range(n_groups):
                y = acc[:, gi * LANES:(gi + 1) * LANES]
                if gi < rope_groups[t]:
                    y = rope(y)
                o_ref[:, gi * LANES:(gi + 1) * LANES] = y.astype(o_ref.dtype)

    is_plain = j >= 0
    for t in roped:
        is_plain = jnp.logical_and(is_plain, j != t)

    @pl.when(is_plain)
    def _():
        o_ref[...] = acc.astype(o_ref.dtype)


def _norm_proj(x, gain, w, cos, sin, seq, *, tm, tn, rope_groups):
    t, d = x.shape
    n = w.shape[1]
    pos_blocks = seq // tm
    kern = functools.partial(_norm_proj_kernel, rope_groups=rope_groups)
    return pl.pallas_call(
        kern,
        grid=(t // tm, n // tn),
        in_specs=[
            pl.BlockSpec((tm, d), lambda i, j: (i, 0)),
            pl.BlockSpec((1, d), lambda i, j: (0, 0)),
            pl.BlockSpec((d, tn), lambda i, j: (0, j)),
            pl.BlockSpec((tm, LANES), lambda i, j: (i % pos_blocks, 0)),
            pl.BlockSpec((tm, LANES), lambda i, j: (i % pos_blocks, 0)),
        ],
        out_specs=pl.BlockSpec((tm, tn), lambda i, j: (i, j)),
        out_shape=jax.ShapeDtypeStruct((t, n), BF16),
        scratch_shapes=[pltpu.VMEM((tm, d), BF16)],
        compiler_params=_params("parallel", "arbitrary"),
        name="norm_proj",
    )(x, gain, w, cos, sin)


MOBA_VROWS = HEAD_DIM + 16


def _moba_kernel(qt_ref, k_ref, vt_ref, o_ref, kmean_scr, kaug_scr, vaug_scr, qaug_scr, *, nblk):
    i = pl.program_id(1)
    blk = MOBA_BLOCK
    hd = HEAD_DIM
    aug = 2 * hd
    vr = MOBA_VROWS
    heads = range(A_HEADS)
    hsl = [slice(hh * hd, (hh + 1) * hd) for hh in heads]

    @pl.when(i == 0)
    def _():
        ones = jnp.ones((vr - hd, blk), BF16)
        blk_lane = lax.broadcasted_iota(I32, (blk, hd), 1)
        for n in range(nblk):
            rows = slice(n * blk, (n + 1) * blk)
            kmean_scr[n:n + 1, :] = jnp.mean(k_ref[rows, :].astype(F32), axis=0, keepdims=True)
            onehot = jnp.where(blk_lane == n, 1.0, 0.0).astype(BF16)
            for hh in heads:
                kaug_scr[rows, hh * aug:hh * aug + hd] = k_ref[rows, hsl[hh]]
                kaug_scr[rows, hh * aug + hd:(hh + 1) * aug] = onehot
                vaug_scr[n, hh * vr:hh * vr + hd, :] = vt_ref[n, hsl[hh], :]
                vaug_scr[n, hh * vr + hd:(hh + 1) * vr, :] = ones

    key = lax.broadcasted_iota(I32, (blk, blk), 0)
    qry = lax.broadcasted_iota(I32, (blk, blk), 1)
    causal = key <= qry
    blk_id = lax.broadcasted_iota(I32, (nblk, blk), 0)
    own0 = pl.multiple_of(i * blk, blk)

    init = []
    for hh in heads:
        qt = qt_ref[hsl[hh], :]
        km_hi, km_lo = _split_bf16(kmean_scr[:, hsl[hh]])
        g = _dot(km_hi, qt) + _dot(km_lo, qt)

        cnt = jnp.zeros((nblk, blk), I32)
        for m in range(nblk):
            gm = g[m:m + 1, :]
            beats = (gm > g) | ((gm == g) & (m < blk_id))
            cnt = cnt + jnp.where(beats, (m < i).astype(I32), 0)
        sel = (cnt < MOBA_TOPK) & (blk_id < i)
        qaug_scr[hh * aug:hh * aug + hd, :] = qt
        qaug_scr[hh * aug + hd:hh * aug + hd + nblk, :] = jnp.where(sel, 0.0, NEG).astype(BF16)
        qaug_scr[hh * aug + hd + nblk:(hh + 1) * aug, :] = jnp.zeros((hd - nblk, blk), BF16)

        s = _dot(k_ref[pl.ds(own0, blk), hsl[hh]], qt)
        s = jnp.where(causal, s, NEG)
        m0 = jnp.max(s, axis=0, keepdims=True)
        p = jnp.exp2(s - m0)
        init.append((m0, _dot(vaug_scr[i, hh * vr:(hh + 1) * vr, :], p.astype(BF16))))

    def step(blocks, carry):
        rows = [pl.ds(pl.multiple_of(j * blk, blk), blk) for j in blocks]
        ss = [[_dot(kaug_scr[r, hh * aug:(hh + 1) * aug], qaug_scr[hh * aug:(hh + 1) * aug, :])
               for r in rows] for hh in heads]
        ms = []
        for hh in heads:
            m_new = carry[hh][0]
            for s in ss[hh]:
                m_new = jnp.maximum(m_new, jnp.max(s, axis=0, keepdims=True))
            ms.append(m_new)
        ps = [[jnp.exp2(s - ms[hh]).astype(BF16) for s in ss[hh]] for hh in heads]
        pvs = []
        for hh in heads:
            pv = None
            for j, p in zip(blocks, ps[hh]):
                d = _dot(vaug_scr[j, hh * vr:(hh + 1) * vr, :], p)
                pv = d if pv is None else pv + d
            pvs.append(pv)
        return tuple((ms[hh], jnp.exp2(carry[hh][0] - ms[hh]) * carry[hh][1] + pvs[hh])
                     for hh in heads)

    fin = lax.fori_loop(0, i // 2, lambda jp, c: step([2 * jp, 2 * jp + 1], c), tuple(init))
    fin = lax.fori_loop(0, i % 2, lambda _, c: step([i - 1], c), fin)
    out_t = jnp.concatenate([acc[:hd, :] / acc[hd:hd + 1, :] for _, acc in fin], axis=0)
    o_ref[...] = out_t.T.astype(o_ref.dtype)


def _moba(p, batch, seq):
    nblk = seq // MOBA_BLOCK
    assert nblk <= HEAD_DIM
    qt = _blocks_t(p, COL_QA, A_W, batch, seq, MOBA_BLOCK)
    vt = _blocks_t(p, COL_VA, A_W, batch, seq, MOBA_BLOCK)
    kern = functools.partial(_moba_kernel, nblk=nblk)
    return pl.pallas_call(
        kern,
        grid=(batch, nblk),
        in_specs=[
            pl.BlockSpec((None, None, A_W, MOBA_BLOCK), lambda b, i: (b, i, 0, 0)),
            pl.BlockSpec((seq, A_W), lambda b, i: (b, COL_KA // A_W)),
            pl.BlockSpec((None, nblk, A_W, MOBA_BLOCK), lambda b, i: (b, 0, 0, 0)),
        ],
        out_specs=pl.BlockSpec((MOBA_BLOCK, A_W), lambda b, i: (b * nblk + i, 0)),
        out_shape=jax.ShapeDtypeStruct((batch * seq, A_W), BF16),
        scratch_shapes=[
            pltpu.VMEM((nblk, A_W), F32),
            pltpu.VMEM((seq, 2 * A_W), BF16),
            pltpu.VMEM((nblk, A_HEADS * MOBA_VROWS, MOBA_BLOCK), BF16),
            pltpu.VMEM((2 * A_W, MOBA_BLOCK), BF16),
        ],
        compiler_params=_params("parallel", "arbitrary"),
        name="moba",
    )(qt, p, vt)


def _swa_kernel(sinks_ref, q_ref, kv_ref, pkv_ref, o_ref, kv_scr, *, tq):
    i = pl.program_id(1)
    w = WINDOW
    kv_scr[0:w, :] = pkv_ref[...]
    kv_scr[w:, :] = kv_ref[...]

    row = lax.broadcasted_iota(I32, (w, 2 * w), 0)
    col = lax.broadcasted_iota(I32, (w, 2 * w), 1)
    rel = row + w - col
    band = (rel >= 0) & (rel < w)
    group = B_HEADS // B_KV_HEADS

    ones = jnp.ones((2 * w, HEAD_DIM), BF16)

    def body(n, carry):
        base = pl.multiple_of(n * w, w)
        kvt = kv_scr[pl.ds(base, 2 * w), :]
        qt = q_ref[pl.ds(base, w), :]
        col_min = jnp.where(jnp.logical_and(i == 0, n == 0), w, 0)
        mask = band & (col >= col_min)
        heads = [(kh, gi) for kh in range(B_KV_HEADS) for gi in range(group)]
        hcol = lambda hd: slice(hd * HEAD_DIM, (hd + 1) * HEAD_DIM)
        ss = []
        for kh in range(B_KV_HEADS):
            q4 = jnp.concatenate([qt[:, hcol(kh * group + gi)] for gi in range(group)], axis=0)
            ss.append(_dot_t(q4, kvt[:, hcol(kh)]))
        ms, ps = {}, {}
        for kh, gi in heads:
            s = jnp.where(mask, ss[kh][gi * w:(gi + 1) * w], NEG)
            m = jnp.maximum(jnp.max(s, axis=-1, keepdims=True), sinks_ref[kh * group + gi])
            ms[(kh, gi)] = m
            ps[(kh, gi)] = jnp.exp2(s - m).astype(BF16)
        pvs = []
        for kh in range(B_KV_HEADS):
            vaug = jnp.concatenate([kvt[:, B_KVW + kh * HEAD_DIM:B_KVW + (kh + 1) * HEAD_DIM], ones],
                                   axis=1)
            p4 = jnp.concatenate([ps[(kh, gi)] for gi in range(group)], axis=0)
            pvs.append(_dot(p4, vaug))
        outs = []
        for kh, gi in heads:
            pv = pvs[kh][gi * w:(gi + 1) * w]
            l = pv[:, HEAD_DIM:] + jnp.exp2(sinks_ref[kh * group + gi] - ms[(kh, gi)])
            outs.append(pv[:, :HEAD_DIM] / l)
        o_ref[pl.ds(base, w), :] = jnp.concatenate(outs, axis=1).astype(o_ref.dtype)
        return carry

    lax.fori_loop(0, tq // w, body, 0)


def _swa(p, sinks, batch, seq, *, tq=512):
    nq = seq // tq
    sub = tq // WINDOW
    kern = functools.partial(_swa_kernel, tq=tq)
    return pl.pallas_call(
        kern,
        grid=(batch, nq),
        in_specs=[
            pl.BlockSpec(memory_space=pltpu.SMEM),
            pl.BlockSpec((tq, B_QW), lambda b, i: (b * nq + i, COL_QB // B_QW)),
            pl.BlockSpec((tq, 2 * B_KVW), lambda b, i: (b * nq + i, COL_KB // (2 * B_KVW))),
            pl.BlockSpec((WINDOW, 2 * B_KVW),
                         lambda b, i: (jnp.maximum((b * nq + i) * sub - 1, 0),
                                       COL_KB // (2 * B_KVW))),
        ],
        out_specs=pl.BlockSpec((tq, B_QW), lambda b, i: (b * nq + i, 0)),
        out_shape=jax.ShapeDtypeStruct((batch * seq, B_QW), BF16),
        scratch_shapes=[pltpu.VMEM((tq + WINDOW, 2 * B_KVW), BF16)],
        compiler_params=_params("parallel", "parallel"),
        name="swa",
    )(sinks, p, p, p)


def _stick_kernel(qt_ref, k_ref, vt_ref, o_ref, z_scr, sp_scr, kpad_scr):
    i = pl.program_id(1)
    blk = SB_BLOCK
    hd = HEAD_DIM
    heads = range(C_HEADS)
    hsl = [slice(hh * hd, (hh + 1) * hd) for hh in heads]
    key = lax.broadcasted_iota(I32, (blk, blk), 0)
    qry = lax.broadcasted_iota(I32, (blk, blk), 1)
    causal = key < qry
    ntri = jnp.where(qry >= key, -1.0, 0.0).astype(BF16)
    own0 = pl.multiple_of(i * blk, blk)

    def exp(x):
        return jnp.exp2(x * LOG2E)

    def softplus2(z):
        return jnp.maximum(z, 0.0) + jnp.log(1.0 + jnp.exp2(jnp.abs(z) * -LOG2E))

    def suffix(sp):
        incl = _dot(ntri, sp.astype(BF16))
        return incl, incl[0:1, :]

    def weight(x):
        return exp(x).astype(BF16)

    @pl.when(i == 0)
    def _():
        pad = jnp.zeros((blk, LANES - hd), BF16)
        for n in range(k_ref.shape[0] // blk):
            rows = slice(n * blk, (n + 1) * blk)
            for hh in heads:
                kpad_scr[rows, hh * LANES:hh * LANES + hd] = k_ref[rows, hsl[hh]]
                kpad_scr[rows, hh * LANES + hd:(hh + 1) * LANES] = pad

    ksl = [slice(hh * LANES, (hh + 1) * LANES) for hh in heads]
    qts = [jnp.concatenate([qt_ref[hsl[hh], :], jnp.zeros((LANES - hd, blk), BF16)], axis=0)
           for hh in heads]
    init = []
    for hh in heads:
        z = _dot(kpad_scr[pl.ds(own0, blk), ksl[hh]], qts[hh])
        incl, carry0 = suffix(jnp.where(causal, softplus2(z), 0.0))
        wgt = jnp.where(causal, weight(z + incl), 0.0)
        init.append((carry0, _dot(vt_ref[i, hsl[hh], :], wgt.astype(BF16))))

    def keys(nb):
        return [(hh, b) for hh in heads for b in range(nb)]

    def logits(blocks):
        rows = [pl.ds(pl.multiple_of(j * blk, blk), blk) for j in blocks]
        return {(hh, b): _dot(kpad_scr[rows[b], ksl[hh]], qts[hh])
                for hh, b in keys(len(blocks))}

    def values(blocks, zs, incls):
        ws = {k: weight(zs[k] + incls[k]) for k in zs}
        return {(hh, b): _dot(vt_ref[blocks[b], hsl[hh], :], ws[(hh, b)]) for hh, b in zs}

    def accumulate(state, pvs, incls, nb):
        out = []
        for hh in heads:
            carry, acc = state[hh]
            for b in range(nb):
                acc = acc + pvs[(hh, b)] * exp(carry)
                carry = carry + incls[(hh, b)][0:1, :]
            out.append((carry, acc))
        return tuple(out)

    def single(j, state):
        zs = logits([j])
        incls = {k: suffix(softplus2(zs[k]))[0] for k in zs}
        return accumulate(state, values([j], zs, incls), incls, 1)

    odd = i % 2
    state = lax.fori_loop(0, odd, lambda _, s: single(i - 1, s), tuple(init))
    n_pairs = i // 2
    nearest = i - 1 - odd

    def pair(p):
        return [nearest - 2 * p, nearest - 1 - 2 * p]

    def slot_of(p, k):
        return (p % 2) * len(keys(2)) + keys(2).index(k)

    def stage_in(p):
        zs = logits(pair(p))
        for k in zs:
            z_scr[slot_of(p, k)] = zs[k]
            sp_scr[slot_of(p, k)] = softplus2(zs[k]).astype(BF16)

    def stage_out(p, state, prefetch):
        incls = {k: _dot(ntri, sp_scr[slot_of(p, k)]) for k in keys(2)}
        if prefetch:
            zs_next = logits(pair(p + 1))
        zs = {k: z_scr[slot_of(p, k)] for k in keys(2)}
        pvs = values(pair(p), zs, incls)
        if prefetch:
            for k in zs_next:
                z_scr[slot_of(p + 1, k)] = zs_next[k]
                sp_scr[slot_of(p + 1, k)] = softplus2(zs_next[k]).astype(BF16)
        return accumulate(state, pvs, incls, 2)

    @pl.when(n_pairs > 0)
    def _():
        stage_in(0)

    state = lax.fori_loop(0, n_pairs - 1, lambda p, s: stage_out(p, s, True), state)
    fin = lax.fori_loop(0, jnp.minimum(n_pairs, 1),
                        lambda _, s: stage_out(n_pairs - 1, s, False), state)
    out_t = jnp.concatenate([acc for _, acc in fin], axis=0)
    o_ref[...] = out_t.T.astype(o_ref.dtype)


def _blocks_t(p, col, width, batch, seq, blk):
    t = p[:, col:col + width].reshape(batch, seq // blk, blk, width)
    return jnp.swapaxes(t, 2, 3)


def _stick(p, batch, seq):
    nblk = seq // SB_BLOCK
    qt = _blocks_t(p, COL_QC, C_W, batch, seq, SB_BLOCK)
    vt = _blocks_t(p, COL_VC, C_W, batch, seq, SB_BLOCK)
    return pl.pallas_call(
        _stick_kernel,
        grid=(batch, nblk),
        in_specs=[
            pl.BlockSpec((None, None, C_W, SB_BLOCK), lambda b, i: (b, i, 0, 0)),
            pl.BlockSpec((seq, C_W), lambda b, i: (b, COL_KC // C_W)),
            pl.BlockSpec((None, nblk, C_W, SB_BLOCK), lambda b, i: (b, 0, 0, 0)),
        ],
        out_specs=pl.BlockSpec((SB_BLOCK, C_W), lambda b, i: (b * nblk + i, 0)),
        out_shape=jax.ShapeDtypeStruct((batch * seq, C_W), BF16),
        scratch_shapes=[pltpu.VMEM((4 * C_HEADS, SB_BLOCK, SB_BLOCK), F32),
                        pltpu.VMEM((4 * C_HEADS, SB_BLOCK, SB_BLOCK), BF16),
                        pltpu.VMEM((seq, C_HEADS * LANES), BF16)],
        compiler_params=_params("parallel", "arbitrary"),
        name="stick",
    )(qt, p, vt)


def _sigmoid(x):
    return 1.0 / (1.0 + jnp.exp(-x))


def _merge_kernel(x_ref, oa_ref, ob_ref, oc_ref, ga_ref, gb_ref, gc_ref,
                  wa_ref, wb_ref, wc_ref, wo_ref, o_ref):
    mixed = _sigmoid(ga_ref[...].astype(F32)) * _dot(oa_ref[...], wa_ref[...])
    mixed = mixed + _sigmoid(gb_ref[...].astype(F32)) * _dot(ob_ref[...], wb_ref[...])
    mixed = mixed + _sigmoid(gc_ref[...].astype(F32)) * _dot(oc_ref[...], wc_ref[...])
    o_ref[...] = x_ref[...] + _dot(mixed.astype(BF16), wo_ref[...])


def _merge(x, oa, ob, oc, p, wa, wb, wc, wo, *, tm=512):
    t, d = x.shape
    full = lambda a: pl.BlockSpec(a.shape, lambda i: (0, 0))
    return pl.pallas_call(
        _merge_kernel,
        grid=(t // tm,),
        in_specs=[
            pl.BlockSpec((tm, d), lambda i: (i, 0)),
            pl.BlockSpec((tm, A_W), lambda i: (i, 0)),
            pl.BlockSpec((tm, B_QW), lambda i: (i, 0)),
            pl.BlockSpec((tm, C_W), lambda i: (i, 0)),
            pl.BlockSpec((tm, d), lambda i: (i, COL_GA // D_MODEL)),
            pl.BlockSpec((tm, d), lambda i: (i, COL_GB // D_MODEL)),
            pl.BlockSpec((tm, d), lambda i: (i, COL_GC // D_MODEL)),
            full(wa), full(wb), full(wc), full(wo),
        ],
        out_specs=pl.BlockSpec((tm, d), lambda i: (i, 0)),
        out_shape=jax.ShapeDtypeStruct((t, d), F32),
        compiler_params=_params("parallel"),
        name="merge",
    )(x, oa, ob, oc, p, p, p, wa, wb, wc, wo)


def _cross_kernel(x_ref, g_ref, wq_ref, kv_ref, wo_ref, o_ref, *rows_ref):
    x = x_ref[...]
    h = _rms(x, g_ref[...]).astype(BF16)
    q = (_dot(h, wq_ref[...]) * (X_HEAD_DIM ** -0.5)).astype(BF16)
    outs = []
    for hd in range(X_HEADS):
        hs = slice(hd * X_HEAD_DIM, (hd + 1) * X_HEAD_DIM)
        s = _dot_t(q[:, hs], kv_ref[:, hs])
        m = jnp.max(s, axis=-1, keepdims=True)
        p = jnp.exp(s - m)
        l = jnp.sum(p, axis=-1, keepdims=True)
        vs = slice(X_W + hd * X_HEAD_DIM, X_W + (hd + 1) * X_HEAD_DIM)
        outs.append(_dot(p.astype(BF16), kv_ref[:, vs]) / l)
    att = jnp.concatenate(outs, axis=1).astype(BF16)
    o_ref[...] = x + _dot(att, wo_ref[...])
    if rows_ref:
        tm, d = o_ref.shape
        for c in range(d // LANES):
            rows_ref[0][pl.ds(c, tm, stride=SUBLANES), :] = o_ref[:, c * LANES:(c + 1) * LANES]


def _cross(x, gain, wq, kv, wo, seq, *, tm=512, with_row_tiles=False):
    t, d = x.shape
    per_batch = seq // tm
    full = lambda a: pl.BlockSpec(a.shape, lambda i: (0, 0))
    out_specs = [pl.BlockSpec((tm, d), lambda i: (i, 0))]
    out_shape = [jax.ShapeDtypeStruct((t, d), F32)]
    if with_row_tiles:
        out_specs.append(pl.BlockSpec((tm * SUBLANES, LANES), lambda i: (i, 0)))
        out_shape.append(jax.ShapeDtypeStruct((t * SUBLANES, LANES), F32))
    outs = pl.pallas_call(
        _cross_kernel,
        grid=(t // tm,),
        in_specs=[
            pl.BlockSpec((tm, d), lambda i: (i, 0)),
            full(gain), full(wq),
            pl.BlockSpec((MEM_LEN, 2 * X_W), lambda i: (i // per_batch, 0)),
            full(wo),
        ],
        out_specs=out_specs,
        out_shape=out_shape,
        compiler_params=_params("parallel"),
        name="cross",
    )(x, gain, wq, kv, wo)
    return outs if with_row_tiles else outs[0]


def _silu(x):
    return x * _sigmoid(x)


def _ffn_kernel(x_ref, g_ref, wg_ref, wu_ref, wd_ref, o_ref, h_scr, acc_scr):
    f = pl.program_id(1)

    @pl.when(f == 0)
    def _():
        h_scr[...] = _rms(x_ref[...], g_ref[...]).astype(BF16)
        acc_scr[...] = x_ref[...]

    h = h_scr[...]
    a = _silu(_dot(h, wg_ref[...])) * _dot(h, wu_ref[...])
    acc_scr[...] += _dot(a.astype(BF16), wd_ref[...])

    @pl.when(f == pl.num_programs(1) - 1)
    def _():
        o_ref[...] = acc_scr[...]


def _ffn(x, gain, wg, wu, wd, *, tm=1024, tf=256):
    t, d = x.shape
    ff = wd.shape[0]
    return pl.pallas_call(
        _ffn_kernel,
        grid=(t // tm, ff // tf),
        in_specs=[
            pl.BlockSpec((tm, d), lambda i, f: (i, 0)),
            pl.BlockSpec((1, d), lambda i, f: (0, 0)),
            pl.BlockSpec((d, tf), lambda i, f: (0, f)),
            pl.BlockSpec((d, tf), lambda i, f: (0, f)),
            pl.BlockSpec((tf, d), lambda i, f: (f, 0)),
        ],
        out_specs=pl.BlockSpec((tm, d), lambda i, f: (i, 0)),
        out_shape=jax.ShapeDtypeStruct((t, d), F32),
        scratch_shapes=[pltpu.VMEM((tm, d), BF16), pltpu.VMEM((tm, d), F32)],
        compiler_params=_params("parallel", "arbitrary"),
        name="ffn",
    )(x, gain, wg, wu, wd)


def _router_kernel(x_ref, g_ref, wr_ref, o_ref):
    h = _rms(x_ref[...], g_ref[...])
    h_hi, h_lo = _split_bf16(h)
    w_hi, w_lo = _split_bf16(wr_ref[...])
    logits = _dot(h_hi, w_hi) + (_dot(h_hi, w_lo) + _dot(h_lo, w_hi))
    lane = lax.broadcasted_iota(I32, logits.shape, 1)
    logits = jnp.where(lane < N_EXPERTS, logits, NEG)
    v1 = jnp.max(logits, axis=-1, keepdims=True)
    i1 = jnp.min(jnp.where(logits == v1, lane, LANES), axis=-1, keepdims=True)
    rest = jnp.where(lane == i1, NEG, logits)
    v2 = jnp.max(rest, axis=-1, keepdims=True)
    i2 = jnp.min(jnp.where(rest == v2, lane, LANES), axis=-1, keepdims=True)
    e = jnp.exp(v2 - v1)
    w1 = 1.0 / (1.0 + e)
    w2 = e / (1.0 + e)
    out = jnp.where(lane == 0, i1.astype(F32), 0.0)
    out = jnp.where(lane == 1, i2.astype(F32), out)
    out = jnp.where(lane == 2, w1, out)
    out = jnp.where(lane == 3, w2, out)
    o_ref[...] = out


def _router(x, gain, wr, *, tm=1024):
    t, d = x.shape
    return pl.pallas_call(
        _router_kernel,
        grid=(t // tm,),
        in_specs=[
            pl.BlockSpec((tm, d), lambda i: (i, 0)),
            pl.BlockSpec((1, d), lambda i: (0, 0)),
            pl.BlockSpec((d, LANES), lambda i: (0, 0)),
        ],
        out_specs=pl.BlockSpec((tm, LANES), lambda i: (i, 0)),
        out_shape=jax.ShapeDtypeStruct((t, LANES), F32),
        compiler_params=_params("parallel"),
        name="router",
    )(x, gain, wr)


def _moe_kernel(tok_ref, dst_ref, texp_ref, tval_ref, x_hbm, g_ref, wg_ref, wu_ref, wd_ref,
                y_hbm, xbuf, obuf, h_scr, acc_scr, gsem, ssem, *, tm, nf):
    m = pl.program_id(0)
    f = pl.program_id(1)
    last_phase = pl.num_programs(0) - 1
    slot = m % 2
    other = 1 - slot
    valid = tval_ref[m] > 0
    per_step = (tm // nf) // 8 * 8
    in_steps = per_step * nf

    def row_tile(r):
        return pl.ds(pl.multiple_of(r * SUBLANES, SUBLANES), SUBLANES)

    def gather_row(tile, r, s):
        tok = pl.multiple_of(tok_ref[tile * tm + r], SUBLANES)
        return pltpu.make_async_copy(x_hbm.at[pl.ds(tok, SUBLANES), :], xbuf.at[s, row_tile(r), :],
                                     gsem.at[s])

    def scatter_row(phase, r, s):
        dst = pl.multiple_of(dst_ref[phase * tm + r], SUBLANES)
        return pltpu.make_async_copy(obuf.at[s, row_tile(r), :], y_hbm.at[pl.ds(dst, SUBLANES), :],
                                     ssem.at[s])

    def chunk(c):
        return pl.ds(c, tm, stride=SUBLANES)

    wait_group = 16

    def gather_wait(tile, s):
        def body(it, c):
            for u in range(wait_group):
                gather_row(tile, it * wait_group + u, s).wait()
            return c
        lax.fori_loop(0, tm // wait_group, body, 0)

    def scatter_wait(phase, s):
        def body(it, c):
            for u in range(wait_group):
                scatter_row(phase, it * wait_group + u, s).wait()
            return c
        lax.fori_loop(0, tm // wait_group, body, 0)

    next_valid = tval_ref[m + 1] > 0
    prev_valid = jnp.logical_and(m >= 1, tval_ref[jnp.maximum(m - 1, 0)] > 0)
    prev2_valid = jnp.logical_and(m >= 2, tval_ref[jnp.maximum(m - 2, 0)] > 0)
    pad_only = jnp.logical_and(jnp.logical_not(valid), m < last_phase)

    def pad_fill():
        first = pl.multiple_of(dst_ref[(m + 1) * tm], SUBLANES)
        return pltpu.make_async_copy(obuf.at[slot], y_hbm.at[pl.ds(first, tm * SUBLANES), :],
                                     ssem.at[slot])

    @pl.when(jnp.logical_and(f == 0, prev2_valid))
    def _():
        scatter_wait(m - 1, slot)

    @pl.when(jnp.logical_and(f == 0, pad_only))
    def _():
        obuf[slot] = jnp.zeros(obuf.shape[1:], obuf.dtype)
        pad_fill().start()

    @pl.when(jnp.logical_and(f == 0, next_valid))
    def _():
        for r in range(in_steps, tm):
            gather_row(m + 1, r, other).start(priority=r % 2)

    @pl.when(jnp.logical_and(f == 0, prev_valid))
    def _():
        for r in range(in_steps, tm):
            scatter_row(m, r, other).start(priority=r % 2)

    @pl.when(jnp.logical_and(f == 0, valid))
    def _():
        @pl.when(m == 0)
        def _():
            def first(r, c):
                gather_row(0, r, 0).start()
                return c
            lax.fori_loop(0, tm, first, 0, unroll=8)

        gather_wait(m, slot)
        n_chunks = h_scr.shape[1] // LANES
        xs = [xbuf[slot, chunk(c), :] for c in range(n_chunks)]
        ssq = xs[0] * xs[0]
        for xc in xs[1:]:
            ssq = ssq + xc * xc
        inv = lax.rsqrt(jnp.sum(ssq, axis=-1, keepdims=True) / h_scr.shape[1] + EPS)
        for c in range(n_chunks):
            lanes = slice(c * LANES, (c + 1) * LANES)
            h_scr[:, lanes] = (xs[c] * inv * g_ref[:, lanes]).astype(BF16)
        acc_scr[...] = jnp.zeros_like(acc_scr)

    def expert_step():
        h = h_scr[...]
        a = _silu(_dot(h, wg_ref[...].astype(BF16))) * _dot(h, wu_ref[...].astype(BF16))
        acc_scr[...] += _dot(a.astype(BF16), wd_ref[...].astype(BF16))

    def gather_rows():
        r0 = pl.multiple_of(f * per_step, 8)
        for rr in range(per_step):
            gather_row(m + 1, r0 + rr, other).start(priority=rr % 2)

    def scatter_rows():
        r0 = pl.multiple_of(f * per_step, 8)
        for rr in range(per_step):
            scatter_row(m, r0 + rr, other).start(priority=rr % 2)

    interior = jnp.logical_and(valid, jnp.logical_and(next_valid, prev_valid))
    edge = jnp.logical_not(interior)

    @pl.when(interior)
    def _():
        gather_rows()
        scatter_rows()
        expert_step()

    @pl.when(jnp.logical_and(edge, valid))
    def _():
        expert_step()

    @pl.when(jnp.logical_and(edge, next_valid))
    def _():
        gather_rows()

    @pl.when(jnp.logical_and(edge, prev_valid))
    def _():
        scatter_rows()

    @pl.when(jnp.logical_and(f == nf - 1, valid))
    def _():
        for c in range(acc_scr.shape[1] // LANES):
            obuf[slot, chunk(c), :] = acc_scr[:, c * LANES:(c + 1) * LANES]

    @pl.when(jnp.logical_and(f == nf - 1, pad_only))
    def _():
        pad_fill().wait()

    @pl.when(jnp.logical_and(jnp.logical_and(f == nf - 1, m == last_phase), prev_valid))
    def _():
        scatter_wait(m, other)


def _moe_experts(x_rows, gain, wg, wu, wd, tok, dst, tile_expert, tile_valid, n_rows, *, tm, tf):
    d = gain.shape[1]
    assert d == SUBLANES * LANES
    nf = wg.shape[2] // tf
    n_phases = tile_expert.shape[0]
    kern = functools.partial(_moe_kernel, tm=tm, nf=nf)

    def fsel(m, f, tval):
        return jnp.where(tval[m] > 0, f, nf - 1)

    grid_spec = pltpu.PrefetchScalarGridSpec(
        num_scalar_prefetch=4,
        grid=(n_phases, nf),
        in_specs=[
            pl.BlockSpec(memory_space=pl.ANY),
            pl.BlockSpec((1, d), lambda m, f, tok, dst, texp, tval: (0, 0)),
            pl.BlockSpec((None, d, tf),
                         lambda m, f, tok, dst, texp, tval: (texp[m], 0, fsel(m, f, tval))),
            pl.BlockSpec((None, d, tf),
                         lambda m, f, tok, dst, texp, tval: (texp[m], 0, fsel(m, f, tval))),
            pl.BlockSpec((None, tf, d),
                         lambda m, f, tok, dst, texp, tval: (texp[m], fsel(m, f, tval), 0)),
        ],
        out_specs=pl.BlockSpec(memory_space=pl.ANY),
        scratch_shapes=[
            pltpu.VMEM((2, tm * SUBLANES, LANES), F32),
            pltpu.VMEM((2, tm * SUBLANES, LANES), F32),
            pltpu.VMEM((tm, d), BF16),
            pltpu.VMEM((tm, d), F32),
            pltpu.SemaphoreType.DMA((2,)),
            pltpu.SemaphoreType.DMA((2,)),
        ],
    )
    return pl.pallas_call(
        kern,
        grid_spec=grid_spec,
        out_shape=jax.ShapeDtypeStruct((n_rows * SUBLANES, LANES), F32),
        compiler_params=_params("arbitrary", "arbitrary"),
        name="moe_experts",
    )(tok, dst, tile_expert, tile_valid, x_rows, gain, wg, wu, wd)


def _combine_kernel(x_ref, r_ref, y0_ref, y1_ref, g_ref, o_ref, *, out_norm):
    tm, d = x_ref.shape
    route = r_ref[...]
    w0 = route[:, TOP_K:TOP_K + 1]
    w1 = route[:, TOP_K + 1:TOP_K + 2]
    for c in range(d // LANES):
        lanes = slice(c * LANES, (c + 1) * LANES)
        rows = pl.ds(c, tm, stride=SUBLANES)
        o_ref[:, lanes] = x_ref[:, lanes] + w0 * y0_ref[rows, :] + w1 * y1_ref[rows, :]
    if out_norm:
        o_ref[...] = _rms(o_ref[...], g_ref[...])


def _moe_combine(x, route, y_rows, out_gain, *, tm=512):
    t, d = x.shape
    assert TOP_K == 2
    out_norm = out_gain is not None
    gain = out_gain if out_norm else jnp.ones((1, d), F32)
    return pl.pallas_call(
        functools.partial(_combine_kernel, out_norm=out_norm),
        grid=(t // tm,),
        in_specs=[
            pl.BlockSpec((tm, d), lambda i: (i, 0)),
            pl.BlockSpec((tm, LANES), lambda i: (i, 0)),
            pl.BlockSpec((tm * SUBLANES, LANES), lambda i: (i, 0)),
            pl.BlockSpec((tm * SUBLANES, LANES), lambda i: (t // tm + i, 0)),
            pl.BlockSpec((1, d), lambda i: (0, 0)),
        ],
        out_specs=pl.BlockSpec((tm, d), lambda i: (i, 0)),
        out_shape=jax.ShapeDtypeStruct((t, d), F32),
        compiler_params=_params("parallel"),
        name="moe_combine",
    )(x, route, y_rows, y_rows, gain)


def _moe(x, x_rows, gain, w_router, wg, wu, wd, out_gain=None, *, tm=MOE_TM, tf=MOE_TF):
    t, d = x.shape
    wr = jnp.zeros((d, LANES), F32).at[:, :N_EXPERTS].set(w_router)
    route = _router(x, gain, wr)
    tok, dst, tile_expert, tile_valid, n_rows = _route_metadata(route[:, :TOP_K].astype(I32), tm)
    y_rows = _moe_experts(x_rows, gain, wg, wu, wd, tok * SUBLANES, dst * SUBLANES,
                          tile_expert, tile_valid, n_rows, tm=tm, tf=tf)
    return _moe_combine(x, route, y_rows, out_gain)


def _route_metadata(experts, tm):
    t = experts.shape[0]
    n_slots = TOP_K * t
    eflat = experts.T.reshape(-1)
    onehot = (eflat[:, None] == jnp.arange(N_EXPERTS, dtype=I32)[None, :]).astype(I32)
    csum = jnp.cumsum(onehot, axis=0)
    rank = jnp.sum((csum - 1) * onehot, axis=1)
    counts = csum[-1]
    tiles_per = (counts + tm - 1) // tm
    tile_end = jnp.cumsum(tiles_per)
    start = (tile_end - tiles_per) * tm
    pos = (jnp.sum(start[None, :] * onehot, axis=1) + rank).astype(I32)
    n_tiles = n_slots // tm + N_EXPERTS
    n_rows_pad = n_tiles * tm
    slot_of_row = jnp.full((n_rows_pad,), -1, I32).at[pos].set(jnp.arange(n_slots, dtype=I32))
    real = slot_of_row >= 0
    tok = jnp.concatenate([jnp.where(real, slot_of_row % t, 0), jnp.zeros((2 * tm,), I32)])
    dump = n_slots + jnp.cumsum(jnp.logical_not(real).astype(I32)) - 1
    dst = jnp.concatenate([jnp.zeros((tm,), I32), jnp.where(real, slot_of_row, dump)])
    tile_id = jnp.arange(n_tiles + 2, dtype=I32)
    tile_expert = jnp.sum((tile_id[:, None] >= tile_end[None, :]).astype(I32), axis=1)
    tile_valid = (tile_id < tile_end[-1]).astype(I32)
    last_expert = jnp.max(jnp.where(counts > 0, jnp.arange(N_EXPERTS, dtype=I32), 0))
    tile_expert = jnp.where(tile_valid > 0, tile_expert, last_expert).astype(I32)
    return tok, dst.astype(I32), tile_expert[:n_tiles + 1], tile_valid, n_rows_pad


def _final_norm_kernel(x_ref, g_ref, o_ref):
    o_ref[...] = _rms(x_ref[...], g_ref[...])


def _final_norm(x, gain, *, tm=1024):
    t, d = x.shape
    return pl.pallas_call(
        _final_norm_kernel,
        grid=(t // tm,),
        in_specs=[pl.BlockSpec((tm, d), lambda i: (i, 0)), pl.BlockSpec((1, d), lambda i: (0, 0))],
        out_specs=pl.BlockSpec((tm, d), lambda i: (i, 0)),
        out_shape=jax.ShapeDtypeStruct((t, d), F32),
        compiler_params=_params("parallel"),
        name="final_norm",
    )(x, gain)


def _rope_tables(seq):
    half = HEAD_DIM // 2
    inv_freq = ROPE_THETA ** (-jnp.arange(half, dtype=F32) / half)
    ang = jnp.arange(seq, dtype=F32)[:, None] * inv_freq[None, :]
    cos, sin = jnp.cos(ang), jnp.sin(ang)
    reps = LANES // HEAD_DIM
    cos_t = jnp.tile(jnp.concatenate([cos, cos], axis=1), (1, reps))
    sin_t = jnp.tile(jnp.concatenate([-sin, sin], axis=1), (1, reps))
    return cos_t, sin_t


def _in_proj_weight(w_in):
    o = 0
    offs = {}
    for name, width in (("qa", A_W), ("ka", A_W), ("va", A_W), ("qb", B_QW), ("kb", B_KVW),
                        ("vb", B_KVW), ("qc", C_W), ("kc", C_W), ("vc", C_W), ("g", 3 * D_MODEL)):
        offs[name] = (o, o + width)
        o += width
    q_scale = {"qa": HEAD_DIM ** -0.5 * LOG2E, "qb": HEAD_DIM ** -0.5 * LOG2E,
               "qc": HEAD_DIM ** -0.5}
    parts = []
    for name in ("g", "qa", "ka", "qb", "kb", "vb", "va", "qc", "kc", "vc"):
        lo, hi = offs[name]
        blk = w_in[:, lo:hi]
        if name in q_scale:
            blk = blk * q_scale[name]
        parts.append(blk)
    return jnp.concatenate(parts, axis=1).astype(BF16)


def _rope_group_map():
    groups = {}
    for t in range(IN_W // PROJ_TN):
        lo, hi = t * PROJ_TN, (t + 1) * PROJ_TN
        n = (min(hi, ROPE_HI) - max(lo, ROPE_LO)) // LANES
        if n > 0:
            assert max(lo, ROPE_LO) == lo
            groups[t] = n
    return groups


def kernel(x, mem, norm_mix, w_in, w_proj_a, w_proj_b, w_proj_c, w_mix_out, sinks, norm_cross,
           norm_mem, w_xq, w_xkv, w_xo, norm_ffn, ffn_gate, ffn_up, ffn_down, moe_router,
           moe_gate, moe_up, moe_down, final_norm):
    batch, seq, d = x.shape
    depth = norm_mix.shape[0]
    assert d == D_MODEL and seq % PROJ_TM == 0 and mem.shape[1] == MEM_LEN
    t = batch * seq
    xf = x.reshape(t, d)
    memf = mem.reshape(batch * MEM_LEN, d)
    cos_t, sin_t = _rope_tables(seq)
    rope_groups = _rope_group_map()
    ones_tab = jnp.ones((MEM_LEN, LANES), F32)

    for l in range(depth):
        gain = lambda g: g[l].reshape(1, d)
        p = _norm_proj(xf, gain(norm_mix), _in_proj_weight(w_in[l]), cos_t, sin_t, seq,
                       tm=PROJ_TM, tn=PROJ_TN, rope_groups=rope_groups)
        oa = _moba(p, batch, seq)
        ob = _swa(p, sinks[l] * LOG2E, batch, seq)
        oc = _stick(p, batch, seq)
        xf = _merge(xf, oa, ob, oc, p, w_proj_a[l].astype(BF16), w_proj_b[l].astype(BF16),
                    w_proj_c[l].astype(BF16), w_mix_out[l].astype(BF16))
        kv = _norm_proj(memf, gain(norm_mem), w_xkv[l].astype(BF16), ones_tab, ones_tab, MEM_LEN,
                        tm=MEM_LEN, tn=2 * X_W, rope_groups={})
        cross_args = (gain(norm_cross), w_xq[l].astype(BF16), kv, w_xo[l].astype(BF16), seq)
        if l % 2 == 0:
            i = l // 2
            xf = _cross(xf, *cross_args)
            xf = _ffn(xf, gain(norm_ffn), ffn_gate[i].astype(BF16), ffn_up[i].astype(BF16),
                      ffn_down[i].astype(BF16), tf=FFN_TF)
        else:
            i = l // 2
            xf, x_rows = _cross(xf, *cross_args, with_row_tiles=True)
            out_gain = final_norm.reshape(1, d) if l == depth - 1 else None
            xf = _moe(xf, x_rows, gain(norm_ffn), moe_router[i], moe_gate[i], moe_up[i],
                      moe_down[i], out_gain)
    if depth % 2 == 1:
        xf = _final_norm(xf, final_norm.reshape(1, d))
    return xf.reshape(batch, seq, d)
```

```python
import functools

import jax
import jax.numpy as jnp
from jax import lax
from jax.experimental import pallas as pl
from jax.experimental.pallas import tpu as pltpu

F32 = jnp.float32
BF16 = jnp.bfloat16
I32 = jnp.int32

D_MODEL = 1024
HEAD_DIM = 64
A_HEADS = 4
MOBA_BLOCK = 256
MOBA_TOPK = 3
B_HEADS = 8
B_KV_HEADS = 2
WINDOW = 128
C_HEADS = 4
SB_BLOCK = 256
MEM_LEN = 256
X_HEADS = 4
X_HEAD_DIM = 128
N_EXPERTS = 8
TOP_K = 2
ROPE_THETA = 10000.0
EPS = 1e-6

A_W = A_HEADS * HEAD_DIM
B_QW = B_HEADS * HEAD_DIM
B_KVW = B_KV_HEADS * HEAD_DIM
C_W = C_HEADS * HEAD_DIM
X_W = X_HEADS * X_HEAD_DIM
QKV_W = 3 * A_W + B_QW + 2 * B_KVW + 3 * C_W
IN_W = QKV_W + 3 * D_MODEL

LANES = 128
SUBLANES = 8
NEG = -1e30
LOG2E = 1.4426950408889634

COL_GA, COL_GB, COL_GC = 0, D_MODEL, 2 * D_MODEL
COL_QA = 3 * D_MODEL
COL_KA = COL_QA + A_W
COL_QB = COL_KA + A_W
COL_KB = COL_QB + B_QW
COL_VB = COL_KB + B_KVW
COL_VA = COL_VB + B_KVW
COL_QC = COL_VA + A_W
COL_KC = COL_QC + C_W
COL_VC = COL_KC + C_W
ROPE_LO, ROPE_HI = COL_QA, COL_VB

PROJ_TM = 2048
PROJ_TN = 768
FFN_TF = 256
MOE_TF = 512
MOE_TM = 1072
VMEM_LIMIT = 48 * 1024 * 1024


def _params(*sem):
    return pltpu.CompilerParams(dimension_semantics=sem, vmem_limit_bytes=VMEM_LIMIT)


def _rms(x, g):
    ms = jnp.mean(x * x, axis=-1, keepdims=True)
    return x * lax.rsqrt(ms + EPS) * g


def _dot(a, b):
    return jnp.dot(a, b, preferred_element_type=F32)


def _dot_t(a, b):
    return lax.dot_general(a, b, (((1,), (1,)), ((), ())), preferred_element_type=F32)


def _split_bf16(v):
    hi = v.astype(BF16)
    lo = (v - hi.astype(F32)).astype(BF16)
    return hi, lo


def _norm_proj_kernel(x_ref, g_ref, w_ref, cos_ref, sin_ref, o_ref, h_scr, *, rope_groups):
    j = pl.program_id(1)

    @pl.when(j == 0)
    def _():
        h_scr[...] = _rms(x_ref[...], g_ref[...]).astype(BF16)

    acc = _dot(h_scr[...], w_ref[...])
    n_groups = acc.shape[1] // LANES

    def rope(y):
        lane = lax.broadcasted_iota(I32, y.shape, 1)
        first_half = (lane % HEAD_DIM) < (HEAD_DIM // 2)
        sw = jnp.where(first_half, pltpu.roll(y, LANES - HEAD_DIM // 2, 1),
                       pltpu.roll(y, HEAD_DIM // 2, 1))
        return y * cos_ref[...] + sw * sin_ref[...]

    roped = sorted(rope_groups)

    for t in roped:
        @pl.when(j == t)
        def _(t=t):
            for gi in range(n_groups):
                y = acc[:, gi * LANES:(gi + 1) * LANES]
                if gi < rope_groups[t]:
                    y = rope(y)
                o_ref[:, gi * LANES:(gi + 1) * LANES] = y.astype(o_ref.dtype)

    is_plain = j >= 0
    for t in roped:
        is_plain = jnp.logical_and(is_plain, j != t)

    @pl.when(is_plain)
    def _():
        o_ref[...] = acc.astype(o_ref.dtype)


def _norm_proj(x, gain, w, cos, sin, seq, *, tm, tn, rope_groups):
    t, d = x.shape
    n = w.shape[1]
    pos_blocks = seq // tm
    kern = functools.partial(_norm_proj_kernel, rope_groups=rope_groups)
    return pl.pallas_call(
        kern,
        grid=(t // tm, n // tn),
        in_specs=[
            pl.BlockSpec((tm, d), lambda i, j: (i, 0)),
            pl.BlockSpec((1, d), lambda i, j: (0, 0)),
            pl.BlockSpec((d, tn), lambda i, j: (0, j)),
            pl.BlockSpec((tm, LANES), lambda i, j: (i % pos_blocks, 0)),
            pl.BlockSpec((tm, LANES), lambda i, j: (i % pos_blocks, 0)),
        ],
        out_specs=pl.BlockSpec((tm, tn), lambda i, j: (i, j)),
        out_shape=jax.ShapeDtypeStruct((t, n), BF16),
        scratch_shapes=[pltpu.VMEM((tm, d), BF16)],
        compiler_params=_params("parallel", "arbitrary"),
        name="norm_proj",
    )(x, gain, w, cos, sin)


MOBA_VROWS = HEAD_DIM + 16


def _moba_kernel(qt_ref, k_ref, vt_ref, o_ref, kmean_scr, kaug_scr, vaug_scr, qaug_scr, *, nblk):
    i = pl.program_id(1)
    blk = MOBA_BLOCK
    hd = HEAD_DIM
    aug = 2 * hd
    vr = MOBA_VROWS
    heads = range(A_HEADS)
    hsl = [slice(hh * hd, (hh + 1) * hd) for hh in heads]

    @pl.when(i == 0)
    def _():
        ones = jnp.ones((vr - hd, blk), BF16)
        blk_lane = lax.broadcasted_iota(I32, (blk, hd), 1)
        for n in range(nblk):
            rows = slice(n * blk, (n + 1) * blk)
            kmean_scr[n:n + 1, :] = jnp.mean(k_ref[rows, :].astype(F32), axis=0, keepdims=True)
            onehot = jnp.where(blk_lane == n, 1.0, 0.0).astype(BF16)
            for hh in heads:
                kaug_scr[rows, hh * aug:hh * aug + hd] = k_ref[rows, hsl[hh]]
                kaug_scr[rows, hh * aug + hd:(hh + 1) * aug] = onehot
                vaug_scr[n, hh * vr:hh * vr + hd, :] = vt_ref[n, hsl[hh], :]
                vaug_scr[n, hh * vr + hd:(hh + 1) * vr, :] = ones

    key = lax.broadcasted_iota(I32, (blk, blk), 0)
    qry = lax.broadcasted_iota(I32, (blk, blk), 1)
    causal = key <= qry
    blk_id = lax.broadcasted_iota(I32, (nblk, blk), 0)
    own0 = pl.multiple_of(i * blk, blk)

    init = []
    for hh in heads:
        qt = qt_ref[hsl[hh], :]
        km_hi, km_lo = _split_bf16(kmean_scr[:, hsl[hh]])
        g = _dot(km_hi, qt) + _dot(km_lo, qt)

        cnt = jnp.zeros((nblk, blk), I32)
        for m in range(nblk):
            gm = g[m:m + 1, :]
            beats = (gm > g) | ((gm == g) & (m < blk_id))
            cnt = cnt + jnp.where(beats, (m < i).astype(I32), 0)
        sel = (cnt < MOBA_TOPK) & (blk_id < i)
        qaug_scr[hh * aug:hh * aug + hd, :] = qt
        qaug_scr[hh * aug + hd:hh * aug + hd + nblk, :] = jnp.where(sel, 0.0, NEG).astype(BF16)
        qaug_scr[hh * aug + hd + nblk:(hh + 1) * aug, :] = jnp.zeros((hd - nblk, blk), BF16)

        s = _dot(k_ref[pl.ds(own0, blk), hsl[hh]], qt)
        s = jnp.where(causal, s, NEG)
        m0 = jnp.max(s, axis=0, keepdims=True)
        p = jnp.exp2(s - m0)
        init.append((m0, _dot(vaug_scr[i, hh * vr:(hh + 1) * vr, :], p.astype(BF16))))

    def step(blocks, carry):
        rows = [pl.ds(pl.multiple_of(j * blk, blk), blk) for j in blocks]
        ss = [[_dot(kaug_scr[r, hh * aug:(hh + 1) * aug], qaug_scr[hh * aug:(hh + 1) * aug, :])
               for r in rows] for hh in heads]
        ms = []
        for hh in heads:
            m_new = carry[hh][0]
            for s in ss[hh]:
                m_new = jnp.maximum(m_new, jnp.max(s, axis=0, keepdims=True))
            ms.append(m_new)
        ps = [[jnp.exp2(s - ms[hh]).astype(BF16) for s in ss[hh]] for hh in heads]
        pvs = []
        for hh in heads:
            pv = None
            for j, p in zip(blocks, ps[hh]):
                d = _dot(vaug_scr[j, hh * vr:(hh + 1) * vr, :], p)
                pv = d if pv is None else pv + d
            pvs.append(pv)
        return tuple((ms[hh], jnp.exp2(carry[hh][0] - ms[hh]) * carry[hh][1] + pvs[hh])
                     for hh in heads)

    fin = lax.fori_loop(0, i // 2, lambda jp, c: step([2 * jp, 2 * jp + 1], c), tuple(init))
    fin = lax.fori_loop(0, i % 2, lambda _, c: step([i - 1], c), fin)
    out_t = jnp.concatenate([acc[:hd, :] / acc[hd:hd + 1, :] for _, acc in fin], axis=0)
    o_ref[...] = out_t.T.astype(o_ref.dtype)


def _moba(p, batch, seq):
    nblk = seq // MOBA_BLOCK
    assert nblk <= HEAD_DIM
    qt = _blocks_t(p, COL_QA, A_W, batch, seq, MOBA_BLOCK)
    vt = _blocks_t(p, COL_VA, A_W, batch, seq, MOBA_BLOCK)
    kern = functools.partial(_moba_kernel, nblk=nblk)
    return pl.pallas_call(
        kern,
        grid=(batch, nblk),
        in_specs=[
            pl.BlockSpec((None, None, A_W, MOBA_BLOCK), lambda b, i: (b, i, 0, 0)),
            pl.BlockSpec((seq, A_W), lambda b, i: (b, COL_KA // A_W)),
            pl.BlockSpec((None, nblk, A_W, MOBA_BLOCK), lambda b, i: (b, 0, 0, 0)),
        ],
        out_specs=pl.BlockSpec((MOBA_BLOCK, A_W), lambda b, i: (b * nblk + i, 0)),
        out_shape=jax.ShapeDtypeStruct((batch * seq, A_W), BF16),
        scratch_shapes=[
            pltpu.VMEM((nblk, A_W), F32),
            pltpu.VMEM((seq, 2 * A_W), BF16),
            pltpu.VMEM((nblk, A_HEADS * MOBA_VROWS, MOBA_BLOCK), BF16),
            pltpu.VMEM((2 * A_W, MOBA_BLOCK), BF16),
        ],
        compiler_params=_params("parallel", "arbitrary"),
        name="moba",
    )(qt, p, vt)


def _swa_kernel(sinks_ref, q_ref, kv_ref, pkv_ref, o_ref, kv_scr, *, tq):
    i = pl.program_id(1)
    w = WINDOW
    kv_scr[0:w, :] = pkv_ref[...]
    kv_scr[w:, :] = kv_ref[...]

    row = lax.broadcasted_iota(I32, (w, 2 * w), 0)
    col = lax.broadcasted_iota(I32, (w, 2 * w), 1)
    rel = row + w - col
    band = (rel >= 0) & (rel < w)
    group = B_HEADS // B_KV_HEADS

    ones = jnp.ones((2 * w, HEAD_DIM), BF16)

    def body(n, carry):
        base = pl.multiple_of(n * w, w)
        kvt = kv_scr[pl.ds(base, 2 * w), :]
        qt = q_ref[pl.ds(base, w), :]
        col_min = jnp.where(jnp.logical_and(i == 0, n == 0), w, 0)
        mask = band & (col >= col_min)
        heads = [(kh, gi) for kh in range(B_KV_HEADS) for gi in range(group)]
        hcol = lambda hd: slice(hd * HEAD_DIM, (hd + 1) * HEAD_DIM)
        ss = []
        for kh in range(B_KV_HEADS):
            q4 = jnp.concatenate([qt[:, hcol(kh * group + gi)] for gi in range(group)], axis=0)
            ss.append(_dot_t(q4, kvt[:, hcol(kh)]))
        ms, ps = {}, {}
        for kh, gi in heads:
            s = jnp.where(mask, ss[kh][gi * w:(gi + 1) * w], NEG)
            m = jnp.maximum(jnp.max(s, axis=-1, keepdims=True), sinks_ref[kh * group + gi])
            ms[(kh, gi)] = m
            ps[(kh, gi)] = jnp.exp2(s - m).astype(BF16)
        pvs = []
        for kh in range(B_KV_HEADS):
            vaug = jnp.concatenate([kvt[:, B_KVW + kh * HEAD_DIM:B_KVW + (kh + 1) * HEAD_DIM], ones],
                                   axis=1)
            p4 = jnp.concatenate([ps[(kh, gi)] for gi in range(group)], axis=0)
            pvs.append(_dot(p4, vaug))
        outs = []
        for kh, gi in heads:
            pv = pvs[kh][gi * w:(gi + 1) * w]
            l = pv[:, HEAD_DIM:] + jnp.exp2(sinks_ref[kh * group + gi] - ms[(kh, gi)])
            outs.append(pv[:, :HEAD_DIM] / l)
        o_ref[pl.ds(base, w), :] = jnp.concatenate(outs, axis=1).astype(o_ref.dtype)
        return carry

    lax.fori_loop(0, tq // w, body, 0)


def _swa(p, sinks, batch, seq, *, tq=512):
    nq = seq // tq
    sub = tq // WINDOW
    kern = functools.partial(_swa_kernel, tq=tq)
    return pl.pallas_call(
        kern,
        grid=(batch, nq),
        in_specs=[
            pl.BlockSpec(memory_space=pltpu.SMEM),
            pl.BlockSpec((tq, B_QW), lambda b, i: (b * nq + i, COL_QB // B_QW)),
            pl.BlockSpec((tq, 2 * B_KVW), lambda b, i: (b * nq + i, COL_KB // (2 * B_KVW))),
            pl.BlockSpec((WINDOW, 2 * B_KVW),
                         lambda b, i: (jnp.maximum((b * nq + i) * sub - 1, 0),
                                       COL_KB // (2 * B_KVW))),
        ],
        out_specs=pl.BlockSpec((tq, B_QW), lambda b, i: (b * nq + i, 0)),
        out_shape=jax.ShapeDtypeStruct((batch * seq, B_QW), BF16),
        scratch_shapes=[pltpu.VMEM((tq + WINDOW, 2 * B_KVW), BF16)],
        compiler_params=_params("parallel", "parallel"),
        name="swa",
    )(sinks, p, p, p)


def _stick_kernel(qt_ref, k_ref, vt_ref, o_ref, z_scr, sp_scr, kpad_scr):
    i = pl.program_id(1)
    blk = SB_BLOCK
    hd = HEAD_DIM
    heads = range(C_HEADS)
    hsl = [slice(hh * hd, (hh + 1) * hd) for hh in heads]
    key = lax.broadcasted_iota(I32, (blk, blk), 0)
    qry = lax.broadcasted_iota(I32, (blk, blk), 1)
    causal = key < qry
    ntri = jnp.where(qry >= key, -1.0, 0.0).astype(BF16)
    own0 = pl.multiple_of(i * blk, blk)

    def exp(x):
        return jnp.exp2(x * LOG2E)

    def softplus2(z):
        return jnp.maximum(z, 0.0) + jnp.log(1.0 + jnp.exp2(jnp.abs(z) * -LOG2E))

    def suffix(sp):
        incl = _dot(ntri, sp.astype(BF16))
        return incl, incl[0:1, :]

    def weight(x):
        return exp(x).astype(BF16)

    @pl.when(i == 0)
    def _():
        pad = jnp.zeros((blk, LANES - hd), BF16)
        for n in range(k_ref.shape[0] // blk):
            rows = slice(n * blk, (n + 1) * blk)
            for hh in heads:
                kpad_scr[rows, hh * LANES:hh * LANES + hd] = k_ref[rows, hsl[hh]]
                kpad_scr[rows, hh * LANES + hd:(hh + 1) * LANES] = pad

    ksl = [slice(hh * LANES, (hh + 1) * LANES) for hh in heads]
    qts = [jnp.concatenate([qt_ref[hsl[hh], :], jnp.zeros((LANES - hd, blk), BF16)], axis=0)
           for hh in heads]
    init = []
    for hh in heads:
        z = _dot(kpad_scr[pl.ds(own0, blk), ksl[hh]], qts[hh])
        incl, carry0 = suffix(jnp.where(causal, softplus2(z), 0.0))
        wgt = jnp.where(causal, weight(z + incl), 0.0)
        init.append((carry0, _dot(vt_ref[i, hsl[hh], :], wgt.astype(BF16))))

    def keys(nb):
        return [(hh, b) for hh in heads for b in range(nb)]

    def logits(blocks):
        rows = [pl.ds(pl.multiple_of(j * blk, blk), blk) for j in blocks]
        return {(hh, b): _dot(kpad_scr[rows[b], ksl[hh]], qts[hh])
                for hh, b in keys(len(blocks))}

    def values(blocks, zs, incls):
        ws = {k: weight(zs[k] + incls[k]) for k in zs}
        return {(hh, b): _dot(vt_ref[blocks[b], hsl[hh], :], ws[(hh, b)]) for hh, b in zs}

    def accumulate(state, pvs, incls, nb):
        out = []
        for hh in heads:
            carry, acc = state[hh]
            for b in range(nb):
                acc = acc + pvs[(hh, b)] * exp(carry)
                carry = carry + incls[(hh, b)][0:1, :]
            out.append((carry, acc))
        return tuple(out)

    def single(j, state):
        zs = logits([j])
        incls = {k: suffix(softplus2(zs[k]))[0] for k in zs}
        return accumulate(state, values([j], zs, incls), incls, 1)

    odd = i % 2
    state = lax.fori_loop(0, odd, lambda _, s: single(i - 1, s), tuple(init))
    n_pairs = i // 2
    nearest = i - 1 - odd

    def pair(p):
        return [nearest - 2 * p, nearest - 1 - 2 * p]

    def slot_of(p, k):
        return (p % 2) * len(keys(2)) + keys(2).index(k)

    def stage_in(p):
        zs = logits(pair(p))
        for k in zs:
            z_scr[slot_of(p, k)] = zs[k]
            sp_scr[slot_of(p, k)] = softplus2(zs[k]).astype(BF16)

    def stage_out(p, state, prefetch):
        incls = {k: _dot(ntri, sp_scr[slot_of(p, k)]) for k in keys(2)}
        if prefetch:
            zs_next = logits(pair(p + 1))
        zs = {k: z_scr[slot_of(p, k)] for k in keys(2)}
        pvs = values(pair(p), zs, incls)
        if prefetch:
            for k in zs_next:
                z_scr[slot_of(p + 1, k)] = zs_next[k]
                sp_scr[slot_of(p + 1, k)] = softplus2(zs_next[k]).astype(BF16)
        return accumulate(state, pvs, incls, 2)

    @pl.when(n_pairs > 0)
    def _():
        stage_in(0)

    state = lax.fori_loop(0, n_pairs - 1, lambda p, s: stage_out(p, s, True), state)
    fin = lax.fori_loop(0, jnp.minimum(n_pairs, 1),
                        lambda _, s: stage_out(n_pairs - 1, s, False), state)
    out_t = jnp.concatenate([acc for _, acc in fin], axis=0)
    o_ref[...] = out_t.T.astype(o_ref.dtype)


def _blocks_t(p, col, width, batch, seq, blk):
    t = p[:, col:col + width].reshape(batch, seq // blk, blk, width)
    return jnp.swapaxes(t, 2, 3)


def _stick(p, batch, seq):
    nblk = seq // SB_BLOCK
    qt = _blocks_t(p, COL_QC, C_W, batch, seq, SB_BLOCK)
    vt = _blocks_t(p, COL_VC, C_W, batch, seq, SB_BLOCK)
    return pl.pallas_call(
        _stick_kernel,
        grid=(batch, nblk),
        in_specs=[
            pl.BlockSpec((None, None, C_W, SB_BLOCK), lambda b, i: (b, i, 0, 0)),
            pl.BlockSpec((seq, C_W), lambda b, i: (b, COL_KC // C_W)),
            pl.BlockSpec((None, nblk, C_W, SB_BLOCK), lambda b, i: (b, 0, 0, 0)),
        ],
        out_specs=pl.BlockSpec((SB_BLOCK, C_W), lambda b, i: (b * nblk + i, 0)),
        out_shape=jax.ShapeDtypeStruct((batch * seq, C_W), BF16),
        scratch_shapes=[pltpu.VMEM((4 * C_HEADS, SB_BLOCK, SB_BLOCK), F32),
                        pltpu.VMEM((4 * C_HEADS, SB_BLOCK, SB_BLOCK), BF16),
                        pltpu.VMEM((seq, C_HEADS * LANES), BF16)],
        compiler_params=_params("parallel", "arbitrary"),
        name="stick",
    )(qt, p, vt)


def _sigmoid(x):
    return 1.0 / (1.0 + jnp.exp(-x))


def _merge_kernel(x_ref, oa_ref, ob_ref, oc_ref, ga_ref, gb_ref, gc_ref,
                  wa_ref, wb_ref, wc_ref, wo_ref, o_ref):
    mixed = _sigmoid(ga_ref[...].astype(F32)) * _dot(oa_ref[...], wa_ref[...])
    mixed = mixed + _sigmoid(gb_ref[...].astype(F32)) * _dot(ob_ref[...], wb_ref[...])
    mixed = mixed + _sigmoid(gc_ref[...].astype(F32)) * _dot(oc_ref[...], wc_ref[...])
    o_ref[...] = x_ref[...] + _dot(mixed.astype(BF16), wo_ref[...])


def _merge(x, oa, ob, oc, p, wa, wb, wc, wo, *, tm=512):
    t, d = x.shape
    full = lambda a: pl.BlockSpec(a.shape, lambda i: (0, 0))
    return pl.pallas_call(
        _merge_kernel,
        grid=(t // tm,),
        in_specs=[
            pl.BlockSpec((tm, d), lambda i: (i, 0)),
            pl.BlockSpec((tm, A_W), lambda i: (i, 0)),
            pl.BlockSpec((tm, B_QW), lambda i: (i, 0)),
            pl.BlockSpec((tm, C_W), lambda i: (i, 0)),
            pl.BlockSpec((tm, d), lambda i: (i, COL_GA // D_MODEL)),
            pl.BlockSpec((tm, d), lambda i: (i, COL_GB // D_MODEL)),
            pl.BlockSpec((tm, d), lambda i: (i, COL_GC // D_MODEL)),
            full(wa), full(wb), full(wc), full(wo),
        ],
        out_specs=pl.BlockSpec((tm, d), lambda i: (i, 0)),
        out_shape=jax.ShapeDtypeStruct((t, d), F32),
        compiler_params=_params("parallel"),
        name="merge",
    )(x, oa, ob, oc, p, p, p, wa, wb, wc, wo)


def _cross_kernel(x_ref, g_ref, wq_ref, kv_ref, wo_ref, o_ref, *rows_ref):
    x = x_ref[...]
    h = _rms(x, g_ref[...]).astype(BF16)
    q = (_dot(h, wq_ref[...]) * (X_HEAD_DIM ** -0.5)).astype(BF16)
    outs = []
    for hd in range(X_HEADS):
        hs = slice(hd * X_HEAD_DIM, (hd + 1) * X_HEAD_DIM)
        s = _dot_t(q[:, hs], kv_ref[:, hs])
        m = jnp.max(s, axis=-1, keepdims=True)
        p = jnp.exp(s - m)
        l = jnp.sum(p, axis=-1, keepdims=True)
        vs = slice(X_W + hd * X_HEAD_DIM, X_W + (hd + 1) * X_HEAD_DIM)
        outs.append(_dot(p.astype(BF16), kv_ref[:, vs]) / l)
    att = jnp.concatenate(outs, axis=1).astype(BF16)
    o_ref[...] = x + _dot(att, wo_ref[...])
    if rows_ref:
        tm, d = o_ref.shape
        for c in range(d // LANES):
            rows_ref[0][pl.ds(c, tm, stride=SUBLANES), :] = o_ref[:, c * LANES:(c + 1) * LANES]


def _cross(x, gain, wq, kv, wo, seq, *, tm=512, with_row_tiles=False):
    t, d = x.shape
    per_batch = seq // tm
    full = lambda a: pl.BlockSpec(a.shape, lambda i: (0, 0))
    out_specs = [pl.BlockSpec((tm, d), lambda i: (i, 0))]
    out_shape = [jax.ShapeDtypeStruct((t, d), F32)]
    if with_row_tiles:
        out_specs.append(pl.BlockSpec((tm * SUBLANES, LANES), lambda i: (i, 0)))
        out_shape.append(jax.ShapeDtypeStruct((t * SUBLANES, LANES), F32))
    outs = pl.pallas_call(
        _cross_kernel,
        grid=(t // tm,),
        in_specs=[
            pl.BlockSpec((tm, d), lambda i: (i, 0)),
            full(gain), full(wq),
            pl.BlockSpec((MEM_LEN, 2 * X_W), lambda i: (i // per_batch, 0)),
            full(wo),
        ],
        out_specs=out_specs,
        out_shape=out_shape,
        compiler_params=_params("parallel"),
        name="cross",
    )(x, gain, wq, kv, wo)
    return outs if with_row_tiles else outs[0]


def _silu(x):
    return x * _sigmoid(x)


def _ffn_kernel(x_ref, g_ref, wg_ref, wu_ref, wd_ref, o_ref, h_scr, acc_scr):
    f = pl.program_id(1)

    @pl.when(f == 0)
    def _():
        h_scr[...] = _rms(x_ref[...], g_ref[...]).astype(BF16)
        acc_scr[...] = x_ref[...]

    h = h_scr[...]
    a = _silu(_dot(h, wg_ref[...])) * _dot(h, wu_ref[...])
    acc_scr[...] += _dot(a.astype(BF16), wd_ref[...])

    @pl.when(f == pl.num_programs(1) - 1)
    def _():
        o_ref[...] = acc_scr[...]


def _ffn(x, gain, wg, wu, wd, *, tm=1024, tf=256):
    t, d = x.shape
    ff = wd.shape[0]
    return pl.pallas_call(
        _ffn_kernel,
        grid=(t // tm, ff // tf),
        in_specs=[
            pl.BlockSpec((tm, d), lambda i, f: (i, 0)),
            pl.BlockSpec((1, d), lambda i, f: (0, 0)),
            pl.BlockSpec((d, tf), lambda i, f: (0, f)),
            pl.BlockSpec((d, tf), lambda i, f: (0, f)),
            pl.BlockSpec((tf, d), lambda i, f: (f, 0)),
        ],
        out_specs=pl.BlockSpec((tm, d), lambda i, f: (i, 0)),
        out_shape=jax.ShapeDtypeStruct((t, d), F32),
        scratch_shapes=[pltpu.VMEM((tm, d), BF16), pltpu.VMEM((tm, d), F32)],
        compiler_params=_params("parallel", "arbitrary"),
        name="ffn",
    )(x, gain, wg, wu, wd)


def _router_kernel(x_ref, g_ref, wr_ref, o_ref, counts_ref, base_scr):
    @pl.when(pl.program_id(0) == 0)
    def _():
        base_scr[...] = jnp.zeros_like(base_scr)

    h = _rms(x_ref[...], g_ref[...])
    h_hi, h_lo = _split_bf16(h)
    w_hi, w_lo = _split_bf16(wr_ref[...])
    logits = _dot(h_hi, w_hi) + (_dot(h_hi, w_lo) + _dot(h_lo, w_hi))
    lane = lax.broadcasted_iota(I32, logits.shape, 1)
    logits = jnp.where(lane < N_EXPERTS, logits, NEG)
    v1 = jnp.max(logits, axis=-1, keepdims=True)
    i1 = jnp.min(jnp.where(logits == v1, lane, LANES), axis=-1, keepdims=True)
    rest = jnp.where(lane == i1, NEG, logits)
    v2 = jnp.max(rest, axis=-1, keepdims=True)
    i2 = jnp.min(jnp.where(rest == v2, lane, LANES), axis=-1, keepdims=True)
    e = jnp.exp(v2 - v1)
    w1 = 1.0 / (1.0 + e)
    w2 = e / (1.0 + e)
    tm = logits.shape[0]
    picked = jnp.where((lane == i1) | (lane == i2), 1.0, 0.0)
    tok_r = lax.broadcasted_iota(I32, (tm, tm), 0)
    tok_c = lax.broadcasted_iota(I32, (tm, tm), 1)
    before = jnp.where(tok_c < tok_r, 1.0, 0.0).astype(BF16)
    rank = _dot(before, picked.astype(BF16)) + base_scr[...]
    r1 = jnp.sum(jnp.where(lane == i1, rank, 0.0), axis=-1, keepdims=True)
    r2 = jnp.sum(jnp.where(lane == i2, rank, 0.0), axis=-1, keepdims=True)
    base_scr[...] += jnp.sum(picked, axis=0, keepdims=True)
    counts_ref[...] = base_scr[...]

    out = jnp.where(lane == 0, i1.astype(F32), 0.0)
    out = jnp.where(lane == 1, i2.astype(F32), out)
    out = jnp.where(lane == 2, w1, out)
    out = jnp.where(lane == 3, w2, out)
    out = jnp.where(lane == 4, r1, out)
    out = jnp.where(lane == 5, r2, out)
    o_ref[...] = out


def _router(x, gain, wr, *, tm=1024):
    t, d = x.shape
    return pl.pallas_call(
        _router_kernel,
        grid=(t // tm,),
        in_specs=[
            pl.BlockSpec((tm, d), lambda i: (i, 0)),
            pl.BlockSpec((1, d), lambda i: (0, 0)),
            pl.BlockSpec((d, LANES), lambda i: (0, 0)),
        ],
        out_specs=[pl.BlockSpec((tm, LANES), lambda i: (i, 0)),
                   pl.BlockSpec((1, LANES), lambda i: (0, 0))],
        out_shape=[jax.ShapeDtypeStruct((t, LANES), F32), jax.ShapeDtypeStruct((1, LANES), F32)],
        scratch_shapes=[pltpu.VMEM((1, LANES), F32)],
        compiler_params=_params("arbitrary"),
        name="router",
    )(x, gain, wr)


def _moe_kernel(tok_ref, dst_ref, texp_ref, tval_ref, x_hbm, g_ref, wg_ref, wu_ref, wd_ref,
                y_hbm, xbuf, obuf, h_scr, acc_scr, gsem, ssem, *, tm, nf):
    m = pl.program_id(0)
    f = pl.program_id(1)
    last_phase = pl.num_programs(0) - 1
    slot = m % 2
    other = 1 - slot
    valid = tval_ref[m] > 0
    per_step = (tm // nf) // 8 * 8
    in_steps = per_step * nf

    def row_tile(r):
        return pl.ds(pl.multiple_of(r * SUBLANES, SUBLANES), SUBLANES)

    def gather_row(tile, r, s):
        tok = pl.multiple_of(tok_ref[tile * tm + r], SUBLANES)
        return pltpu.make_async_copy(x_hbm.at[pl.ds(tok, SUBLANES), :], xbuf.at[s, row_tile(r), :],
                                     gsem.at[s])

    def scatter_row(phase, r, s):
        dst = pl.multiple_of(dst_ref[phase * tm + r], SUBLANES)
        return pltpu.make_async_copy(obuf.at[s, row_tile(r), :], y_hbm.at[pl.ds(dst, SUBLANES), :],
                                     ssem.at[s])

    def chunk(c):
        return pl.ds(c, tm, stride=SUBLANES)

    wait_group = 16

    def gather_wait(tile, s):
        def body(it, c):
            for u in range(wait_group):
                gather_row(tile, it * wait_group + u, s).wait()
            return c
        lax.fori_loop(0, tm // wait_group, body, 0)

    def scatter_wait(phase, s):
        def body(it, c):
            for u in range(wait_group):
                scatter_row(phase, it * wait_group + u, s).wait()
            return c
        lax.fori_loop(0, tm // wait_group, body, 0)

    next_valid = tval_ref[m + 1] > 0
    prev_valid = jnp.logical_and(m >= 1, tval_ref[jnp.maximum(m - 1, 0)] > 0)
    prev2_valid = jnp.logical_and(m >= 2, tval_ref[jnp.maximum(m - 2, 0)] > 0)
    pad_only = jnp.logical_and(jnp.logical_not(valid), m < last_phase)

    def pad_fill():
        first = pl.multiple_of(dst_ref[(m + 1) * tm], SUBLANES)
        return pltpu.make_async_copy(obuf.at[slot], y_hbm.at[pl.ds(first, tm * SUBLANES), :],
                                     ssem.at[slot])

    @pl.when(jnp.logical_and(f == 0, prev2_valid))
    def _():
        scatter_wait(m - 1, slot)

    @pl.when(jnp.logical_and(f == 0, pad_only))
    def _():
        obuf[slot] = jnp.zeros(obuf.shape[1:], obuf.dtype)
        pad_fill().start()

    @pl.when(jnp.logical_and(f == 0, next_valid))
    def _():
        for r in range(in_steps, tm):
            gather_row(m + 1, r, other).start(priority=r % 2)

    @pl.when(jnp.logical_and(f == 0, prev_valid))
    def _():
        for r in range(in_steps, tm):
            scatter_row(m, r, other).start(priority=r % 2)

    @pl.when(jnp.logical_and(f == 0, valid))
    def _():
        @pl.when(m == 0)
        def _():
            def first(r, c):
                gather_row(0, r, 0).start()
                return c
            lax.fori_loop(0, tm, first, 0, unroll=8)

        gather_wait(m, slot)
        n_chunks = h_scr.shape[1] // LANES
        xs = [xbuf[slot, chunk(c), :] for c in range(n_chunks)]
        ssq = xs[0] * xs[0]
        for xc in xs[1:]:
            ssq = ssq + xc * xc
        inv = lax.rsqrt(jnp.sum(ssq, axis=-1, keepdims=True) / h_scr.shape[1] + EPS)
        for c in range(n_chunks):
            lanes = slice(c * LANES, (c + 1) * LANES)
            h_scr[:, lanes] = (xs[c] * inv * g_ref[:, lanes]).astype(BF16)
        acc_scr[...] = jnp.zeros_like(acc_scr)

    def expert_step():
        h = h_scr[...]
        a = _silu(_dot(h, wg_ref[...].astype(BF16))) * _dot(h, wu_ref[...].astype(BF16))
        acc_scr[...] += _dot(a.astype(BF16), wd_ref[...].astype(BF16))

    def gather_rows():
        r0 = pl.multiple_of(f * per_step, 8)
        for rr in range(per_step):
            gather_row(m + 1, r0 + rr, other).start(priority=rr % 2)

    def scatter_rows():
        r0 = pl.multiple_of(f * per_step, 8)
        for rr in range(per_step):
            scatter_row(m, r0 + rr, other).start(priority=rr % 2)

    interior = jnp.logical_and(valid, jnp.logical_and(next_valid, prev_valid))
    edge = jnp.logical_not(interior)

    @pl.when(interior)
    def _():
        gather_rows()
        scatter_rows()
        expert_step()

    @pl.when(jnp.logical_and(edge, valid))
    def _():
        expert_step()

    @pl.when(jnp.logical_and(edge, next_valid))
    def _():
        gather_rows()

    @pl.when(jnp.logical_and(edge, prev_valid))
    def _():
        scatter_rows()

    @pl.when(jnp.logical_and(f == nf - 1, valid))
    def _():
        for c in range(acc_scr.shape[1] // LANES):
            obuf[slot, chunk(c), :] = acc_scr[:, c * LANES:(c + 1) * LANES]

    @pl.when(jnp.logical_and(f == nf - 1, pad_only))
    def _():
        pad_fill().wait()

    @pl.when(jnp.logical_and(jnp.logical_and(f == nf - 1, m == last_phase), prev_valid))
    def _():
        scatter_wait(m, other)


def _moe_experts(x_rows, gain, wg, wu, wd, tok, dst, tile_expert, tile_valid, n_rows, *, tm, tf):
    d = gain.shape[1]
    assert d == SUBLANES * LANES
    nf = wg.shape[2] // tf
    n_phases = tile_expert.shape[0]
    kern = functools.partial(_moe_kernel, tm=tm, nf=nf)

    def fsel(m, f, tval):
        return jnp.where(tval[m] > 0, f, nf - 1)

    grid_spec = pltpu.PrefetchScalarGridSpec(
        num_scalar_prefetch=4,
        grid=(n_phases, nf),
        in_specs=[
            pl.BlockSpec(memory_space=pl.ANY),
            pl.BlockSpec((1, d), lambda m, f, tok, dst, texp, tval: (0, 0)),
            pl.BlockSpec((None, d, tf),
                         lambda m, f, tok, dst, texp, tval: (texp[m], 0, fsel(m, f, tval))),
            pl.BlockSpec((None, d, tf),
                         lambda m, f, tok, dst, texp, tval: (texp[m], 0, fsel(m, f, tval))),
            pl.BlockSpec((None, tf, d),
                         lambda m, f, tok, dst, texp, tval: (texp[m], fsel(m, f, tval), 0)),
        ],
        out_specs=pl.BlockSpec(memory_space=pl.ANY),
        scratch_shapes=[
            pltpu.VMEM((2, tm * SUBLANES, LANES), F32),
            pltpu.VMEM((2, tm * SUBLANES, LANES), F32),
            pltpu.VMEM((tm, d), BF16),
            pltpu.VMEM((tm, d), F32),
            pltpu.SemaphoreType.DMA((2,)),
            pltpu.SemaphoreType.DMA((2,)),
        ],
    )
    return pl.pallas_call(
        kern,
        grid_spec=grid_spec,
        out_shape=jax.ShapeDtypeStruct((n_rows * SUBLANES, LANES), F32),
        compiler_params=_params("arbitrary", "arbitrary"),
        name="moe_experts",
    )(tok, dst, tile_expert, tile_valid, x_rows, gain, wg, wu, wd)


def _combine_kernel(x_ref, r_ref, y_ref, g_ref, o_ref, *, out_norm):
    tm, d = x_ref.shape
    route = r_ref[...]
    for c in range(d // LANES):
        lanes = slice(c * LANES, (c + 1) * LANES)
        acc = x_ref[:, lanes]
        for k in range(TOP_K):
            rows = pl.ds(k * SUBLANES + c, tm, stride=TOP_K * SUBLANES)
            acc = acc + route[:, TOP_K + k:TOP_K + k + 1] * y_ref[rows, :]
        o_ref[:, lanes] = acc
    if out_norm:
        o_ref[...] = _rms(o_ref[...], g_ref[...])


def _moe_combine(x, route, y_rows, out_gain, *, tm=512):
    t, d = x.shape
    out_norm = out_gain is not None
    gain = out_gain if out_norm else jnp.ones((1, d), F32)
    return pl.pallas_call(
        functools.partial(_combine_kernel, out_norm=out_norm),
        grid=(t // tm,),
        in_specs=[
            pl.BlockSpec((tm, d), lambda i: (i, 0)),
            pl.BlockSpec((tm, LANES), lambda i: (i, 0)),
            pl.BlockSpec((tm * TOP_K * SUBLANES, LANES), lambda i: (i, 0)),
            pl.BlockSpec((1, d), lambda i: (0, 0)),
        ],
        out_specs=pl.BlockSpec((tm, d), lambda i: (i, 0)),
        out_shape=jax.ShapeDtypeStruct((t, d), F32),
        compiler_params=_params("parallel"),
        name="moe_combine",
    )(x, route, y_rows, gain)


def _moe(x, x_rows, gain, w_router, wg, wu, wd, out_gain=None, *, tm=MOE_TM, tf=MOE_TF):
    t, d = x.shape
    wr = jnp.zeros((d, LANES), F32).at[:, :N_EXPERTS].set(w_router)
    route, counts = _router(x, gain, wr)
    tok, dst, tile_expert, tile_valid, n_rows = _route_metadata(route, counts, tm)
    y_rows = _moe_experts(x_rows, gain, wg, wu, wd, tok * SUBLANES, dst * SUBLANES,
                          tile_expert, tile_valid, n_rows, tm=tm, tf=tf)
    return _moe_combine(x, route, y_rows, out_gain)


def _route_metadata(route, counts, tm):
    t = route.shape[0]
    n_slots = TOP_K * t
    experts = route[:, :TOP_K].astype(I32)
    rank = route[:, 2 * TOP_K:3 * TOP_K].astype(I32)
    counts = counts[0, :N_EXPERTS].astype(I32)
    eids = jnp.arange(N_EXPERTS, dtype=I32)
    tiles_per = (counts + tm - 1) // tm
    tile_end = jnp.cumsum(tiles_per)
    start = (tile_end - tiles_per) * tm
    pos = jnp.sum(jnp.where(experts[..., None] == eids, start, 0), axis=-1) + rank
    n_tiles = n_slots // tm + N_EXPERTS
    n_rows_pad = n_tiles * tm
    slot_of_row = jnp.full((n_rows_pad,), -1, I32).at[pos.reshape(-1)].set(
        jnp.arange(n_slots, dtype=I32))
    real = slot_of_row >= 0
    tok = jnp.concatenate([jnp.where(real, slot_of_row // TOP_K, 0), jnp.zeros((2 * tm,), I32)])
    row = jnp.arange(n_rows_pad, dtype=I32)
    row_expert = jnp.minimum(jnp.sum((row[:, None] // tm >= tile_end[None, :]).astype(I32), axis=1),
                             N_EXPERTS - 1)
    pick = row_expert[:, None] == eids
    pads = tiles_per * tm - counts
    pads_before = jnp.cumsum(pads) - pads
    used_rows = tile_end[-1] * tm
    dump_used = (jnp.sum(jnp.where(pick, pads_before - start - counts, 0), axis=1) + row)
    dump = n_slots + jnp.where(row < used_rows, dump_used, row - n_slots)
    dst = jnp.concatenate([jnp.zeros((tm,), I32), jnp.where(real, slot_of_row, dump)])
    tile_id = jnp.arange(n_tiles + 2, dtype=I32)
    tile_expert = jnp.sum((tile_id[:, None] >= tile_end[None, :]).astype(I32), axis=1)
    tile_valid = (tile_id < tile_end[-1]).astype(I32)
    last_expert = jnp.max(jnp.where(counts > 0, eids, 0))
    tile_expert = jnp.where(tile_valid > 0, tile_expert, last_expert).astype(I32)
    return tok, dst.astype(I32), tile_expert[:n_tiles + 1], tile_valid, n_rows_pad


def _final_norm_kernel(x_ref, g_ref, o_ref):
    o_ref[...] = _rms(x_ref[...], g_ref[...])


def _final_norm(x, gain, *, tm=1024):
    t, d = x.shape
    return pl.pallas_call(
        _final_norm_kernel,
        grid=(t // tm,),
        in_specs=[pl.BlockSpec((tm, d), lambda i: (i, 0)), pl.BlockSpec((1, d), lambda i: (0, 0))],
        out_specs=pl.BlockSpec((tm, d), lambda i: (i, 0)),
        out_shape=jax.ShapeDtypeStruct((t, d), F32),
        compiler_params=_params("parallel"),
        name="final_norm",
    )(x, gain)


def _rope_tables(seq):
    half = HEAD_DIM // 2
    inv_freq = ROPE_THETA ** (-jnp.arange(half, dtype=F32) / half)
    ang = jnp.arange(seq, dtype=F32)[:, None] * inv_freq[None, :]
    cos, sin = jnp.cos(ang), jnp.sin(ang)
    reps = LANES // HEAD_DIM
    cos_t = jnp.tile(jnp.concatenate([cos, cos], axis=1), (1, reps))
    sin_t = jnp.tile(jnp.concatenate([-sin, sin], axis=1), (1, reps))
    return cos_t, sin_t


def _in_proj_weight(w_in):
    o = 0
    offs = {}
    for name, width in (("qa", A_W), ("ka", A_W), ("va", A_W), ("qb", B_QW), ("kb", B_KVW),
                        ("vb", B_KVW), ("qc", C_W), ("kc", C_W), ("vc", C_W), ("g", 3 * D_MODEL)):
        offs[name] = (o, o + width)
        o += width
    q_scale = {"qa": HEAD_DIM ** -0.5 * LOG2E, "qb": HEAD_DIM ** -0.5 * LOG2E,
               "qc": HEAD_DIM ** -0.5}
    parts = []
    for name in ("g", "qa", "ka", "qb", "kb", "vb", "va", "qc", "kc", "vc"):
        lo, hi = offs[name]
        blk = w_in[:, lo:hi]
        if name in q_scale:
            blk = blk * q_scale[name]
        parts.append(blk)
    return jnp.concatenate(parts, axis=1).astype(BF16)


def _rope_group_map():
    groups = {}
    for t in range(IN_W // PROJ_TN):
        lo, hi = t * PROJ_TN, (t + 1) * PROJ_TN
        n = (min(hi, ROPE_HI) - max(lo, ROPE_LO)) // LANES
        if n > 0:
            assert max(lo, ROPE_LO) == lo
            groups[t] = n
    return groups


def kernel(x, mem, norm_mix, w_in, w_proj_a, w_proj_b, w_proj_c, w_mix_out, sinks, norm_cross,
           norm_mem, w_xq, w_xkv, w_xo, norm_ffn, ffn_gate, ffn_up, ffn_down, moe_router,
           moe_gate, moe_up, moe_down, final_norm):
    batch, seq, d = x.shape
    depth = norm_mix.shape[0]
    assert d == D_MODEL and seq % PROJ_TM == 0 and mem.shape[1] == MEM_LEN
    t = batch * seq
    xf = x.reshape(t, d)
    memf = mem.reshape(batch * MEM_LEN, d)
    cos_t, sin_t = _rope_tables(seq)
    rope_groups = _rope_group_map()
    ones_tab = jnp.ones((MEM_LEN, LANES), F32)

    for l in range(depth):
        gain = lambda g: g[l].reshape(1, d)
        p = _norm_proj(xf, gain(norm_mix), _in_proj_weight(w_in[l]), cos_t, sin_t, seq,
                       tm=PROJ_TM, tn=PROJ_TN, rope_groups=rope_groups)
        oa = _moba(p, batch, seq)
        ob = _swa(p, sinks[l] * LOG2E, batch, seq)
        oc = _stick(p, batch, seq)
        xf = _merge(xf, oa, ob, oc, p, w_proj_a[l].astype(BF16), w_proj_b[l].astype(BF16),
                    w_proj_c[l].astype(BF16), w_mix_out[l].astype(BF16))
        kv = _norm_proj(memf, gain(norm_mem), w_xkv[l].astype(BF16), ones_tab, ones_tab, MEM_LEN,
                        tm=MEM_LEN, tn=2 * X_W, rope_groups={})
        cross_args = (gain(norm_cross), w_xq[l].astype(BF16), kv, w_xo[l].astype(BF16), seq)
        if l % 2 == 0:
            i = l // 2
            xf = _cross(xf, *cross_args)
            xf = _ffn(xf, gain(norm_ffn), ffn_gate[i].astype(BF16), ffn_up[i].astype(BF16),
                      ffn_down[i].astype(BF16), tf=FFN_TF)
        else:
            i = l // 2
            xf, x_rows = _cross(xf, *cross_args, with_row_tiles=True)
            out_gain = final_norm.reshape(1, d) if l == depth - 1 else None
            xf = _moe(xf, x_rows, gain(norm_ffn), moe_router[i], moe_gate[i], moe_up[i],
                      moe_down[i], out_gain)
    if depth % 2 == 1:
        xf = _final_norm(xf, final_norm.reshape(1, d))
    return xf.reshape(batch, seq, d)
```

```python
import functools

import jax
import jax.numpy as jnp
from jax import lax
from jax.experimental import pallas as pl
from jax.experimental.pallas import tpu as pltpu

F32 = jnp.float32
BF16 = jnp.bfloat16
I32 = jnp.int32

D_MODEL = 1024
HEAD_DIM = 64
A_HEADS = 4
MOBA_BLOCK = 256
MOBA_TOPK = 3
B_HEADS = 8
B_KV_HEADS = 2
WINDOW = 128
C_HEADS = 4
SB_BLOCK = 256
MEM_LEN = 256
X_HEADS = 4
X_HEAD_DIM = 128
N_EXPERTS = 8
TOP_K = 2
ROPE_THETA = 10000.0
EPS = 1e-6

A_W = A_HEADS * HEAD_DIM
B_QW = B_HEADS * HEAD_DIM
B_KVW = B_KV_HEADS * HEAD_DIM
C_W = C_HEADS * HEAD_DIM
X_W = X_HEADS * X_HEAD_DIM
QKV_W = 3 * A_W + B_QW + 2 * B_KVW + 3 * C_W
IN_W = QKV_W + 3 * D_MODEL

LANES = 128
SUBLANES = 8
NEG = -1e30
LOG2E = 1.4426950408889634

COL_GA, COL_GB, COL_GC = 0, D_MODEL, 2 * D_MODEL
COL_QA = 3 * D_MODEL
COL_KA = COL_QA + A_W
COL_QB = COL_KA + A_W
COL_KB = COL_QB + B_QW
COL_VB = COL_KB + B_KVW
COL_VA = COL_VB + B_KVW
COL_QC = COL_VA + A_W
COL_KC = COL_QC + C_W
COL_VC = COL_KC + C_W
ROPE_LO, ROPE_HI = COL_QA, COL_VB

PROJ_TM = 2048
PROJ_TN = 768
FFN_TF = 256
MOE_TF = 512
MOE_TM = 1072
VMEM_LIMIT = 48 * 1024 * 1024


def _params(*sem):
    return pltpu.CompilerParams(dimension_semantics=sem, vmem_limit_bytes=VMEM_LIMIT)


def _rms(x, g):
    ms = jnp.mean(x * x, axis=-1, keepdims=True)
    return x * lax.rsqrt(ms + EPS) * g


def _dot(a, b):
    return jnp.dot(a, b, preferred_element_type=F32)


def _dot_t(a, b):
    return lax.dot_general(a, b, (((1,), (1,)), ((), ())), preferred_element_type=F32)


def _split_bf16(v):
    hi = v.astype(BF16)
    lo = (v - hi.astype(F32)).astype(BF16)
    return hi, lo


def _norm_proj_kernel(x_ref, g_ref, w_ref, cos_ref, sin_ref, o_ref, h_scr, *, rope_groups):
    j = pl.program_id(1)

    @pl.when(j == 0)
    def _():
        h_scr[...] = _rms(x_ref[...], g_ref[...]).astype(BF16)

    acc = _dot(h_scr[...], w_ref[...])
    n_groups = acc.shape[1] // LANES

    def rope(y):
        lane = lax.broadcasted_iota(I32, y.shape, 1)
        first_half = (lane % HEAD_DIM) < (HEAD_DIM // 2)
        sw = jnp.where(first_half, pltpu.roll(y, LANES - HEAD_DIM // 2, 1),
                       pltpu.roll(y, HEAD_DIM // 2, 1))
        return y * cos_ref[...] + sw * sin_ref[...]

    roped = sorted(rope_groups)

    for t in roped:
        @pl.when(j == t)
        def _(t=t):
            for gi in range(n_groups):
                y = acc[:, gi * LANES:(gi + 1) * LANES]
                if gi < rope_groups[t]:
                    y = rope(y)
                o_ref[:, gi * LANES:(gi + 1) * LANES] = y.astype(o_ref.dtype)

    is_plain = j >= 0
    for t in roped:
        is_plain = jnp.logical_and(is_plain, j != t)

    @pl.when(is_plain)
    def _():
        o_ref[...] = acc.astype(o_ref.dtype)


def _norm_proj(x, gain, w, cos, sin, seq, *, tm, tn, rope_groups):
    t, d = x.shape
    n = w.shape[1]
    pos_blocks = seq // tm
    kern = functools.partial(_norm_proj_kernel, rope_groups=rope_groups)
    return pl.pallas_call(
        kern,
        grid=(t // tm, n // tn),
        in_specs=[
            pl.BlockSpec((tm, d), lambda i, j: (i, 0)),
            pl.BlockSpec((1, d), lambda i, j: (0, 0)),
            pl.BlockSpec((d, tn), lambda i, j: (0, j)),
            pl.BlockSpec((tm, LANES), lambda i, j: (i % pos_blocks, 0)),
            pl.BlockSpec((tm, LANES), lambda i, j: (i % pos_blocks, 0)),
        ],
        out_specs=pl.BlockSpec((tm, tn), lambda i, j: (i, j)),
        out_shape=jax.ShapeDtypeStruct((t, n), BF16),
        scratch_shapes=[pltpu.VMEM((tm, d), BF16)],
        compiler_params=_params("parallel", "arbitrary"),
        name="norm_proj",
    )(x, gain, w, cos, sin)


MOBA_VROWS = HEAD_DIM + 16


def _moba_kernel(qt_ref, k_ref, vt_ref, o_ref, kmean_scr, kaug_scr, vaug_scr, qaug_scr, *, nblk):
    i = pl.program_id(1)
    blk = MOBA_BLOCK
    hd = HEAD_DIM
    aug = 2 * hd
    vr = MOBA_VROWS
    heads = range(A_HEADS)
    hsl = [slice(hh * hd, (hh + 1) * hd) for hh in heads]

    @pl.when(i == 0)
    def _():
        ones = jnp.ones((vr - hd, blk), BF16)
        blk_lane = lax.broadcasted_iota(I32, (blk, hd), 1)
        for n in range(nblk):
            rows = slice(n * blk, (n + 1) * blk)
            kmean_scr[n:n + 1, :] = jnp.mean(k_ref[rows, :].astype(F32), axis=0, keepdims=True)
            onehot = jnp.where(blk_lane == n, 1.0, 0.0).astype(BF16)
            for hh in heads:
                kaug_scr[rows, hh * aug:hh * aug + hd] = k_ref[rows, hsl[hh]]
                kaug_scr[rows, hh * aug + hd:(hh + 1) * aug] = onehot
                vaug_scr[n, hh * vr:hh * vr + hd, :] = vt_ref[n, hsl[hh], :]
                vaug_scr[n, hh * vr + hd:(hh + 1) * vr, :] = ones

    key = lax.broadcasted_iota(I32, (blk, blk), 0)
    qry = lax.broadcasted_iota(I32, (blk, blk), 1)
    causal = key <= qry
    blk_id = lax.broadcasted_iota(I32, (nblk, blk), 0)
    own0 = pl.multiple_of(i * blk, blk)

    init = []
    for hh in heads:
        qt = qt_ref[hsl[hh], :]
        km_hi, km_lo = _split_bf16(kmean_scr[:, hsl[hh]])
        g = _dot(km_hi, qt) + _dot(km_lo, qt)

        cnt = jnp.zeros((nblk, blk), I32)
        for m in range(nblk):
            gm = g[m:m + 1, :]
            beats = (gm > g) | ((gm == g) & (m < blk_id))
            cnt = cnt + jnp.where(beats, (m < i).astype(I32), 0)
        sel = (cnt < MOBA_TOPK) & (blk_id < i)
        qaug_scr[hh * aug:hh * aug + hd, :] = qt
        qaug_scr[hh * aug + hd:hh * aug + hd + nblk, :] = jnp.where(sel, 0.0, NEG).astype(BF16)
        qaug_scr[hh * aug + hd + nblk:(hh + 1) * aug, :] = jnp.zeros((hd - nblk, blk), BF16)

        s = _dot(k_ref[pl.ds(own0, blk), hsl[hh]], qt)
        s = jnp.where(causal, s, NEG)
        m0 = jnp.max(s, axis=0, keepdims=True)
        p = jnp.exp2(s - m0)
        init.append((m0, _dot(vaug_scr[i, hh * vr:(hh + 1) * vr, :], p.astype(BF16))))

    def step(blocks, carry):
        rows = [pl.ds(pl.multiple_of(j * blk, blk), blk) for j in blocks]
        ss = [[_dot(kaug_scr[r, hh * aug:(hh + 1) * aug], qaug_scr[hh * aug:(hh + 1) * aug, :])
               for r in rows] for hh in heads]
        ms = []
        for hh in heads:
            m_new = carry[hh][0]
            for s in ss[hh]:
                m_new = jnp.maximum(m_new, jnp.max(s, axis=0, keepdims=True))
            ms.append(m_new)
        ps = [[jnp.exp2(s - ms[hh]).astype(BF16) for s in ss[hh]] for hh in heads]
        pvs = []
        for hh in heads:
            pv = None
            for j, p in zip(blocks, ps[hh]):
                d = _dot(vaug_scr[j, hh * vr:(hh + 1) * vr, :], p)
                pv = d if pv is None else pv + d
            pvs.append(pv)
        return tuple((ms[hh], jnp.exp2(carry[hh][0] - ms[hh]) * carry[hh][1] + pvs[hh])
                     for hh in heads)

    fin = lax.fori_loop(0, i // 2, lambda jp, c: step([2 * jp, 2 * jp + 1], c), tuple(init))
    fin = lax.fori_loop(0, i % 2, lambda _, c: step([i - 1], c), fin)
    out_t = jnp.concatenate([acc[:hd, :] / acc[hd:hd + 1, :] for _, acc in fin], axis=0)
    o_ref[...] = out_t.T.astype(o_ref.dtype)


def _moba(p, batch, seq):
    nblk = seq // MOBA_BLOCK
    assert nblk <= HEAD_DIM
    qt = _blocks_t(p, COL_QA, A_W, batch, seq, MOBA_BLOCK)
    vt = _blocks_t(p, COL_VA, A_W, batch, seq, MOBA_BLOCK)
    kern = functools.partial(_moba_kernel, nblk=nblk)
    return pl.pallas_call(
        kern,
        grid=(batch, nblk),
        in_specs=[
            pl.BlockSpec((None, None, A_W, MOBA_BLOCK), lambda b, i: (b, i, 0, 0)),
            pl.BlockSpec((seq, A_W), lambda b, i: (b, COL_KA // A_W)),
            pl.BlockSpec((None, nblk, A_W, MOBA_BLOCK), lambda b, i: (b, 0, 0, 0)),
        ],
        out_specs=pl.BlockSpec((MOBA_BLOCK, A_W), lambda b, i: (b * nblk + i, 0)),
        out_shape=jax.ShapeDtypeStruct((batch * seq, A_W), BF16),
        scratch_shapes=[
            pltpu.VMEM((nblk, A_W), F32),
            pltpu.VMEM((seq, 2 * A_W), BF16),
            pltpu.VMEM((nblk, A_HEADS * MOBA_VROWS, MOBA_BLOCK), BF16),
            pltpu.VMEM((2 * A_W, MOBA_BLOCK), BF16),
        ],
        compiler_params=_params("parallel", "arbitrary"),
        name="moba",
    )(qt, p, vt)


def _swa_kernel(sinks_ref, q_ref, kv_ref, pkv_ref, o_ref, kv_scr, *, tq):
    i = pl.program_id(1)
    w = WINDOW
    kv_scr[0:w, :] = pkv_ref[...]
    kv_scr[w:, :] = kv_ref[...]

    row = lax.broadcasted_iota(I32, (w, 2 * w), 0)
    col = lax.broadcasted_iota(I32, (w, 2 * w), 1)
    rel = row + w - col
    band = (rel >= 0) & (rel < w)
    group = B_HEADS // B_KV_HEADS

    ones = jnp.ones((2 * w, HEAD_DIM), BF16)

    def body(n, carry):
        base = pl.multiple_of(n * w, w)
        kvt = kv_scr[pl.ds(base, 2 * w), :]
        qt = q_ref[pl.ds(base, w), :]
        col_min = jnp.where(jnp.logical_and(i == 0, n == 0), w, 0)
        mask = band & (col >= col_min)
        heads = [(kh, gi) for kh in range(B_KV_HEADS) for gi in range(group)]
        hcol = lambda hd: slice(hd * HEAD_DIM, (hd + 1) * HEAD_DIM)
        ss = []
        for kh in range(B_KV_HEADS):
            q4 = jnp.concatenate([qt[:, hcol(kh * group + gi)] for gi in range(group)], axis=0)
            ss.append(_dot_t(q4, kvt[:, hcol(kh)]))
        ms, ps = {}, {}
        for kh, gi in heads:
            s = jnp.where(mask, ss[kh][gi * w:(gi + 1) * w], NEG)
            m = jnp.maximum(jnp.max(s, axis=-1, keepdims=True), sinks_ref[kh * group + gi])
            ms[(kh, gi)] = m
            ps[(kh, gi)] = jnp.exp2(s - m).astype(BF16)
        pvs = []
        for kh in range(B_KV_HEADS):
            vaug = jnp.concatenate([kvt[:, B_KVW + kh * HEAD_DIM:B_KVW + (kh + 1) * HEAD_DIM], ones],
                                   axis=1)
            p4 = jnp.concatenate([ps[(kh, gi)] for gi in range(group)], axis=0)
            pvs.append(_dot(p4, vaug))
        outs = []
        for kh, gi in heads:
            pv = pvs[kh][gi * w:(gi + 1) * w]
            l = pv[:, HEAD_DIM:] + jnp.exp2(sinks_ref[kh * group + gi] - ms[(kh, gi)])
            outs.append(pv[:, :HEAD_DIM] / l)
        o_ref[pl.ds(base, w), :] = jnp.concatenate(outs, axis=1).astype(o_ref.dtype)
        return carry

    lax.fori_loop(0, tq // w, body, 0)


def _swa(p, sinks, batch, seq, *, tq=512):
    nq = seq // tq
    sub = tq // WINDOW
    kern = functools.partial(_swa_kernel, tq=tq)
    return pl.pallas_call(
        kern,
        grid=(batch, nq),
        in_specs=[
            pl.BlockSpec(memory_space=pltpu.SMEM),
            pl.BlockSpec((tq, B_QW), lambda b, i: (b * nq + i, COL_QB // B_QW)),
            pl.BlockSpec((tq, 2 * B_KVW), lambda b, i: (b * nq + i, COL_KB // (2 * B_KVW))),
            pl.BlockSpec((WINDOW, 2 * B_KVW),
                         lambda b, i: (jnp.maximum((b * nq + i) * sub - 1, 0),
                                       COL_KB // (2 * B_KVW))),
        ],
        out_specs=pl.BlockSpec((tq, B_QW), lambda b, i: (b * nq + i, 0)),
        out_shape=jax.ShapeDtypeStruct((batch * seq, B_QW), BF16),
        scratch_shapes=[pltpu.VMEM((tq + WINDOW, 2 * B_KVW), BF16)],
        compiler_params=_params("parallel", "parallel"),
        name="swa",
    )(sinks, p, p, p)


def _stick_kernel(qt_ref, k_ref, vt_ref, o_ref, z_scr, sp_scr, kpad_scr):
    i = pl.program_id(1)
    blk = SB_BLOCK
    hd = HEAD_DIM
    heads = range(C_HEADS)
    hsl = [slice(hh * hd, (hh + 1) * hd) for hh in heads]
    key = lax.broadcasted_iota(I32, (blk, blk), 0)
    qry = lax.broadcasted_iota(I32, (blk, blk), 1)
    causal = key < qry
    ntri = jnp.where(qry >= key, -1.0, 0.0).astype(BF16)
    own0 = pl.multiple_of(i * blk, blk)

    def exp(x):
        return jnp.exp2(x * LOG2E)

    def softplus2(z):
        return jnp.maximum(z, 0.0) + jnp.log(1.0 + jnp.exp2(jnp.abs(z) * -LOG2E))

    def suffix(sp):
        incl = _dot(ntri, sp.astype(BF16))
        return incl, incl[0:1, :]

    def weight(x):
        return exp(x).astype(BF16)

    @pl.when(i == 0)
    def _():
        pad = jnp.zeros((blk, LANES - hd), BF16)
        for n in range(k_ref.shape[0] // blk):
            rows = slice(n * blk, (n + 1) * blk)
            for hh in heads:
                kpad_scr[rows, hh * LANES:hh * LANES + hd] = k_ref[rows, hsl[hh]]
                kpad_scr[rows, hh * LANES + hd:(hh + 1) * LANES] = pad

    ksl = [slice(hh * LANES, (hh + 1) * LANES) for hh in heads]
    qts = [jnp.concatenate([qt_ref[hsl[hh], :], jnp.zeros((LANES - hd, blk), BF16)], axis=0)
           for hh in heads]
    init = []
    for hh in heads:
        z = _dot(kpad_scr[pl.ds(own0, blk), ksl[hh]], qts[hh])
        incl, carry0 = suffix(jnp.where(causal, softplus2(z), 0.0))
        wgt = jnp.where(causal, weight(z + incl), 0.0)
        init.append((carry0, _dot(vt_ref[i, hsl[hh], :], wgt.astype(BF16))))

    def keys(nb):
        return [(hh, b) for hh in heads for b in range(nb)]

    def logits(blocks):
        rows = [pl.ds(pl.multiple_of(j * blk, blk), blk) for j in blocks]
        return {(hh, b): _dot(kpad_scr[rows[b], ksl[hh]], qts[hh])
                for hh, b in keys(len(blocks))}

    def values(blocks, zs, incls):
        ws = {k: weight(zs[k] + incls[k]) for k in zs}
        return {(hh, b): _dot(vt_ref[blocks[b], hsl[hh], :], ws[(hh, b)]) for hh, b in zs}

    def accumulate(state, pvs, incls, nb):
        out = []
        for hh in heads:
            carry, acc = state[hh]
            for b in range(nb):
                acc = acc + pvs[(hh, b)] * exp(carry)
                carry = carry + incls[(hh, b)][0:1, :]
            out.append((carry, acc))
        return tuple(out)

    def single(j, state):
        zs = logits([j])
        incls = {k: suffix(softplus2(zs[k]))[0] for k in zs}
        return accumulate(state, values([j], zs, incls), incls, 1)

    odd = i % 2
    state = lax.fori_loop(0, odd, lambda _, s: single(i - 1, s), tuple(init))
    n_pairs = i // 2
    nearest = i - 1 - odd

    def pair(p):
        return [nearest - 2 * p, nearest - 1 - 2 * p]

    def slot_of(p, k):
        return (p % 2) * len(keys(2)) + keys(2).index(k)

    def stage_in(p):
        zs = logits(pair(p))
        for k in zs:
            z_scr[slot_of(p, k)] = zs[k]
            sp_scr[slot_of(p, k)] = softplus2(zs[k]).astype(BF16)

    def stage_out(p, state, prefetch):
        incls = {k: _dot(ntri, sp_scr[slot_of(p, k)]) for k in keys(2)}
        if prefetch:
            zs_next = logits(pair(p + 1))
        zs = {k: z_scr[slot_of(p, k)] for k in keys(2)}
        pvs = values(pair(p), zs, incls)
        if prefetch:
            for k in zs_next:
                z_scr[slot_of(p + 1, k)] = zs_next[k]
                sp_scr[slot_of(p + 1, k)] = softplus2(zs_next[k]).astype(BF16)
        return accumulate(state, pvs, incls, 2)

    @pl.when(n_pairs > 0)
    def _():
        stage_in(0)

    state = lax.fori_loop(0, n_pairs - 1, lambda p, s: stage_out(p, s, True), state)
    fin = lax.fori_loop(0, jnp.minimum(n_pairs, 1),
                        lambda _, s: stage_out(n_pairs - 1, s, False), state)
    out_t = jnp.concatenate([acc for _, acc in fin], axis=0)
    o_ref[...] = out_t.T.astype(o_ref.dtype)


def _blocks_t(p, col, width, batch, seq, blk):
    t = p[:, col:col + width].reshape(batch, seq // blk, blk, width)
    return jnp.swapaxes(t, 2, 3)


def _stick(p, batch, seq):
    nblk = seq // SB_BLOCK
    qt = _blocks_t(p, COL_QC, C_W, batch, seq, SB_BLOCK)
    vt = _blocks_t(p, COL_VC, C_W, batch, seq, SB_BLOCK)
    return pl.pallas_call(
        _stick_kernel,
        grid=(batch, nblk),
        in_specs=[
            pl.BlockSpec((None, None, C_W, SB_BLOCK), lambda b, i: (b, i, 0, 0)),
            pl.BlockSpec((seq, C_W), lambda b, i: (b, COL_KC // C_W)),
            pl.BlockSpec((None, nblk, C_W, SB_BLOCK), lambda b, i: (b, 0, 0, 0)),
        ],
        out_specs=pl.BlockSpec((SB_BLOCK, C_W), lambda b, i: (b * nblk + i, 0)),
        out_shape=jax.ShapeDtypeStruct((batch * seq, C_W), BF16),
        scratch_shapes=[pltpu.VMEM((4 * C_HEADS, SB_BLOCK, SB_BLOCK), F32),
                        pltpu.VMEM((4 * C_HEADS, SB_BLOCK, SB_BLOCK), BF16),
                        pltpu.VMEM((seq, C_HEADS * LANES), BF16)],
        compiler_params=_params("parallel", "arbitrary"),
        name="stick",
    )(qt, p, vt)


def _sigmoid(x):
    return 1.0 / (1.0 + jnp.exp(-x))


def _merge_kernel(x_ref, oa_ref, ob_ref, oc_ref, ga_ref, gb_ref, gc_ref,
                  wa_ref, wb_ref, wc_ref, wo_ref, o_ref):
    mixed = _sigmoid(ga_ref[...].astype(F32)) * _dot(oa_ref[...], wa_ref[...])
    mixed = mixed + _sigmoid(gb_ref[...].astype(F32)) * _dot(ob_ref[...], wb_ref[...])
    mixed = mixed + _sigmoid(gc_ref[...].astype(F32)) * _dot(oc_ref[...], wc_ref[...])
    o_ref[...] = x_ref[...] + _dot(mixed.astype(BF16), wo_ref[...])


def _merge(x, oa, ob, oc, p, wa, wb, wc, wo, *, tm=512):
    t, d = x.shape
    full = lambda a: pl.BlockSpec(a.shape, lambda i: (0, 0))
    return pl.pallas_call(
        _merge_kernel,
        grid=(t // tm,),
        in_specs=[
            pl.BlockSpec((tm, d), lambda i: (i, 0)),
            pl.BlockSpec((tm, A_W), lambda i: (i, 0)),
            pl.BlockSpec((tm, B_QW), lambda i: (i, 0)),
            pl.BlockSpec((tm, C_W), lambda i: (i, 0)),
            pl.BlockSpec((tm, d), lambda i: (i, COL_GA // D_MODEL)),
            pl.BlockSpec((tm, d), lambda i: (i, COL_GB // D_MODEL)),
            pl.BlockSpec((tm, d), lambda i: (i, COL_GC // D_MODEL)),
            full(wa), full(wb), full(wc), full(wo),
        ],
        out_specs=pl.BlockSpec((tm, d), lambda i: (i, 0)),
        out_shape=jax.ShapeDtypeStruct((t, d), F32),
        compiler_params=_params("parallel"),
        name="merge",
    )(x, oa, ob, oc, p, p, p, wa, wb, wc, wo)


def _cross_kernel(x_ref, g_ref, wq_ref, kv_ref, wo_ref, o_ref, *rows_ref):
    x = x_ref[...]
    h = _rms(x, g_ref[...]).astype(BF16)
    q = (_dot(h, wq_ref[...]) * (X_HEAD_DIM ** -0.5)).astype(BF16)
    outs = []
    for hd in range(X_HEADS):
        hs = slice(hd * X_HEAD_DIM, (hd + 1) * X_HEAD_DIM)
        s = _dot_t(q[:, hs], kv_ref[:, hs])
        m = jnp.max(s, axis=-1, keepdims=True)
        p = jnp.exp(s - m)
        l = jnp.sum(p, axis=-1, keepdims=True)
        vs = slice(X_W + hd * X_HEAD_DIM, X_W + (hd + 1) * X_HEAD_DIM)
        outs.append(_dot(p.astype(BF16), kv_ref[:, vs]) / l)
    att = jnp.concatenate(outs, axis=1).astype(BF16)
    o_ref[...] = x + _dot(att, wo_ref[...])
    if rows_ref:
        tm, d = o_ref.shape
        for c in range(d // LANES):
            rows_ref[0][pl.ds(c, tm, stride=SUBLANES), :] = o_ref[:, c * LANES:(c + 1) * LANES]


def _cross(x, gain, wq, kv, wo, seq, *, tm=512, with_row_tiles=False):
    t, d = x.shape
    per_batch = seq // tm
    full = lambda a: pl.BlockSpec(a.shape, lambda i: (0, 0))
    out_specs = [pl.BlockSpec((tm, d), lambda i: (i, 0))]
    out_shape = [jax.ShapeDtypeStruct((t, d), F32)]
    if with_row_tiles:
        out_specs.append(pl.BlockSpec((tm * SUBLANES, LANES), lambda i: (i, 0)))
        out_shape.append(jax.ShapeDtypeStruct((t * SUBLANES, LANES), F32))
    outs = pl.pallas_call(
        _cross_kernel,
        grid=(t // tm,),
        in_specs=[
            pl.BlockSpec((tm, d), lambda i: (i, 0)),
            full(gain), full(wq),
            pl.BlockSpec((MEM_LEN, 2 * X_W), lambda i: (i // per_batch, 0)),
            full(wo),
        ],
        out_specs=out_specs,
        out_shape=out_shape,
        compiler_params=_params("parallel"),
        name="cross",
    )(x, gain, wq, kv, wo)
    return outs if with_row_tiles else outs[0]


def _silu(x):
    return x * _sigmoid(x)


def _ffn_kernel(x_ref, g_ref, wg_ref, wu_ref, wd_ref, o_ref, h_scr, acc_scr):
    f = pl.program_id(1)

    @pl.when(f == 0)
    def _():
        h_scr[...] = _rms(x_ref[...], g_ref[...]).astype(BF16)
        acc_scr[...] = x_ref[...]

    h = h_scr[...]
    a = _silu(_dot(h, wg_ref[...])) * _dot(h, wu_ref[...])
    acc_scr[...] += _dot(a.astype(BF16), wd_ref[...])

    @pl.when(f == pl.num_programs(1) - 1)
    def _():
        o_ref[...] = acc_scr[...]


def _ffn(x, gain, wg, wu, wd, *, tm=1024, tf=256):
    t, d = x.shape
    ff = wd.shape[0]
    return pl.pallas_call(
        _ffn_kernel,
        grid=(t // tm, ff // tf),
        in_specs=[
            pl.BlockSpec((tm, d), lambda i, f: (i, 0)),
            pl.BlockSpec((1, d), lambda i, f: (0, 0)),
            pl.BlockSpec((d, tf), lambda i, f: (0, f)),
            pl.BlockSpec((d, tf), lambda i, f: (0, f)),
            pl.BlockSpec((tf, d), lambda i, f: (f, 0)),
        ],
        out_specs=pl.BlockSpec((tm, d), lambda i, f: (i, 0)),
        out_shape=jax.ShapeDtypeStruct((t, d), F32),
        scratch_shapes=[pltpu.VMEM((tm, d), BF16), pltpu.VMEM((tm, d), F32)],
        compiler_params=_params("parallel", "arbitrary"),
        name="ffn",
    )(x, gain, wg, wu, wd)


def _router_kernel(x_ref, g_ref, wr_ref, o_ref):
    h = _rms(x_ref[...], g_ref[...])
    h_hi, h_lo = _split_bf16(h)
    w_hi, w_lo = _split_bf16(wr_ref[...])
    logits = _dot(h_hi, w_hi) + (_dot(h_hi, w_lo) + _dot(h_lo, w_hi))
    lane = lax.broadcasted_iota(I32, logits.shape, 1)
    logits = jnp.where(lane < N_EXPERTS, logits, NEG)
    v1 = jnp.max(logits, axis=-1, keepdims=True)
    i1 = jnp.min(jnp.where(logits == v1, lane, LANES), axis=-1, keepdims=True)
    rest = jnp.where(lane == i1, NEG, logits)
    v2 = jnp.max(rest, axis=-1, keepdims=True)
    i2 = jnp.min(jnp.where(rest == v2, lane, LANES), axis=-1, keepdims=True)
    e = jnp.exp(v2 - v1)
    w1 = 1.0 / (1.0 + e)
    w2 = e / (1.0 + e)
    out = jnp.where(lane == 0, i1.astype(F32), 0.0)
    out = jnp.where(lane == 1, i2.astype(F32), out)
    out = jnp.where(lane == 2, w1, out)
    out = jnp.where(lane == 3, w2, out)
    o_ref[...] = out


def _router(x, gain, wr, *, tm=1024):
    t, d = x.shape
    return pl.pallas_call(
        _router_kernel,
        grid=(t // tm,),
        in_specs=[
            pl.BlockSpec((tm, d), lambda i: (i, 0)),
            pl.BlockSpec((1, d), lambda i: (0, 0)),
            pl.BlockSpec((d, LANES), lambda i: (0, 0)),
        ],
        out_specs=pl.BlockSpec((tm, LANES), lambda i: (i, 0)),
        out_shape=jax.ShapeDtypeStruct((t, LANES), F32),
        compiler_params=_params("parallel"),
        name="router",
    )(x, gain, wr)


def _moe_kernel(tok_ref, dst_ref, texp_ref, tval_ref, x_hbm, g_ref, wg_ref, wu_ref, wd_ref,
                y_hbm, xbuf, obuf, h_scr, acc_scr, gsem, ssem, *, tm, nf):
    m = pl.program_id(0)
    f = pl.program_id(1)
    last_phase = pl.num_programs(0) - 1
    slot = m % 2
    other = 1 - slot
    valid = tval_ref[m] > 0
    per_step = (tm // nf) // 8 * 8
    in_steps = per_step * nf

    def row_tile(r):
        return pl.ds(pl.multiple_of(r * SUBLANES, SUBLANES), SUBLANES)

    def gather_row(tile, r, s):
        tok = pl.multiple_of(tok_ref[tile * tm + r], SUBLANES)
        return pltpu.make_async_copy(x_hbm.at[pl.ds(tok, SUBLANES), :], xbuf.at[s, row_tile(r), :],
                                     gsem.at[s])

    def scatter_row(phase, r, s):
        dst = pl.multiple_of(dst_ref[phase * tm + r], SUBLANES)
        return pltpu.make_async_copy(obuf.at[s, row_tile(r), :], y_hbm.at[pl.ds(dst, SUBLANES), :],
                                     ssem.at[s])

    def chunk(c):
        return pl.ds(c, tm, stride=SUBLANES)

    wait_group = 16

    def gather_wait(tile, s):
        def body(it, c):
            for u in range(wait_group):
                gather_row(tile, it * wait_group + u, s).wait()
            return c
        lax.fori_loop(0, tm // wait_group, body, 0)

    def scatter_wait(phase, s):
        def body(it, c):
            for u in range(wait_group):
                scatter_row(phase, it * wait_group + u, s).wait()
            return c
        lax.fori_loop(0, tm // wait_group, body, 0)

    next_valid = tval_ref[m + 1] > 0
    prev_valid = jnp.logical_and(m >= 1, tval_ref[jnp.maximum(m - 1, 0)] > 0)
    prev2_valid = jnp.logical_and(m >= 2, tval_ref[jnp.maximum(m - 2, 0)] > 0)
    pad_only = jnp.logical_and(jnp.logical_not(valid), m < last_phase)

    def pad_fill():
        first = pl.multiple_of(dst_ref[(m + 1) * tm], SUBLANES)
        return pltpu.make_async_copy(obuf.at[slot], y_hbm.at[pl.ds(first, tm * SUBLANES), :],
                                     ssem.at[slot])

    @pl.when(jnp.logical_and(f == 0, prev2_valid))
    def _():
        scatter_wait(m - 1, slot)

    @pl.when(jnp.logical_and(f == 0, pad_only))
    def _():
        obuf[slot] = jnp.zeros(obuf.shape[1:], obuf.dtype)
        pad_fill().start()

    @pl.when(jnp.logical_and(f == 0, next_valid))
    def _():
        for r in range(in_steps, tm):
            gather_row(m + 1, r, other).start(priority=r % 2)

    @pl.when(jnp.logical_and(f == 0, prev_valid))
    def _():
        for r in range(in_steps, tm):
            scatter_row(m, r, other).start(priority=r % 2)

    @pl.when(jnp.logical_and(f == 0, valid))
    def _():
        @pl.when(m == 0)
        def _():
            def first(r, c):
                gather_row(0, r, 0).start()
                return c
            lax.fori_loop(0, tm, first, 0, unroll=8)

        gather_wait(m, slot)
        n_chunks = h_scr.shape[1] // LANES
        xs = [xbuf[slot, chunk(c), :] for c in range(n_chunks)]
        ssq = xs[0] * xs[0]
        for xc in xs[1:]:
            ssq = ssq + xc * xc
        inv = lax.rsqrt(jnp.sum(ssq, axis=-1, keepdims=True) / h_scr.shape[1] + EPS)
        for c in range(n_chunks):
            lanes = slice(c * LANES, (c + 1) * LANES)
            h_scr[:, lanes] = (xs[c] * inv * g_ref[:, lanes]).astype(BF16)
        acc_scr[...] = jnp.zeros_like(acc_scr)

    def expert_step():
        h = h_scr[...]
        a = _silu(_dot(h, wg_ref[...].astype(BF16))) * _dot(h, wu_ref[...].astype(BF16))
        acc_scr[...] += _dot(a.astype(BF16), wd_ref[...].astype(BF16))

    def gather_rows():
        r0 = pl.multiple_of(f * per_step, 8)
        for rr in range(per_step):
            gather_row(m + 1, r0 + rr, other).start(priority=rr % 2)

    def scatter_rows():
        r0 = pl.multiple_of(f * per_step, 8)
        for rr in range(per_step):
            scatter_row(m, r0 + rr, other).start(priority=rr % 2)

    interior = jnp.logical_and(valid, jnp.logical_and(next_valid, prev_valid))
    edge = jnp.logical_not(interior)

    @pl.when(interior)
    def _():
        gather_rows()
        scatter_rows()
        expert_step()

    @pl.when(jnp.logical_and(edge, valid))
    def _():
        expert_step()

    @pl.when(jnp.logical_and(edge, next_valid))
    def _():
        gather_rows()

    @pl.when(jnp.logical_and(edge, prev_valid))
    def _():
        scatter_rows()

    @pl.when(jnp.logical_and(f == nf - 1, valid))
    def _():
        for c in range(acc_scr.shape[1] // LANES):
            obuf[slot, chunk(c), :] = acc_scr[:, c * LANES:(c + 1) * LANES]

    @pl.when(jnp.logical_and(f == nf - 1, pad_only))
    def _():
        pad_fill().wait()

    @pl.when(jnp.logical_and(jnp.logical_and(f == nf - 1, m == last_phase), prev_valid))
    def _():
        scatter_wait(m, other)


def _moe_experts(x_rows, gain, wg, wu, wd, tok, dst, tile_expert, tile_valid, n_rows, *, tm, tf):
    d = gain.shape[1]
    assert d == SUBLANES * LANES
    nf = wg.shape[2] // tf
    n_phases = tile_expert.shape[0]
    kern = functools.partial(_moe_kernel, tm=tm, nf=nf)

    def fsel(m, f, tval):
        return jnp.where(tval[m] > 0, f, nf - 1)

    grid_spec = pltpu.PrefetchScalarGridSpec(
        num_scalar_prefetch=4,
        grid=(n_phases, nf),
        in_specs=[
            pl.BlockSpec(memory_space=pl.ANY),
            pl.BlockSpec((1, d), lambda m, f, tok, dst, texp, tval: (0, 0)),
            pl.BlockSpec((None, d, tf),
                         lambda m, f, tok, dst, texp, tval: (texp[m], 0, fsel(m, f, tval))),
            pl.BlockSpec((None, d, tf),
                         lambda m, f, tok, dst, texp, tval: (texp[m], 0, fsel(m, f, tval))),
            pl.BlockSpec((None, tf, d),
                         lambda m, f, tok, dst, texp, tval: (texp[m], fsel(m, f, tval), 0)),
        ],
        out_specs=pl.BlockSpec(memory_space=pl.ANY),
        scratch_shapes=[
            pltpu.VMEM((2, tm * SUBLANES, LANES), F32),
            pltpu.VMEM((2, tm * SUBLANES, LANES), F32),
            pltpu.VMEM((tm, d), BF16),
            pltpu.VMEM((tm, d), F32),
            pltpu.SemaphoreType.DMA((2,)),
            pltpu.SemaphoreType.DMA((2,)),
        ],
    )
    return pl.pallas_call(
        kern,
        grid_spec=grid_spec,
        out_shape=jax.ShapeDtypeStruct((n_rows * SUBLANES, LANES), F32),
        compiler_params=_params("arbitrary", "arbitrary"),
        name="moe_experts",
    )(tok, dst, tile_expert, tile_valid, x_rows, gain, wg, wu, wd)


def _combine_kernel(x_ref, r_ref, y0_ref, y1_ref, g_ref, o_ref, *, out_norm):
    tm, d = x_ref.shape
    route = r_ref[...]
    w0 = route[:, TOP_K:TOP_K + 1]
    w1 = route[:, TOP_K + 1:TOP_K + 2]
    for c in range(d // LANES):
        lanes = slice(c * LANES, (c + 1) * LANES)
        rows = pl.ds(c, tm, stride=SUBLANES)
        o_ref[:, lanes] = x_ref[:, lanes] + w0 * y0_ref[rows, :] + w1 * y1_ref[rows, :]
    if out_norm:
        o_ref[...] = _rms(o_ref[...], g_ref[...])


def _moe_combine(x, route, y_rows, out_gain, *, tm=512):
    t, d = x.shape
    assert TOP_K == 2
    out_norm = out_gain is not None
    gain = out_gain if out_norm else jnp.ones((1, d), F32)
    return pl.pallas_call(
        functools.partial(_combine_kernel, out_norm=out_norm),
        grid=(t // tm,),
        in_specs=[
            pl.BlockSpec((tm, d), lambda i: (i, 0)),
            pl.BlockSpec((tm, LANES), lambda i: (i, 0)),
            pl.BlockSpec((tm * SUBLANES, LANES), lambda i: (i, 0)),
            pl.BlockSpec((tm * SUBLANES, LANES), lambda i: (t // tm + i, 0)),
            pl.BlockSpec((1, d), lambda i: (0, 0)),
        ],
        out_specs=pl.BlockSpec((tm, d), lambda i: (i, 0)),
        out_shape=jax.ShapeDtypeStruct((t, d), F32),
        compiler_params=_params("parallel"),
        name="moe_combine",
    )(x, route, y_rows, y_rows, gain)


def _moe(x, x_rows, gain, w_router, wg, wu, wd, out_gain=None, *, tm=MOE_TM, tf=MOE_TF):
    t, d = x.shape
    wr = jnp.zeros((d, LANES), F32).at[:, :N_EXPERTS].set(w_router)
    route = _router(x, gain, wr)
    tok, dst, tile_expert, tile_valid, n_rows = _route_metadata(route[:, :TOP_K].astype(I32), tm)
    y_rows = _moe_experts(x_rows, gain, wg, wu, wd, tok * SUBLANES, dst * SUBLANES,
                          tile_expert, tile_valid, n_rows, tm=tm, tf=tf)
    return _moe_combine(x, route, y_rows, out_gain)


def _route_metadata(experts, tm):
    t = experts.shape[0]
    n_slots = TOP_K * t
    eids = jnp.arange(N_EXPERTS, dtype=I32)
    eflat = experts.T.reshape(-1)
    onehot = (eflat[:, None] == eids[None, :]).astype(I32)
    csum = jnp.cumsum(onehot, axis=0)
    rank = jnp.sum((csum - 1) * onehot, axis=1)
    counts = csum[-1]
    tiles_per = (counts + tm - 1) // tm
    tile_end = jnp.cumsum(tiles_per)
    start = (tile_end - tiles_per) * tm
    pos = (jnp.sum(start[None, :] * onehot, axis=1) + rank).astype(I32)
    n_tiles = n_slots // tm + N_EXPERTS
    n_rows_pad = n_tiles * tm
    slot_of_row = jnp.full((n_rows_pad,), -1, I32).at[pos].set(jnp.arange(n_slots, dtype=I32))
    real = slot_of_row >= 0
    tok = jnp.concatenate([jnp.where(real, slot_of_row % t, 0), jnp.zeros((2 * tm,), I32)])
    dump = n_slots + jnp.cumsum(jnp.logical_not(real).astype(I32)) - 1
    dst = jnp.concatenate([jnp.zeros((tm,), I32), jnp.where(real, slot_of_row, dump)])
    tile_id = jnp.arange(n_tiles + 2, dtype=I32)
    tile_expert = jnp.sum((tile_id[:, None] >= tile_end[None, :]).astype(I32), axis=1)
    tile_valid = (tile_id < tile_end[-1]).astype(I32)
    last_expert = jnp.max(jnp.where(counts > 0, eids, 0))
    tile_expert = jnp.where(tile_valid > 0, tile_expert, last_expert).astype(I32)
    return tok, dst.astype(I32), tile_expert[:n_tiles + 1], tile_valid, n_rows_pad


def _final_norm_kernel(x_ref, g_ref, o_ref):
    o_ref[...] = _rms(x_ref[...], g_ref[...])


def _final_norm(x, gain, *, tm=1024):
    t, d = x.shape
    return pl.pallas_call(
        _final_norm_kernel,
        grid=(t // tm,),
        in_specs=[pl.BlockSpec((tm, d), lambda i: (i, 0)), pl.BlockSpec((1, d), lambda i: (0, 0))],
        out_specs=pl.BlockSpec((tm, d), lambda i: (i, 0)),
        out_shape=jax.ShapeDtypeStruct((t, d), F32),
        compiler_params=_params("parallel"),
        name="final_norm",
    )(x, gain)


def _rope_tables(seq):
    half = HEAD_DIM // 2
    inv_freq = ROPE_THETA ** (-jnp.arange(half, dtype=F32) / half)
    ang = jnp.arange(seq, dtype=F32)[:, None] * inv_freq[None, :]
    cos, sin = jnp.cos(ang), jnp.sin(ang)
    reps = LANES // HEAD_DIM
    cos_t = jnp.tile(jnp.concatenate([cos, cos], axis=1), (1, reps))
    sin_t = jnp.tile(jnp.concatenate([-sin, sin], axis=1), (1, reps))
    return cos_t, sin_t


def _in_proj_weight(w_in):
    o = 0
    offs = {}
    for name, width in (("qa", A_W), ("ka", A_W), ("va", A_W), ("qb", B_QW), ("kb", B_KVW),
                        ("vb", B_KVW), ("qc", C_W), ("kc", C_W), ("vc", C_W), ("g", 3 * D_MODEL)):
        offs[name] = (o, o + width)
        o += width
    q_scale = {"qa": HEAD_DIM ** -0.5 * LOG2E, "qb": HEAD_DIM ** -0.5 * LOG2E,
               "qc": HEAD_DIM ** -0.5}
    parts = []
    for name in ("g", "qa", "ka", "qb", "kb", "vb", "va", "qc", "kc", "vc"):
        lo, hi = offs[name]
        blk = w_in[:, lo:hi]
        if name in q_scale:
            blk = blk * q_scale[name]
        parts.append(blk)
    return jnp.concatenate(parts, axis=1).astype(BF16)


def _rope_group_map():
    groups = {}
    for t in range(IN_W // PROJ_TN):
        lo, hi = t * PROJ_TN, (t + 1) * PROJ_TN
        n = (min(hi, ROPE_HI) - max(lo, ROPE_LO)) // LANES
        if n > 0:
            assert max(lo, ROPE_LO) == lo
            groups[t] = n
    return groups


def kernel(x, mem, norm_mix, w_in, w_proj_a, w_proj_b, w_proj_c, w_mix_out, sinks, norm_cross,
           norm_mem, w_xq, w_xkv, w_xo, norm_ffn, ffn_gate, ffn_up, ffn_down, moe_router,
           moe_gate, moe_up, moe_down, final_norm):
    batch, seq, d = x.shape
    depth = norm_mix.shape[0]
    assert d == D_MODEL and seq % PROJ_TM == 0 and mem.shape[1] == MEM_LEN
    t = batch * seq
    xf = x.reshape(t, d)
    memf = mem.reshape(batch * MEM_LEN, d)
    cos_t, sin_t = _rope_tables(seq)
    rope_groups = _rope_group_map()
    ones_tab = jnp.ones((MEM_LEN, LANES), F32)

    for l in range(depth):
        gain = lambda g: g[l].reshape(1, d)
        p = _norm_proj(xf, gain(norm_mix), _in_proj_weight(w_in[l]), cos_t, sin_t, seq,
                       tm=PROJ_TM, tn=PROJ_TN, rope_groups=rope_groups)
        oa = _moba(p, batch, seq)
        ob = _swa(p, sinks[l] * LOG2E, batch, seq)
        oc = _stick(p, batch, seq)
        xf = _merge(xf, oa, ob, oc, p, w_proj_a[l].astype(BF16), w_proj_b[l].astype(BF16),
                    w_proj_c[l].astype(BF16), w_mix_out[l].astype(BF16))
        kv = _norm_proj(memf, gain(norm_mem), w_xkv[l].astype(BF16), ones_tab, ones_tab, MEM_LEN,
                        tm=MEM_LEN, tn=2 * X_W, rope_groups={})
        cross_args = (gain(norm_cross), w_xq[l].astype(BF16), kv, w_xo[l].astype(BF16), seq)
        if l % 2 == 0:
            i = l // 2
            xf = _cross(xf, *cross_args)
            xf = _ffn(xf, gain(norm_ffn), ffn_gate[i].astype(BF16), ffn_up[i].astype(BF16),
                      ffn_down[i].astype(BF16), tf=FFN_TF)
        else:
            i = l // 2
            xf, x_rows = _cross(xf, *cross_args, with_row_tiles=True)
            out_gain = final_norm.reshape(1, d) if l == depth - 1 else None
            xf = _moe(xf, x_rows, gain(norm_ffn), moe_router[i], moe_gate[i], moe_up[i],
                      moe_down[i], out_gain)
    if depth % 2 == 1:
        xf = _final_norm(xf, final_norm.reshape(1, d))
    return xf.reshape(batch, seq, d)
```

```python
import functools

import jax
import jax.numpy as jnp
from jax import lax
from jax.experimental import pallas as pl
from jax.experimental.pallas import tpu as pltpu

F32 = jnp.float32
BF16 = jnp.bfloat16
I32 = jnp.int32

D_MODEL = 1024
HEAD_DIM = 64
A_HEADS = 4
MOBA_BLOCK = 256
MOBA_TOPK = 3
B_HEADS = 8
B_KV_HEADS = 2
WINDOW = 128
C_HEADS = 4
SB_BLOCK = 256
MEM_LEN = 256
X_HEADS = 4
X_HEAD_DIM = 128
N_EXPERTS = 8
TOP_K = 2
ROPE_THETA = 10000.0
EPS = 1e-6

A_W = A_HEADS * HEAD_DIM
B_QW = B_HEADS * HEAD_DIM
B_KVW = B_KV_HEADS * HEAD_DIM
C_W = C_HEADS * HEAD_DIM
X_W = X_HEADS * X_HEAD_DIM
QKV_W = 3 * A_W + B_QW + 2 * B_KVW + 3 * C_W
IN_W = QKV_W + 3 * D_MODEL

LANES = 128
SUBLANES = 8
NEG = -1e30
LOG2E = 1.4426950408889634

COL_GA, COL_GB, COL_GC = 0, D_MODEL, 2 * D_MODEL
COL_QA = 3 * D_MODEL
COL_KA = COL_QA + A_W
COL_QB = COL_KA + A_W
COL_KB = COL_QB + B_QW
COL_VB = COL_KB + B_KVW
COL_VA = COL_VB + B_KVW
COL_QC = COL_VA + A_W
COL_KC = COL_QC + C_W
COL_VC = COL_KC + C_W
ROPE_LO, ROPE_HI = COL_QA, COL_VB

PROJ_TM = 2048
PROJ_TN = 768
FFN_TF = 256
MOE_TF = 512
MOE_TM = 1072
VMEM_LIMIT = 48 * 1024 * 1024


def _params(*sem):
    return pltpu.CompilerParams(dimension_semantics=sem, vmem_limit_bytes=VMEM_LIMIT)


def _rms(x, g):
    ms = jnp.mean(x * x, axis=-1, keepdims=True)
    return x * lax.rsqrt(ms + EPS) * g


def _dot(a, b):
    return jnp.dot(a, b, preferred_element_type=F32)


def _dot_t(a, b):
    return lax.dot_general(a, b, (((1,), (1,)), ((), ())), preferred_element_type=F32)


def _split_bf16(v):
    hi = v.astype(BF16)
    lo = (v - hi.astype(F32)).astype(BF16)
    return hi, lo


def _norm_proj_kernel(x_ref, g_ref, w_ref, cos_ref, sin_ref, o_ref, h_scr, *, rope_groups):
    j = pl.program_id(1)

    @pl.when(j == 0)
    def _():
        h_scr[...] = _rms(x_ref[...], g_ref[...]).astype(BF16)

    acc = _dot(h_scr[...], w_ref[...])
    n_groups = acc.shape[1] // LANES

    def rope(y):
        lane = lax.broadcasted_iota(I32, y.shape, 1)
        first_half = (lane % HEAD_DIM) < (HEAD_DIM // 2)
        sw = jnp.where(first_half, pltpu.roll(y, LANES - HEAD_DIM // 2, 1),
                       pltpu.roll(y, HEAD_DIM // 2, 1))
        return y * cos_ref[...] + sw * sin_ref[...]

    roped = sorted(rope_groups)

    for t in roped:
        @pl.when(j == t)
        def _(t=t):
            for gi in range(n_groups):
                y = acc[:, gi * LANES:(gi + 1) * LANES]
                if gi < rope_groups[t]:
                    y = rope(y)
                o_ref[:, gi * LANES:(gi + 1) * LANES] = y.astype(o_ref.dtype)

    is_plain = j >= 0
    for t in roped:
        is_plain = jnp.logical_and(is_plain, j != t)

    @pl.when(is_plain)
    def _():
        o_ref[...] = acc.astype(o_ref.dtype)


def _norm_proj(x, gain, w, cos, sin, seq, *, tm, tn, rope_groups):
    t, d = x.shape
    n = w.shape[1]
    pos_blocks = seq // tm
    kern = functools.partial(_norm_proj_kernel, rope_groups=rope_groups)
    return pl.pallas_call(
        kern,
        grid=(t // tm, n // tn),
        in_specs=[
            pl.BlockSpec((tm, d), lambda i, j: (i, 0)),
            pl.BlockSpec((1, d), lambda i, j: (0, 0)),
            pl.BlockSpec((d, tn), lambda i, j: (0, j)),
            pl.BlockSpec((tm, LANES), lambda i, j: (i % pos_blocks, 0)),
            pl.BlockSpec((tm, LANES), lambda i, j: (i % pos_blocks, 0)),
        ],
        out_specs=pl.BlockSpec((tm, tn), lambda i, j: (i, j)),
        out_shape=jax.ShapeDtypeStruct((t, n), BF16),
        scratch_shapes=[pltpu.VMEM((tm, d), BF16)],
        compiler_params=_params("parallel", "arbitrary"),
        name="norm_proj",
    )(x, gain, w, cos, sin)


MOBA_VROWS = HEAD_DIM + 16


def _moba_kernel(qt_ref, k_ref, vt_ref, o_ref, kmean_scr, kaug_scr, vaug_scr, qaug_scr, *, nblk):
    i = pl.program_id(1)
    blk = MOBA_BLOCK
    hd = HEAD_DIM
    aug = 2 * hd
    vr = MOBA_VROWS
    heads = range(A_HEADS)
    hsl = [slice(hh * hd, (hh + 1) * hd) for hh in heads]

    @pl.when(i == 0)
    def _():
        ones = jnp.ones((vr - hd, blk), BF16)
        blk_lane = lax.broadcasted_iota(I32, (blk, hd), 1)
        for n in range(nblk):
            rows = slice(n * blk, (n + 1) * blk)
            kmean_scr[n:n + 1, :] = jnp.mean(k_ref[rows, :].astype(F32), axis=0, keepdims=True)
            onehot = jnp.where(blk_lane == n, 1.0, 0.0).astype(BF16)
            for hh in heads:
                kaug_scr[rows, hh * aug:hh * aug + hd] = k_ref[rows, hsl[hh]]
                kaug_scr[rows, hh * aug + hd:(hh + 1) * aug] = onehot
                vaug_scr[n, hh * vr:hh * vr + hd, :] = vt_ref[n, hsl[hh], :]
                vaug_scr[n, hh * vr + hd:(hh + 1) * vr, :] = ones

    key = lax.broadcasted_iota(I32, (blk, blk), 0)
    qry = lax.broadcasted_iota(I32, (blk, blk), 1)
    causal = key <= qry
    blk_id = lax.broadcasted_iota(I32, (nblk, blk), 0)
    own0 = pl.multiple_of(i * blk, blk)

    init = []
    for hh in heads:
        qt = qt_ref[hsl[hh], :]
        km_hi, km_lo = _split_bf16(kmean_scr[:, hsl[hh]])
        g = _dot(km_hi, qt) + _dot(km_lo, qt)

        cnt = jnp.zeros((nblk, blk), I32)
        for m in range(nblk):
            gm = g[m:m + 1, :]
            beats = (gm > g) | ((gm == g) & (m < blk_id))
            cnt = cnt + jnp.where(beats, (m < i).astype(I32), 0)
        sel = (cnt < MOBA_TOPK) & (blk_id < i)
        qaug_scr[hh * aug:hh * aug + hd, :] = qt
        qaug_scr[hh * aug + hd:hh * aug + hd + nblk, :] = jnp.where(sel, 0.0, NEG).astype(BF16)
        qaug_scr[hh * aug + hd + nblk:(hh + 1) * aug, :] = jnp.zeros((hd - nblk, blk), BF16)

        s = _dot(k_ref[pl.ds(own0, blk), hsl[hh]], qt)
        s = jnp.where(causal, s, NEG)
        m0 = jnp.max(s, axis=0, keepdims=True)
        p = jnp.exp2(s - m0)
        init.append((m0, _dot(vaug_scr[i, hh * vr:(hh + 1) * vr, :], p.astype(BF16))))

    def step(blocks, carry):
        rows = [pl.ds(pl.multiple_of(j * blk, blk), blk) for j in blocks]
        ss = [[_dot(kaug_scr[r, hh * aug:(hh + 1) * aug], qaug_scr[hh * aug:(hh + 1) * aug, :])
               for r in rows] for hh in heads]
        ms = []
        for hh in heads:
            m_new = carry[hh][0]
            for s in ss[hh]:
                m_new = jnp.maximum(m_new, jnp.max(s, axis=0, keepdims=True))
            ms.append(m_new)
        ps = [[jnp.exp2(s - ms[hh]).astype(BF16) for s in ss[hh]] for hh in heads]
        pvs = []
        for hh in heads:
            pv = None
            for j, p in zip(blocks, ps[hh]):
                d = _dot(vaug_scr[j, hh * vr:(hh + 1) * vr, :], p)
                pv = d if pv is None else pv + d
            pvs.append(pv)
        return tuple((ms[hh], jnp.exp2(carry[hh][0] - ms[hh]) * carry[hh][1] + pvs[hh])
                     for hh in heads)

    fin = lax.fori_loop(0, i // 2, lambda jp, c: step([2 * jp, 2 * jp + 1], c), tuple(init))
    fin = lax.fori_loop(0, i % 2, lambda _, c: step([i - 1], c), fin)
    out_t = jnp.concatenate([acc[:hd, :] / acc[hd:hd + 1, :] for _, acc in fin], axis=0)
    o_ref[...] = out_t.T.astype(o_ref.dtype)


def _moba(p, batch, seq):
    nblk = seq // MOBA_BLOCK
    assert nblk <= HEAD_DIM
    qt = _blocks_t(p, COL_QA, A_W, batch, seq, MOBA_BLOCK)
    vt = _blocks_t(p, COL_VA, A_W, batch, seq, MOBA_BLOCK)
    kern = functools.partial(_moba_kernel, nblk=nblk)
    return pl.pallas_call(
        kern,
        grid=(batch, nblk),
        in_specs=[
            pl.BlockSpec((None, None, A_W, MOBA_BLOCK), lambda b, i: (b, i, 0, 0)),
            pl.BlockSpec((seq, A_W), lambda b, i: (b, COL_KA // A_W)),
            pl.BlockSpec((None, nblk, A_W, MOBA_BLOCK), lambda b, i: (b, 0, 0, 0)),
        ],
        out_specs=pl.BlockSpec((MOBA_BLOCK, A_W), lambda b, i: (b * nblk + i, 0)),
        out_shape=jax.ShapeDtypeStruct((batch * seq, A_W), BF16),
        scratch_shapes=[
            pltpu.VMEM((nblk, A_W), F32),
            pltpu.VMEM((seq, 2 * A_W), BF16),
            pltpu.VMEM((nblk, A_HEADS * MOBA_VROWS, MOBA_BLOCK), BF16),
            pltpu.VMEM((2 * A_W, MOBA_BLOCK), BF16),
        ],
        compiler_params=_params("parallel", "arbitrary"),
        name="moba",
    )(qt, p, vt)


def _swa_kernel(sinks_ref, q_ref, kv_ref, pkv_ref, o_ref, kv_scr, *, tq):
    i = pl.program_id(1)
    w = WINDOW
    kv_scr[0:w, :] = pkv_ref[...]
    kv_scr[w:, :] = kv_ref[...]

    row = lax.broadcasted_iota(I32, (w, 2 * w), 0)
    col = lax.broadcasted_iota(I32, (w, 2 * w), 1)
    rel = row + w - col
    band = (rel >= 0) & (rel < w)
    group = B_HEADS // B_KV_HEADS

    ones = jnp.ones((2 * w, HEAD_DIM), BF16)

    def body(n, carry):
        base = pl.multiple_of(n * w, w)
        kvt = kv_scr[pl.ds(base, 2 * w), :]
        qt = q_ref[pl.ds(base, w), :]
        col_min = jnp.where(jnp.logical_and(i == 0, n == 0), w, 0)
        mask = band & (col >= col_min)
        heads = [(kh, gi) for kh in range(B_KV_HEADS) for gi in range(group)]
        hcol = lambda hd: slice(hd * HEAD_DIM, (hd + 1) * HEAD_DIM)
        ss = []
        for kh in range(B_KV_HEADS):
            q4 = jnp.concatenate([qt[:, hcol(kh * group + gi)] for gi in range(group)], axis=0)
            ss.append(_dot_t(q4, kvt[:, hcol(kh)]))
        ms, ps = {}, {}
        for kh, gi in heads:
            s = jnp.where(mask, ss[kh][gi * w:(gi + 1) * w], NEG)
            m = jnp.maximum(jnp.max(s, axis=-1, keepdims=True), sinks_ref[kh * group + gi])
            ms[(kh, gi)] = m
            ps[(kh, gi)] = jnp.exp2(s - m).astype(BF16)
        pvs = []
        for kh in range(B_KV_HEADS):
            vaug = jnp.concatenate([kvt[:, B_KVW + kh * HEAD_DIM:B_KVW + (kh + 1) * HEAD_DIM], ones],
                                   axis=1)
            p4 = jnp.concatenate([ps[(kh, gi)] for gi in range(group)], axis=0)
            pvs.append(_dot(p4, vaug))
        outs = []
        for kh, gi in heads:
            pv = pvs[kh][gi * w:(gi + 1) * w]
            l = pv[:, HEAD_DIM:] + jnp.exp2(sinks_ref[kh * group + gi] - ms[(kh, gi)])
            outs.append(pv[:, :HEAD_DIM] / l)
        o_ref[pl.ds(base, w), :] = jnp.concatenate(outs, axis=1).astype(o_ref.dtype)
        return carry

    lax.fori_loop(0, tq // w, body, 0)


def _swa(p, sinks, batch, seq, *, tq=512):
    nq = seq // tq
    sub = tq // WINDOW
    kern = functools.partial(_swa_kernel, tq=tq)
    return pl.pallas_call(
        kern,
        grid=(batch, nq),
        in_specs=[
            pl.BlockSpec(memory_space=pltpu.SMEM),
            pl.BlockSpec((tq, B_QW), lambda b, i: (b * nq + i, COL_QB // B_QW)),
            pl.BlockSpec((tq, 2 * B_KVW), lambda b, i: (b * nq + i, COL_KB // (2 * B_KVW))),
            pl.BlockSpec((WINDOW, 2 * B_KVW),
                         lambda b, i: (jnp.maximum((b * nq + i) * sub - 1, 0),
                                       COL_KB // (2 * B_KVW))),
        ],
        out_specs=pl.BlockSpec((tq, B_QW), lambda b, i: (b * nq + i, 0)),
        out_shape=jax.ShapeDtypeStruct((batch * seq, B_QW), BF16),
        scratch_shapes=[pltpu.VMEM((tq + WINDOW, 2 * B_KVW), BF16)],
        compiler_params=_params("parallel", "parallel"),
        name="swa",
    )(sinks, p, p, p)


def _stick_kernel(qt_ref, k_ref, vt_ref, o_ref, z_scr, sp_scr, kpad_scr):
    i = pl.program_id(1)
    blk = SB_BLOCK
    hd = HEAD_DIM
    heads = range(C_HEADS)
    hsl = [slice(hh * hd, (hh + 1) * hd) for hh in heads]
    key = lax.broadcasted_iota(I32, (blk, blk), 0)
    qry = lax.broadcasted_iota(I32, (blk, blk), 1)
    causal = key < qry
    ntri = jnp.where(qry >= key, -1.0, 0.0).astype(BF16)
    own0 = pl.multiple_of(i * blk, blk)

    def exp(x):
        return jnp.exp2(x * LOG2E)

    def softplus2(z):
        return jnp.maximum(z, 0.0) + jnp.log(1.0 + jnp.exp2(jnp.abs(z) * -LOG2E))

    def suffix(sp):
        incl = _dot(ntri, sp.astype(BF16))
        return incl, incl[0:1, :]

    def weight(x):
        return exp(x).astype(BF16)

    @pl.when(i == 0)
    def _():
        pad = jnp.zeros((blk, LANES - hd), BF16)
        for n in range(k_ref.shape[0] // blk):
            rows = slice(n * blk, (n + 1) * blk)
            for hh in heads:
                kpad_scr[rows, hh * LANES:hh * LANES + hd] = k_ref[rows, hsl[hh]]
                kpad_scr[rows, hh * LANES + hd:(hh + 1) * LANES] = pad

    ksl = [slice(hh * LANES, (hh + 1) * LANES) for hh in heads]
    qts = [jnp.concatenate([qt_ref[hsl[hh], :], jnp.zeros((LANES - hd, blk), BF16)], axis=0)
           for hh in heads]
    init = []
    for hh in heads:
        z = _dot(kpad_scr[pl.ds(own0, blk), ksl[hh]], qts[hh])
        incl, carry0 = suffix(jnp.where(causal, softplus2(z), 0.0))
        wgt = jnp.where(causal, weight(z + incl), 0.0)
        init.append((carry0, _dot(vt_ref[i, hsl[hh], :], wgt.astype(BF16))))

    def keys(nb):
        return [(hh, b) for hh in heads for b in range(nb)]

    def logits(blocks):
        rows = [pl.ds(pl.multiple_of(j * blk, blk), blk) for j in blocks]
        return {(hh, b): _dot(kpad_scr[rows[b], ksl[hh]], qts[hh])
                for hh, b in keys(len(blocks))}

    def values(blocks, zs, incls):
        ws = {k: weight(zs[k] + incls[k]) for k in zs}
        return {(hh, b): _dot(vt_ref[blocks[b], hsl[hh], :], ws[(hh, b)]) for hh, b in zs}

    def accumulate(state, pvs, incls, nb):
        out = []
        for hh in heads:
            carry, acc = state[hh]
            for b in range(nb):
                acc = acc + pvs[(hh, b)] * exp(carry)
                carry = carry + incls[(hh, b)][0:1, :]
            out.append((carry, acc))
        return tuple(out)

    def single(j, state):
        zs = logits([j])
        incls = {k: suffix(softplus2(zs[k]))[0] for k in zs}
        return accumulate(state, values([j], zs, incls), incls, 1)

    odd = i % 2
    state = lax.fori_loop(0, odd, lambda _, s: single(i - 1, s), tuple(init))
    n_pairs = i // 2
    nearest = i - 1 - odd

    def pair(p):
        return [nearest - 2 * p, nearest - 1 - 2 * p]

    def slot_of(p, k):
        return (p % 2) * len(keys(2)) + keys(2).index(k)

    def stage_in(p):
        zs = logits(pair(p))
        for k in zs:
            z_scr[slot_of(p, k)] = zs[k]
            sp_scr[slot_of(p, k)] = softplus2(zs[k]).astype(BF16)

    def stage_out(p, state, prefetch):
        incls = {k: _dot(ntri, sp_scr[slot_of(p, k)]) for k in keys(2)}
        if prefetch:
            zs_next = logits(pair(p + 1))
        zs = {k: z_scr[slot_of(p, k)] for k in keys(2)}
        pvs = values(pair(p), zs, incls)
        if prefetch:
            for k in zs_next:
                z_scr[slot_of(p + 1, k)] = zs_next[k]
                sp_scr[slot_of(p + 1, k)] = softplus2(zs_next[k]).astype(BF16)
        return accumulate(state, pvs, incls, 2)

    @pl.when(n_pairs > 0)
    def _():
        stage_in(0)

    state = lax.fori_loop(0, n_pairs - 1, lambda p, s: stage_out(p, s, True), state)
    fin = lax.fori_loop(0, jnp.minimum(n_pairs, 1),
                        lambda _, s: stage_out(n_pairs - 1, s, False), state)
    out_t = jnp.concatenate([acc for _, acc in fin], axis=0)
    o_ref[...] = out_t.T.astype(o_ref.dtype)


def _blocks_t(p, col, width, batch, seq, blk):
    t = p[:, col:col + width].reshape(batch, seq // blk, blk, width)
    return jnp.swapaxes(t, 2, 3)


def _stick(p, batch, seq):
    nblk = seq // SB_BLOCK
    qt = _blocks_t(p, COL_QC, C_W, batch, seq, SB_BLOCK)
    vt = _blocks_t(p, COL_VC, C_W, batch, seq, SB_BLOCK)
    return pl.pallas_call(
        _stick_kernel,
        grid=(batch, nblk),
        in_specs=[
            pl.BlockSpec((None, None, C_W, SB_BLOCK), lambda b, i: (b, i, 0, 0)),
            pl.BlockSpec((seq, C_W), lambda b, i: (b, COL_KC // C_W)),
            pl.BlockSpec((None, nblk, C_W, SB_BLOCK), lambda b, i: (b, 0, 0, 0)),
        ],
        out_specs=pl.BlockSpec((SB_BLOCK, C_W), lambda b, i: (b * nblk + i, 0)),
        out_shape=jax.ShapeDtypeStruct((batch * seq, C_W), BF16),
        scratch_shapes=[pltpu.VMEM((4 * C_HEADS, SB_BLOCK, SB_BLOCK), F32),
                        pltpu.VMEM((4 * C_HEADS, SB_BLOCK, SB_BLOCK), BF16),
                        pltpu.VMEM((seq, C_HEADS * LANES), BF16)],
        compiler_params=_params("parallel", "arbitrary"),
        name="stick",
    )(qt, p, vt)


def _sigmoid(x):
    return 0.5 * jnp.tanh(0.5 * x) + 0.5


def _merge_kernel(x_ref, oa_ref, ob_ref, oc_ref, ga_ref, gb_ref, gc_ref,
                  wa_ref, wb_ref, wc_ref, wo_ref, o_ref):
    mixed = _sigmoid(ga_ref[...].astype(F32)) * _dot(oa_ref[...], wa_ref[...])
    mixed = mixed + _sigmoid(gb_ref[...].astype(F32)) * _dot(ob_ref[...], wb_ref[...])
    mixed = mixed + _sigmoid(gc_ref[...].astype(F32)) * _dot(oc_ref[...], wc_ref[...])
    o_ref[...] = x_ref[...] + _dot(mixed.astype(BF16), wo_ref[...])


def _merge(x, oa, ob, oc, p, wa, wb, wc, wo, *, tm=512):
    t, d = x.shape
    full = lambda a: pl.BlockSpec(a.shape, lambda i: (0, 0))
    return pl.pallas_call(
        _merge_kernel,
        grid=(t // tm,),
        in_specs=[
            pl.BlockSpec((tm, d), lambda i: (i, 0)),
            pl.BlockSpec((tm, A_W), lambda i: (i, 0)),
            pl.BlockSpec((tm, B_QW), lambda i: (i, 0)),
            pl.BlockSpec((tm, C_W), lambda i: (i, 0)),
            pl.BlockSpec((tm, d), lambda i: (i, COL_GA // D_MODEL)),
            pl.BlockSpec((tm, d), lambda i: (i, COL_GB // D_MODEL)),
            pl.BlockSpec((tm, d), lambda i: (i, COL_GC // D_MODEL)),
            full(wa), full(wb), full(wc), full(wo),
        ],
        out_specs=pl.BlockSpec((tm, d), lambda i: (i, 0)),
        out_shape=jax.ShapeDtypeStruct((t, d), F32),
        compiler_params=_params("parallel"),
        name="merge",
    )(x, oa, ob, oc, p, p, p, wa, wb, wc, wo)


def _cross_kernel(x_ref, g_ref, wq_ref, kv_ref, wo_ref, o_ref, *rows_ref):
    x = x_ref[...]
    h = _rms(x, g_ref[...]).astype(BF16)
    q = (_dot(h, wq_ref[...]) * (X_HEAD_DIM ** -0.5)).astype(BF16)
    outs = []
    for hd in range(X_HEADS):
        hs = slice(hd * X_HEAD_DIM, (hd + 1) * X_HEAD_DIM)
        s = _dot_t(q[:, hs], kv_ref[:, hs])
        m = jnp.max(s, axis=-1, keepdims=True)
        p = jnp.exp(s - m)
        l = jnp.sum(p, axis=-1, keepdims=True)
        vs = slice(X_W + hd * X_HEAD_DIM, X_W + (hd + 1) * X_HEAD_DIM)
        outs.append(_dot(p.astype(BF16), kv_ref[:, vs]) / l)
    att = jnp.concatenate(outs, axis=1).astype(BF16)
    o_ref[...] = x + _dot(att, wo_ref[...])
    if rows_ref:
        tm, d = o_ref.shape
        for c in range(d // LANES):
            rows_ref[0][pl.ds(c, tm, stride=SUBLANES), :] = o_ref[:, c * LANES:(c + 1) * LANES]


def _cross(x, gain, wq, kv, wo, seq, *, tm=512, with_row_tiles=False):
    t, d = x.shape
    per_batch = seq // tm
    full = lambda a: pl.BlockSpec(a.shape, lambda i: (0, 0))
    out_specs = [pl.BlockSpec((tm, d), lambda i: (i, 0))]
    out_shape = [jax.ShapeDtypeStruct((t, d), F32)]
    if with_row_tiles:
        out_specs.append(pl.BlockSpec((tm * SUBLANES, LANES), lambda i: (i, 0)))
        out_shape.append(jax.ShapeDtypeStruct((t * SUBLANES, LANES), F32))
    outs = pl.pallas_call(
        _cross_kernel,
        grid=(t // tm,),
        in_specs=[
            pl.BlockSpec((tm, d), lambda i: (i, 0)),
            full(gain), full(wq),
            pl.BlockSpec((MEM_LEN, 2 * X_W), lambda i: (i // per_batch, 0)),
            full(wo),
        ],
        out_specs=out_specs,
        out_shape=out_shape,
        compiler_params=_params("parallel"),
        name="cross",
    )(x, gain, wq, kv, wo)
    return outs if with_row_tiles else outs[0]


def _silu(x):
    return x * _sigmoid(x)


def _ffn_kernel(x_ref, g_ref, wg_ref, wu_ref, wd_ref, o_ref, h_scr):
    f = pl.program_id(1)

    @pl.when(f == 0)
    def _():
        h_scr[...] = _rms(x_ref[...], g_ref[...]).astype(BF16)
        o_ref[...] = x_ref[...]

    h = h_scr[...]
    a = _silu(_dot(h, wg_ref[...])) * _dot(h, wu_ref[...])
    o_ref[...] += _dot(a.astype(BF16), wd_ref[...])


def _ffn(x, gain, wg, wu, wd, *, tm=2048, tf=256):
    t, d = x.shape
    ff = wd.shape[0]
    return pl.pallas_call(
        _ffn_kernel,
        grid=(t // tm, ff // tf),
        in_specs=[
            pl.BlockSpec((tm, d), lambda i, f: (i, 0)),
            pl.BlockSpec((1, d), lambda i, f: (0, 0)),
            pl.BlockSpec((d, tf), lambda i, f: (0, f)),
            pl.BlockSpec((d, tf), lambda i, f: (0, f)),
            pl.BlockSpec((tf, d), lambda i, f: (f, 0)),
        ],
        out_specs=pl.BlockSpec((tm, d), lambda i, f: (i, 0)),
        out_shape=jax.ShapeDtypeStruct((t, d), F32),
        scratch_shapes=[pltpu.VMEM((tm, d), BF16)],
        compiler_params=_params("parallel", "arbitrary"),
        name="ffn",
    )(x, gain, wg, wu, wd)


def _router_kernel(x_ref, g_ref, wr_ref, o_ref):
    h = _rms(x_ref[...], g_ref[...])
    h_hi, h_lo = _split_bf16(h)
    w_hi, w_lo = _split_bf16(wr_ref[...])
    logits = _dot(h_hi, w_hi) + (_dot(h_hi, w_lo) + _dot(h_lo, w_hi))
    lane = lax.broadcasted_iota(I32, logits.shape, 1)
    logits = jnp.where(lane < N_EXPERTS, logits, NEG)
    v1 = jnp.max(logits, axis=-1, keepdims=True)
    i1 = jnp.min(jnp.where(logits == v1, lane, LANES), axis=-1, keepdims=True)
    rest = jnp.where(lane == i1, NEG, logits)
    v2 = jnp.max(rest, axis=-1, keepdims=True)
    i2 = jnp.min(jnp.where(rest == v2, lane, LANES), axis=-1, keepdims=True)
    e = jnp.exp(v2 - v1)
    w1 = 1.0 / (1.0 + e)
    w2 = e / (1.0 + e)
    out = jnp.where(lane == 0, i1.astype(F32), 0.0)
    out = jnp.where(lane == 1, i2.astype(F32), out)
    out = jnp.where(lane == 2, w1, out)
    out = jnp.where(lane == 3, w2, out)
    o_ref[...] = out


def _router(x, gain, wr, *, tm=1024):
    t, d = x.shape
    return pl.pallas_call(
        _router_kernel,
        grid=(t // tm,),
        in_specs=[
            pl.BlockSpec((tm, d), lambda i: (i, 0)),
            pl.BlockSpec((1, d), lambda i: (0, 0)),
            pl.BlockSpec((d, LANES), lambda i: (0, 0)),
        ],
        out_specs=pl.BlockSpec((tm, LANES), lambda i: (i, 0)),
        out_shape=jax.ShapeDtypeStruct((t, LANES), F32),
        compiler_params=_params("parallel"),
        name="router",
    )(x, gain, wr)


def _moe_kernel(tok_ref, dst_ref, texp_ref, tval_ref, x_hbm, g_ref, wg_ref, wu_ref, wd_ref,
                y_hbm, xbuf, obuf, h_scr, acc_scr, gsem, ssem, *, tm, nf):
    m = pl.program_id(0)
    f = pl.program_id(1)
    last_phase = pl.num_programs(0) - 1
    slot = m % 2
    other = 1 - slot
    valid = tval_ref[m] > 0
    per_step = (tm // nf) // 8 * 8
    in_steps = per_step * nf

    def row_tile(r):
        return pl.ds(pl.multiple_of(r * SUBLANES, SUBLANES), SUBLANES)

    def gather_row(tile, r, s):
        tok = pl.multiple_of(tok_ref[tile * tm + r], SUBLANES)
        return pltpu.make_async_copy(x_hbm.at[pl.ds(tok, SUBLANES), :], xbuf.at[s, row_tile(r), :],
                                     gsem.at[s])

    def scatter_row(phase, r, s):
        dst = pl.multiple_of(dst_ref[phase * tm + r], SUBLANES)
        return pltpu.make_async_copy(obuf.at[s, row_tile(r), :], y_hbm.at[pl.ds(dst, SUBLANES), :],
                                     ssem.at[s])

    def chunk(c):
        return pl.ds(c, tm, stride=SUBLANES)

    wait_group = 16

    def gather_wait(tile, s):
        def body(it, c):
            for u in range(wait_group):
                gather_row(tile, it * wait_group + u, s).wait()
            return c
        lax.fori_loop(0, tm // wait_group, body, 0)

    def scatter_wait(phase, s):
        def body(it, c):
            for u in range(wait_group):
                scatter_row(phase, it * wait_group + u, s).wait()
            return c
        lax.fori_loop(0, tm // wait_group, body, 0)

    next_valid = tval_ref[m + 1] > 0
    prev_valid = jnp.logical_and(m >= 1, tval_ref[jnp.maximum(m - 1, 0)] > 0)
    prev2_valid = jnp.logical_and(m >= 2, tval_ref[jnp.maximum(m - 2, 0)] > 0)
    pad_only = jnp.logical_and(jnp.logical_not(valid), m < last_phase)

    def pad_fill():
        first = pl.multiple_of(dst_ref[(m + 1) * tm], SUBLANES)
        return pltpu.make_async_copy(obuf.at[slot], y_hbm.at[pl.ds(first, tm * SUBLANES), :],
                                     ssem.at[slot])

    @pl.when(jnp.logical_and(f == 0, prev2_valid))
    def _():
        scatter_wait(m - 1, slot)

    @pl.when(jnp.logical_and(f == 0, pad_only))
    def _():
        obuf[slot] = jnp.zeros(obuf.shape[1:], obuf.dtype)
        pad_fill().start()

    @pl.when(jnp.logical_and(f == 0, next_valid))
    def _():
        for r in range(in_steps, tm):
            gather_row(m + 1, r, other).start(priority=r % 2)

    @pl.when(jnp.logical_and(f == 0, prev_valid))
    def _():
        for r in range(in_steps, tm):
            scatter_row(m, r, other).start(priority=r % 2)

    @pl.when(jnp.logical_and(f == 0, valid))
    def _():
        @pl.when(m == 0)
        def _():
            def first(r, c):
                gather_row(0, r, 0).start()
                return c
            lax.fori_loop(0, tm, first, 0, unroll=8)

        gather_wait(m, slot)
        n_chunks = h_scr.shape[1] // LANES
        xs = [xbuf[slot, chunk(c), :] for c in range(n_chunks)]
        ssq = xs[0] * xs[0]
        for xc in xs[1:]:
            ssq = ssq + xc * xc
        inv = lax.rsqrt(jnp.sum(ssq, axis=-1, keepdims=True) / h_scr.shape[1] + EPS)
        for c in range(n_chunks):
            lanes = slice(c * LANES, (c + 1) * LANES)
            h_scr[:, lanes] = (xs[c] * inv * g_ref[:, lanes]).astype(BF16)
        acc_scr[...] = jnp.zeros_like(acc_scr)

    def expert_step():
        h = h_scr[...]
        a = _silu(_dot(h, wg_ref[...].astype(BF16))) * _dot(h, wu_ref[...].astype(BF16))
        acc_scr[...] += _dot(a.astype(BF16), wd_ref[...].astype(BF16))

    def gather_rows():
        r0 = pl.multiple_of(f * per_step, 8)
        for rr in range(per_step):
            gather_row(m + 1, r0 + rr, other).start(priority=rr % 2)

    def scatter_rows():
        r0 = pl.multiple_of(f * per_step, 8)
        for rr in range(per_step):
            scatter_row(m, r0 + rr, other).start(priority=rr % 2)

    interior = jnp.logical_and(valid, jnp.logical_and(next_valid, prev_valid))
    edge = jnp.logical_not(interior)

    @pl.when(interior)
    def _():
        gather_rows()
        scatter_rows()
        expert_step()

    @pl.when(jnp.logical_and(edge, valid))
    def _():
        expert_step()

    @pl.when(jnp.logical_and(edge, next_valid))
    def _():
        gather_rows()

    @pl.when(jnp.logical_and(edge, prev_valid))
    def _():
        scatter_rows()

    @pl.when(jnp.logical_and(f == nf - 1, valid))
    def _():
        for c in range(acc_scr.shape[1] // LANES):
            obuf[slot, chunk(c), :] = acc_scr[:, c * LANES:(c + 1) * LANES]

    @pl.when(jnp.logical_and(f == nf - 1, pad_only))
    def _():
        pad_fill().wait()

    @pl.when(jnp.logical_and(jnp.logical_and(f == nf - 1, m == last_phase), prev_valid))
    def _():
        scatter_wait(m, other)


def _moe_experts(x_rows, gain, wg, wu, wd, tok, dst, tile_expert, tile_valid, n_rows, *, tm, tf):
    d = gain.shape[1]
    assert d == SUBLANES * LANES
    nf = wg.shape[2] // tf
    n_phases = tile_expert.shape[0]
    kern = functools.partial(_moe_kernel, tm=tm, nf=nf)

    def fsel(m, f, tval):
        return jnp.where(tval[m] > 0, f, nf - 1)

    grid_spec = pltpu.PrefetchScalarGridSpec(
        num_scalar_prefetch=4,
        grid=(n_phases, nf),
        in_specs=[
            pl.BlockSpec(memory_space=pl.ANY),
            pl.BlockSpec((1, d), lambda m, f, tok, dst, texp, tval: (0, 0)),
            pl.BlockSpec((None, d, tf),
                         lambda m, f, tok, dst, texp, tval: (texp[m], 0, fsel(m, f, tval))),
            pl.BlockSpec((None, d, tf),
                         lambda m, f, tok, dst, texp, tval: (texp[m], 0, fsel(m, f, tval))),
            pl.BlockSpec((None, tf, d),
                         lambda m, f, tok, dst, texp, tval: (texp[m], fsel(m, f, tval), 0)),
        ],
        out_specs=pl.BlockSpec(memory_space=pl.ANY),
        scratch_shapes=[
            pltpu.VMEM((2, tm * SUBLANES, LANES), F32),
            pltpu.VMEM((2, tm * SUBLANES, LANES), F32),
            pltpu.VMEM((tm, d), BF16),
            pltpu.VMEM((tm, d), F32),
            pltpu.SemaphoreType.DMA((2,)),
            pltpu.SemaphoreType.DMA((2,)),
        ],
    )
    return pl.pallas_call(
        kern,
        grid_spec=grid_spec,
        out_shape=jax.ShapeDtypeStruct((n_rows * SUBLANES, LANES), F32),
        compiler_params=_params("arbitrary", "arbitrary"),
        name="moe_experts",
    )(tok, dst, tile_expert, tile_valid, x_rows, gain, wg, wu, wd)


def _combine_kernel(x_ref, r_ref, y0_ref, y1_ref, g_ref, o_ref, *, out_norm):
    tm, d = x_ref.shape
    route = r_ref[...]
    w0 = route[:, TOP_K:TOP_K + 1]
    w1 = route[:, TOP_K + 1:TOP_K + 2]
    for c in range(d // LANES):
        lanes = slice(c * LANES, (c + 1) * LANES)
        rows = pl.ds(c, tm, stride=SUBLANES)
        o_ref[:, lanes] = x_ref[:, lanes] + w0 * y0_ref[rows, :] + w1 * y1_ref[rows, :]
    if out_norm:
        o_ref[...] = _rms(o_ref[...], g_ref[...])


def _moe_combine(x, route, y_rows, out_gain, *, tm=512):
    t, d = x.shape
    assert TOP_K == 2
    out_norm = out_gain is not None
    gain = out_gain if out_norm else jnp.ones((1, d), F32)
    return pl.pallas_call(
        functools.partial(_combine_kernel, out_norm=out_norm),
        grid=(t // tm,),
        in_specs=[
            pl.BlockSpec((tm, d), lambda i: (i, 0)),
            pl.BlockSpec((tm, LANES), lambda i: (i, 0)),
            pl.BlockSpec((tm * SUBLANES, LANES), lambda i: (i, 0)),
            pl.BlockSpec((tm * SUBLANES, LANES), lambda i: (t // tm + i, 0)),
            pl.BlockSpec((1, d), lambda i: (0, 0)),
        ],
        out_specs=pl.BlockSpec((tm, d), lambda i: (i, 0)),
        out_shape=jax.ShapeDtypeStruct((t, d), F32),
        compiler_params=_params("parallel"),
        name="moe_combine",
    )(x, route, y_rows, y_rows, gain)


def _moe(x, x_rows, gain, w_router, wg, wu, wd, out_gain=None, *, tm=MOE_TM, tf=MOE_TF):
    t, d = x.shape
    wr = jnp.zeros((d, LANES), F32).at[:, :N_EXPERTS].set(w_router)
    route = _router(x, gain, wr)
    tok, dst, tile_expert, tile_valid, n_rows = _route_metadata(route[:, :TOP_K].astype(I32), tm)
    y_rows = _moe_experts(x_rows, gain, wg, wu, wd, tok * SUBLANES, dst * SUBLANES,
                          tile_expert, tile_valid, n_rows, tm=tm, tf=tf)
    return _moe_combine(x, route, y_rows, out_gain)


def _route_metadata(experts, tm):
    t = experts.shape[0]
    n_slots = TOP_K * t
    eids = jnp.arange(N_EXPERTS, dtype=I32)
    eflat = experts.T.reshape(-1)
    onehot = (eflat[:, None] == eids[None, :]).astype(I32)
    csum = jnp.cumsum(onehot, axis=0)
    rank = jnp.sum((csum - 1) * onehot, axis=1)
    counts = csum[-1]
    tiles_per = (counts + tm - 1) // tm
    tile_end = jnp.cumsum(tiles_per)
    start = (tile_end - tiles_per) * tm
    pos = (jnp.sum(start[None, :] * onehot, axis=1) + rank).astype(I32)
    n_tiles = n_slots // tm + N_EXPERTS
    n_rows_pad = n_tiles * tm
    slot_of_row = jnp.full((n_rows_pad,), -1, I32).at[pos].set(jnp.arange(n_slots, dtype=I32))
    real = slot_of_row >= 0
    tok = jnp.concatenate([jnp.where(real, slot_of_row % t, 0), jnp.zeros((2 * tm,), I32)])
    dump = n_slots + jnp.cumsum(jnp.logical_not(real).astype(I32)) - 1
    dst = jnp.concatenate([jnp.zeros((tm,), I32), jnp.where(real, slot_of_row, dump)])
    tile_id = jnp.arange(n_tiles + 2, dtype=I32)
    tile_expert = jnp.sum((tile_id[:, None] >= tile_end[None, :]).astype(I32), axis=1)
    tile_valid = (tile_id < tile_end[-1]).astype(I32)
    last_expert = jnp.max(jnp.where(counts > 0, eids, 0))
    tile_expert = jnp.where(tile_valid > 0, tile_expert, last_expert).astype(I32)
    return tok, dst.astype(I32), tile_expert[:n_tiles + 1], tile_valid, n_rows_pad


def _final_norm_kernel(x_ref, g_ref, o_ref):
    o_ref[...] = _rms(x_ref[...], g_ref[...])


def _final_norm(x, gain, *, tm=1024):
    t, d = x.shape
    return pl.pallas_call(
        _final_norm_kernel,
        grid=(t // tm,),
        in_specs=[pl.BlockSpec((tm, d), lambda i: (i, 0)), pl.BlockSpec((1, d), lambda i: (0, 0))],
        out_specs=pl.BlockSpec((tm, d), lambda i: (i, 0)),
        out_shape=jax.ShapeDtypeStruct((t, d), F32),
        compiler_params=_params("parallel"),
        name="final_norm",
    )(x, gain)


def _rope_tables(seq):
    half = HEAD_DIM // 2
    inv_freq = ROPE_THETA ** (-jnp.arange(half, dtype=F32) / half)
    ang = jnp.arange(seq, dtype=F32)[:, None] * inv_freq[None, :]
    cos, sin = jnp.cos(ang), jnp.sin(ang)
    reps = LANES // HEAD_DIM
    cos_t = jnp.tile(jnp.concatenate([cos, cos], axis=1), (1, reps))
    sin_t = jnp.tile(jnp.concatenate([-sin, sin], axis=1), (1, reps))
    return cos_t, sin_t


def _in_proj_weight(w_in):
    o = 0
    offs = {}
    for name, width in (("qa", A_W), ("ka", A_W), ("va", A_W), ("qb", B_QW), ("kb", B_KVW),
                        ("vb", B_KVW), ("qc", C_W), ("kc", C_W), ("vc", C_W), ("g", 3 * D_MODEL)):
        offs[name] = (o, o + width)
        o += width
    q_scale = {"qa": HEAD_DIM ** -0.5 * LOG2E, "qb": HEAD_DIM ** -0.5 * LOG2E,
               "qc": HEAD_DIM ** -0.5}
    parts = []
    for name in ("g", "qa", "ka", "qb", "kb", "vb", "va", "qc", "kc", "vc"):
        lo, hi = offs[name]
        blk = w_in[:, lo:hi]
        if name in q_scale:
            blk = blk * q_scale[name]
        parts.append(blk)
    return jnp.concatenate(parts, axis=1).astype(BF16)


def _rope_group_map():
    groups = {}
    for t in range(IN_W // PROJ_TN):
        lo, hi = t * PROJ_TN, (t + 1) * PROJ_TN
        n = (min(hi, ROPE_HI) - max(lo, ROPE_LO)) // LANES
        if n > 0:
            assert max(lo, ROPE_LO) == lo
            groups[t] = n
    return groups


def kernel(x, mem, norm_mix, w_in, w_proj_a, w_proj_b, w_proj_c, w_mix_out, sinks, norm_cross,
           norm_mem, w_xq, w_xkv, w_xo, norm_ffn, ffn_gate, ffn_up, ffn_down, moe_router,
           moe_gate, moe_up, moe_down, final_norm):
    batch, seq, d = x.shape
    depth = norm_mix.shape[0]
    assert d == D_MODEL and seq % PROJ_TM == 0 and mem.shape[1] == MEM_LEN
    t = batch * seq
    xf = x.reshape(t, d)
    memf = mem.reshape(batch * MEM_LEN, d)
    cos_t, sin_t = _rope_tables(seq)
    rope_groups = _rope_group_map()
    ones_tab = jnp.ones((MEM_LEN, LANES), F32)

    for l in range(depth):
        gain = lambda g: g[l].reshape(1, d)
        p = _norm_proj(xf, gain(norm_mix), _in_proj_weight(w_in[l]), cos_t, sin_t, seq,
                       tm=PROJ_TM, tn=PROJ_TN, rope_groups=rope_groups)
        oa = _moba(p, batch, seq)
        ob = _swa(p, sinks[l] * LOG2E, batch, seq)
        oc = _stick(p, batch, seq)
        xf = _merge(xf, oa, ob, oc, p, w_proj_a[l].astype(BF16), w_proj_b[l].astype(BF16),
                    w_proj_c[l].astype(BF16), w_mix_out[l].astype(BF16))
        kv = _norm_proj(memf, gain(norm_mem), w_xkv[l].astype(BF16), ones_tab, ones_tab, MEM_LEN,
                        tm=MEM_LEN, tn=2 * X_W, rope_groups={})
        cross_args = (gain(norm_cross), w_xq[l].astype(BF16), kv, w_xo[l].astype(BF16), seq)
        if l % 2 == 0:
            i = l // 2
            xf = _cross(xf, *cross_args)
            xf = _ffn(xf, gain(norm_ffn), ffn_gate[i].astype(BF16), ffn_up[i].astype(BF16),
                      ffn_down[i].astype(BF16), tf=FFN_TF)
        else:
            i = l // 2
            xf, x_rows = _cross(xf, *cross_args, with_row_tiles=True)
            out_gain = final_norm.reshape(1, d) if l == depth - 1 else None
            xf = _moe(xf, x_rows, gain(norm_ffn), moe_router[i], moe_gate[i], moe_up[i],
                      moe_down[i], out_gain)
    if depth % 2 == 1:
        xf = _final_norm(xf, final_norm.reshape(1, d))
    return xf.reshape(batch, seq, d)
```
